```python
import jax, jax.numpy as jnp
from jax import lax
import numpy as np

D_MODEL = 1024
BATCH = 8
SEQ = 2048
DEPTH = 1
DEC_BATCH = 32
DEC_SEQ = 8
PAST_LEN = 16384
PAGE_SIZE = 128

CONV_WIDTH = D_MODEL // 2
CONV_K = 3
N_HEADS = 8
NOPE_DIM = 64
ROPE_DIM = 32
V_DIM = 64
Q_RANK = D_MODEL // 4
KV_RANK = D_MODEL // 8
ROPE_BASE = 10000.0
MIX_WIDTH = CONV_WIDTH + N_HEADS * V_DIM
OUT_GROUP_DIM = 64
N_OUT_GROUPS = MIX_WIDTH // OUT_GROUP_DIM
PROJ_WIDTH = 3 * CONV_WIDTH + Q_RANK + KV_RANK + ROPE_DIM
N_GROUPS = 4
EXPERTS_PER_GROUP = 8
N_EXPERTS = N_GROUPS * EXPERTS_PER_GROUP
TOP_K = 2
D_EXPERT = D_MODEL // 4
EPS = 1e-6
Q_BLOCK = 128

kernel_name = "hymba_conv_mla_hmoe_step"


def _rmsnorm(x, g):
    xf = x.astype(jnp.float32)
    y = xf * lax.rsqrt(jnp.mean(xf * xf, axis=-1, keepdims=True) + EPS)
    return (y * g.astype(jnp.float32)).astype(x.dtype)


def _rope(x, pos):
    half = ROPE_DIM // 2
    inv_freq = ROPE_BASE ** (-jnp.arange(half, dtype=jnp.float32) / half)
    ang = pos.astype(jnp.float32)[:, None] * inv_freq[None, :]
    ang = ang.reshape(ang.shape[:1] + (1,) * (x.ndim - 3) + (half,))
    cos, sin = jnp.cos(ang), jnp.sin(ang)
    xf = x.astype(jnp.float32)
    x1, x2 = xf[..., :half], xf[..., half:]
    return jnp.concatenate([x1 * cos - x2 * sin, x1 * sin + x2 * cos], axis=-1).astype(x.dtype)


def _project(h, pos, w_in, g_q_lat, w_uq, g_kv_lat, g_q_nope, g_q_rope, g_k_rope):
    z = h @ w_in
    c = CONV_WIDTH
    u_in, gate_b, gate_c = z[..., :c], z[..., c:2 * c], z[..., 2 * c:3 * c]
    o = 3 * c
    q_lat = z[..., o:o + Q_RANK]
    o += Q_RANK
    kv_lat = z[..., o:o + KV_RANK]
    o += KV_RANK
    k_rope_raw = z[..., o:o + ROPE_DIM]
    q = (_rmsnorm(q_lat, g_q_lat) @ w_uq).reshape(q_lat.shape[:-1] + (N_HEADS, NOPE_DIM + ROPE_DIM))
    q_nope = _rmsnorm(q[..., :NOPE_DIM], g_q_nope)
    q_rope = _rope(_rmsnorm(q[..., NOPE_DIM:], g_q_rope), pos)
    c_kv = _rmsnorm(kv_lat, g_kv_lat)
    k_rope = _rope(_rmsnorm(k_rope_raw, g_k_rope), pos)
    return u_in, gate_b, gate_c, q_nope, q_rope, c_kv, k_rope


def _short_conv(u_in, gate_b, gate_c, buf, conv_w, conv_b):
    u = gate_c * u_in
    ext = jnp.concatenate([buf.astype(u.dtype), u], axis=1)
    t = u.shape[1]
    v = conv_b
    for k in range(CONV_K):
        v = v + conv_w[k] * ext[:, k:k + t]
    return gate_b * v, ext[:, t:]


def _kv_expand(c_kv, w_uk, w_uv, g_k_nope):
    lead = c_kv.shape[:-1]
    k_nope = _rmsnorm((c_kv @ w_uk).reshape(lead + (N_HEADS, NOPE_DIM)), g_k_nope)
    v = (c_kv @ w_uv).reshape(lead + (N_HEADS, V_DIM))
    return k_nope, v


def _attend(q_nope, q_rope, k_nope, k_rope, v, q_pos, k_pos):
    scale = (NOPE_DIM + ROPE_DIM) ** -0.5
    s = (jnp.einsum('bthd,bshd->bhts', q_nope, k_nope)
         + jnp.einsum('bthd,bsd->bhts', q_rope, k_rope)).astype(jnp.float32) * scale
    s = jnp.where(k_pos[None, :] <= q_pos[:, None], s, -1e30)
    p = jax.nn.softmax(s, axis=-1).astype(v.dtype)
    return jnp.einsum('bhts,bshd->bthd', p, v)


def _merge(y_conv, y_attn, g_out, w_out):
    y = jnp.concatenate([y_conv, y_attn.reshape(y_attn.shape[:-2] + (N_HEADS * V_DIM,))], axis=-1)
    lead = y.shape[:-1]
    y = _rmsnorm(y.reshape(lead + (N_OUT_GROUPS, OUT_GROUP_DIM)),
                 g_out.reshape(N_OUT_GROUPS, OUT_GROUP_DIM)).reshape(lead + (MIX_WIDTH,))
    return y @ w_out


def _hier_moe(h, w_rg, b_rg, w_re, b_re, w_gate, w_up, w_down):
    lead = h.shape[:-1]
    t = h.reshape(-1, D_MODEL)
    p_group = jax.nn.softmax((t @ w_rg + b_rg).astype(jnp.float32), axis=-1)
    p_sel, g_sel = lax.top_k(p_group, 1)
    e_logits = (t @ w_re + b_re).astype(jnp.float32).reshape(-1, N_GROUPS, EXPERTS_PER_GROUP)
    e_in_group = jnp.take_along_axis(e_logits, g_sel[:, :, None], axis=1)[:, 0]
    top_vals, top_idx = lax.top_k(e_in_group, TOP_K)
    w = jax.nn.softmax(top_vals, axis=-1) * p_sel
    e_idx = g_sel * EXPERTS_PER_GROUP + top_idx
    gate = jnp.sum(w[..., None] * jax.nn.one_hot(e_idx, N_EXPERTS, dtype=jnp.float32), axis=1).astype(h.dtype)
    y = jnp.zeros_like(t)
    for e in range(N_EXPERTS):
        a = jax.nn.silu(t @ w_gate[e]) * (t @ w_up[e])
        y = y + gate[:, e:e + 1] * (a @ w_down[e])
    return y.reshape(lead + (D_MODEL,))


def setup_inputs(seed: int = 0) -> dict:
    key = jax.random.key(seed)
    ks = jax.random.split(key, 32)
    n_pages = PAST_LEN // PAGE_SIZE
    n_pool = (DEC_BATCH * n_pages * 5) // 4
    L = DEPTH

    def nrm(k, shape, scale):
        return jax.random.normal(k, shape, jnp.float32) * scale

    def gain(k, shape):
        return 1.0 + 0.01 * jax.random.normal(k, shape, jnp.float32)

    page_table = jax.random.permutation(ks[5], n_pool)[:DEC_BATCH * n_pages]
    page_table = page_table.reshape(DEC_BATCH, n_pages).astype(jnp.int32)
    return {
        "x_prompt": nrm(ks[0], (BATCH, SEQ, D_MODEL), 1.0),
        "x_sample": nrm(ks[1], (DEC_BATCH, DEC_SEQ, D_MODEL), 1.0),
        "state_conv": nrm(ks[2], (L, DEC_BATCH, CONV_K - 1, CONV_WIDTH), 1.0),
        "cache_ckv": nrm(ks[3], (L, n_pool, PAGE_SIZE, KV_RANK), 1.0),
        "cache_krope": nrm(ks[4], (L, n_pool, PAGE_SIZE, ROPE_DIM), 1.0),
        "page_table": page_table,
        "g_mix": gain(ks[6], (L, D_MODEL)),
        "w_in": nrm(ks[7], (L, D_MODEL, PROJ_WIDTH), D_MODEL ** -0.5),
        "conv_w": nrm(ks[8], (L, CONV_K, CONV_WIDTH), 0.5),
        "conv_b": nrm(ks[9], (L, CONV_WIDTH), 0.01),
        "g_q_lat": gain(ks[10], (L, Q_RANK)),
        "w_uq": nrm(ks[11], (L, Q_RANK, N_HEADS * (NOPE_DIM + ROPE_DIM)), Q_RANK ** -0.5),
        "g_kv_lat": gain(ks[12], (L, KV_RANK)),
        "w_uk": nrm(ks[13], (L, KV_RANK, N_HEADS * NOPE_DIM), KV_RANK ** -0.5),
        "w_uv": nrm(ks[14], (L, KV_RANK, N_HEADS * V_DIM), KV_RANK ** -0.5),
        "g_q_nope": gain(ks[15], (L, NOPE_DIM)),
        "g_q_rope": gain(ks[16], (L, ROPE_DIM)),
        "g_k_nope": gain(ks[17], (L, NOPE_DIM)),
        "g_k_rope": gain(ks[18], (L, ROPE_DIM)),
        "g_out": gain(ks[19], (L, MIX_WIDTH)),
        "w_out": nrm(ks[20], (L, MIX_WIDTH, D_MODEL), MIX_WIDTH ** -0.5),
        "g_ffn": gain(ks[21], (L, D_MODEL)),
        "w_router_group": nrm(ks[22], (L, D_MODEL, N_GROUPS), D_MODEL ** -0.5),
        "b_router_group": nrm(ks[23], (L, N_GROUPS), 0.01),
        "w_router_expert": nrm(ks[24], (L, D_MODEL, N_EXPERTS), D_MODEL ** -0.5),
        "b_router_expert": nrm(ks[25], (L, N_EXPERTS), 0.01),
        "w_gate": nrm(ks[26], (L, N_EXPERTS, D_MODEL, D_EXPERT), D_MODEL ** -0.5),
        "w_up": nrm(ks[27], (L, N_EXPERTS, D_MODEL, D_EXPERT), D_MODEL ** -0.5),
        "w_down": nrm(ks[28], (L, N_EXPERTS, D_EXPERT, D_MODEL), D_EXPERT ** -0.5),
    }


def reference(x_prompt, x_sample, state_conv, cache_ckv, cache_krope, page_table,
              g_mix, w_in, conv_w, conv_b, g_q_lat, w_uq, g_kv_lat, w_uk, w_uv,
              g_q_nope, g_q_rope, g_k_nope, g_k_rope, g_out, w_out,
              g_ffn, w_router_group, b_router_group, w_router_expert, b_router_expert,
              w_gate, w_up, w_down):
    b_p, s_p = x_prompt.shape[0], x_prompt.shape[1]
    t_s = x_sample.shape[1]
    past_len = page_table.shape[1] * cache_ckv.shape[2]
    pos_p = jnp.arange(s_p, dtype=jnp.int32)
    pos_s = past_len + jnp.arange(t_s, dtype=jnp.int32)
    kpos_s = jnp.arange(past_len + t_s, dtype=jnp.int32)
    n_blk = s_p // Q_BLOCK

    xp, xs = x_prompt, x_sample
    ckv_p_l, kr_p_l, conv_p_l, ckv_s_l, kr_s_l, conv_s_l = [], [], [], [], [], []
    for l in range(DEPTH):
        proj_w = (w_in[l], g_q_lat[l], w_uq[l], g_kv_lat[l], g_q_nope[l], g_q_rope[l], g_k_rope[l])
        wuk, wuv, gkn = w_uk[l], w_uv[l], g_k_nope[l]

        u_p, gb_p, gc_p, qn_p, qr_p, ckv_p, kr_p = _project(_rmsnorm(xp, g_mix[l]), pos_p, *proj_w)
        yconv_p, buf_p = _short_conv(u_p, gb_p, gc_p, jnp.zeros((b_p, CONV_K - 1, CONV_WIDTH), xp.dtype),
                                     conv_w[l], conv_b[l])
        kn_p, v_p = _kv_expand(ckv_p, wuk, wuv, gkn)

        def q_block(args, kn_p=kn_p, kr_p=kr_p, v_p=v_p):
            qn, qr, qpos = args
            return _attend(qn, qr, kn_p, kr_p, v_p, qpos, pos_p)

        def to_blocks(a):
            return jnp.moveaxis(a.reshape((b_p, n_blk, Q_BLOCK) + a.shape[2:]), 1, 0)

        ya = lax.map(q_block, (to_blocks(qn_p), to_blocks(qr_p), pos_p.reshape(n_blk, Q_BLOCK)))
        yattn_p = jnp.moveaxis(ya, 0, 1).reshape(b_p, s_p, N_HEADS, V_DIM)
        xp = xp + _merge(yconv_p, yattn_p, g_out[l], w_out[l])
        xp = xp + _hier_moe(_rmsnorm(xp, g_ffn[l]), w_router_group[l], b_router_group[l],
                            w_router_expert[l], b_router_expert[l], w_gate[l], w_up[l], w_down[l])

        u_s, gb_s, gc_s, qn_s, qr_s, ckv_s, kr_s = _project(_rmsnorm(xs, g_mix[l]), pos_s, *proj_w)
        yconv_s, buf_s = _short_conv(u_s, gb_s, gc_s, state_conv[l], conv_w[l], conv_b[l])
        pool_ckv, pool_kr = cache_ckv[l], cache_krope[l]

        def one_seq(args, pool_ckv=pool_ckv, pool_kr=pool_kr):
            qn, qr, ckv_new, kr_new, pages = args
            ckv_all = jnp.concatenate([pool_ckv[pages].reshape(-1, KV_RANK).astype(ckv_new.dtype), ckv_new], axis=0)
            kr_all = jnp.concatenate([pool_kr[pages].reshape(-1, ROPE_DIM).astype(kr_new.dtype), kr_new], axis=0)
            kn, v = _kv_expand(ckv_all, wuk, wuv, gkn)
            return _attend(qn[None], qr[None], kn[None], kr_all[None], v[None], pos_s, kpos_s)[0]

        yattn_s = lax.map(one_seq, (qn_s, qr_s, ckv_s, kr_s, page_table))
        xs = xs + _merge(yconv_s, yattn_s, g_out[l], w_out[l])
        xs = xs + _hier_moe(_rmsnorm(xs, g_ffn[l]), w_router_group[l], b_router_group[l],
                            w_router_expert[l], b_router_expert[l], w_gate[l], w_up[l], w_down[l])

        ckv_p_l.append(ckv_p)
        kr_p_l.append(kr_p)
        conv_p_l.append(buf_p)
        ckv_s_l.append(ckv_s)
        kr_s_l.append(kr_s)
        conv_s_l.append(buf_s)

    return (xp, xs, jnp.stack(ckv_p_l), jnp.stack(kr_p_l), jnp.stack(conv_p_l),
            jnp.stack(ckv_s_l), jnp.stack(kr_s_l), jnp.stack(conv_s_l))
```

```python
import functools

import jax
import jax.numpy as jnp
from jax import lax
from jax.experimental import pallas as pl
from jax.experimental.pallas import tpu as pltpu

N_HEADS = 8
NOPE_DIM = 64
ROPE_DIM = 32
V_DIM = 64
HEAD_QK = NOPE_DIM + ROPE_DIM
HALF_ROPE = ROPE_DIM // 2
ROPE_BASE = 10000.0
CONV_K = 3
OUT_GROUP_DIM = 64
N_GROUPS = 4
EXPERTS_PER_GROUP = 8
N_EXPERTS = N_GROUPS * EXPERTS_PER_GROUP
TOP_K = 2
EPS = 1e-6
MASK_VALUE = -1e30

LANES = 128
SUBLANES = 8
VMEM_LIMIT_BYTES = 48 * 1024 * 1024

PROJ_TILE = 512
ATTN_TILE = 512
TOKEN_TILE = 256
EXPERT_TILE = 256
DECODE_CHUNK_PAGES = 8

F32 = jnp.float32
BF16 = jnp.bfloat16

_NT = (((1,), (1,)), ((), ()))
_TN = (((0,), (0,)), ((), ()))


def _params(sem):
    return pltpu.CompilerParams(dimension_semantics=sem, vmem_limit_bytes=VMEM_LIMIT_BYTES)


def _rms_rows(x):
    return lax.rsqrt(jnp.mean(x * x, axis=0, keepdims=True) + EPS)


def _rms_lanes(x):
    return lax.rsqrt(jnp.mean(x * x, axis=-1, keepdims=True) + EPS)


def _group_norm_lanes(y, gain):
    lane = lax.broadcasted_iota(jnp.int32, (1, LANES), 1)
    low = lane < OUT_GROUP_DIM
    outs = []
    for j in range(y.shape[1] // LANES):
        t = y[:, j * LANES:(j + 1) * LANES]
        sq = t * t
        ss_lo = jnp.sum(jnp.where(low, sq, 0.0), axis=-1, keepdims=True)
        ss_hi = jnp.sum(jnp.where(low, 0.0, sq), axis=-1, keepdims=True)
        r = jnp.where(low, lax.rsqrt(ss_lo / OUT_GROUP_DIM + EPS), lax.rsqrt(ss_hi / OUT_GROUP_DIM + EPS))
        outs.append(t * r)
    return jnp.concatenate(outs, axis=1) * gain


def _rope_rows(x, cos, sin):
    x1, x2 = x[:HALF_ROPE], x[HALF_ROPE:]
    return x1 * cos - x2 * sin, x1 * sin + x2 * cos


def _attention_side(h_bf16, win_at_ref, gqlat_ref, wuqt_ref, gkv_ref, gqn_ref, gqr_ref, gkr_ref, cos, sin):
    q_rank = gqlat_ref.shape[0]
    kv_rank = gkv_ref.shape[0]
    scale = HEAD_QK ** -0.5
    zat = lax.dot_general(win_at_ref[...], h_bf16, _NT, preferred_element_type=F32)
    qlt = zat[:q_rank]
    kvt = zat[q_rank:q_rank + kv_rank]
    krt = zat[q_rank + kv_rank:]
    qln = (qlt * _rms_rows(qlt) * gqlat_ref[...]).astype(BF16)
    qt = jnp.dot(wuqt_ref[...], qln, preferred_element_type=F32)
    q_nope, q_rope = [], []
    for h in range(N_HEADS):
        nope = qt[h * HEAD_QK:h * HEAD_QK + NOPE_DIM]
        rope = qt[h * HEAD_QK + NOPE_DIM:(h + 1) * HEAD_QK]
        q_nope.append(nope * _rms_rows(nope) * gqn_ref[...] * scale)
        r1, r2 = _rope_rows(rope * _rms_rows(rope) * gqr_ref[...], cos, sin)
        q_rope.append((r1 * scale, r2 * scale))
    ckvt = kvt * _rms_rows(kvt) * gkv_ref[...]
    k1, k2 = _rope_rows(krt * _rms_rows(krt) * gkr_ref[...], cos, sin)
    return q_nope, q_rope, ckvt, (k1, k2)


def _to_token_major(xt, width):
    rows, toks = xt.shape
    if rows < LANES:
        xt = jnp.concatenate([xt, jnp.zeros((LANES - rows, toks), F32)], axis=0)
    return xt.T[:, :width]


def _proj_prompt_kernel(x_ref, gmix_ref, win_c_ref, win_at_ref, convw_ref, convb_ref, gout_c_ref,
                        gqlat_ref, wuqt_ref, gkv_ref, gqn_ref, gqr_ref, gkn_ref, gkr_ref,
                        wukt_ref, wuvt_ref, cos_ref, sin_ref,
                        yconv_ref, qt_ref, kt_ref, vt_ref, ckv_ref, krope_ref, convst_ref,
                        ext_ref):
    si = pl.program_id(1)
    tm = x_ref.shape[1]
    c = convw_ref.shape[1]

    xf = x_ref[0]
    h = (xf * _rms_lanes(xf) * gmix_ref[...]).astype(BF16)

    zc = jnp.dot(h, win_c_ref[...], preferred_element_type=F32)
    u = zc[:, 2 * c:] * zc[:, :c]

    @pl.when(si == 0)
    def _():
        ext_ref[0:SUBLANES, :] = jnp.zeros((SUBLANES, c), F32)

    ext_ref[SUBLANES:, :] = u
    v = (convb_ref[...]
         + convw_ref[0:1, :] * ext_ref[pl.ds(SUBLANES - 2, tm), :]
         + convw_ref[1:2, :] * ext_ref[pl.ds(SUBLANES - 1, tm), :]
         + convw_ref[2:3, :] * u)
    yconv = zc[:, c:2 * c] * v
    yconv_ref[0] = _group_norm_lanes(yconv, gout_c_ref[...]).astype(BF16)
    ext_ref[0:SUBLANES, :] = ext_ref[pl.ds(tm, SUBLANES), :]
    convst_ref[0] = ext_ref[pl.ds(SUBLANES - (CONV_K - 1), CONV_K - 1), :]

    q_nope, q_rope, ckvt, (k1, k2) = _attention_side(
        h, win_at_ref, gqlat_ref, wuqt_ref, gkv_ref, gqn_ref, gqr_ref, gkr_ref, cos_ref[...], sin_ref[...])
    for hd in range(N_HEADS):
        qt_ref[0, hd, 0:NOPE_DIM, :] = q_nope[hd].astype(BF16)
        qt_ref[0, hd, NOPE_DIM:NOPE_DIM + HALF_ROPE, :] = q_rope[hd][0].astype(BF16)
        qt_ref[0, hd, NOPE_DIM + HALF_ROPE:HEAD_QK, :] = q_rope[hd][1].astype(BF16)
    ckv_ref[0] = ckvt.T
    krt = jnp.concatenate([k1, k2], axis=0)
    krope_ref[0] = _to_token_major(krt, ROPE_DIM)
    ckv_b = ckvt.astype(BF16)
    ktn = jnp.dot(wukt_ref[...], ckv_b, preferred_element_type=F32)
    vt = jnp.dot(wuvt_ref[...], ckv_b, preferred_element_type=F32)
    krt_b = krt.astype(BF16)
    for hd in range(N_HEADS):
        blk = ktn[hd * NOPE_DIM:(hd + 1) * NOPE_DIM]
        kt_ref[0, hd, 0:NOPE_DIM, :] = (blk * _rms_rows(blk) * gkn_ref[...]).astype(BF16)
        kt_ref[0, hd, NOPE_DIM:HEAD_QK, :] = krt_b
        vt_ref[0, hd] = vt[hd * V_DIM:(hd + 1) * V_DIM].astype(BF16)


def _proj_prompt(x, w, cos_t, sin_t):
    b, s, d = x.shape
    tm = min(PROJ_TILE, s)
    assert s % tm == 0
    c = w["conv_w"].shape[1]
    kv_rank = w["g_kv"].shape[0]
    full = lambda a: pl.BlockSpec(a.shape, lambda bi, si: (0,) * a.ndim)
    weights = [w["g_mix"], w["win_c"], w["win_at"], w["conv_w"], w["conv_b"], w["g_out_c"],
               w["g_qlat"], w["wuq_t"], w["g_kv"], w["g_qn"], w["g_qr"], w["g_kn"], w["g_kr"],
               w["wuk_t"], w["wuv_t"]]
    in_specs = ([pl.BlockSpec((1, tm, d), lambda bi, si: (bi, si, 0))] + [full(a) for a in weights]
                + [pl.BlockSpec((HALF_ROPE, tm), lambda bi, si: (0, si))] * 2)
    out_shape = [
        jax.ShapeDtypeStruct((b, s, c), BF16),
        jax.ShapeDtypeStruct((b, N_HEADS, HEAD_QK, s), BF16),
        jax.ShapeDtypeStruct((b, N_HEADS, HEAD_QK, s), BF16),
        jax.ShapeDtypeStruct((b, N_HEADS, V_DIM, s), BF16),
        jax.ShapeDtypeStruct((b, s, kv_rank), F32),
        jax.ShapeDtypeStruct((b, s, ROPE_DIM), F32),
        jax.ShapeDtypeStruct((b, CONV_K - 1, c), F32),
    ]
    out_specs = [
        pl.BlockSpec((1, tm, c), lambda bi, si: (bi, si, 0)),
        pl.BlockSpec((1, N_HEADS, HEAD_QK, tm), lambda bi, si: (bi, 0, 0, si)),
        pl.BlockSpec((1, N_HEADS, HEAD_QK, tm), lambda bi, si: (bi, 0, 0, si)),
        pl.BlockSpec((1, N_HEADS, V_DIM, tm), lambda bi, si: (bi, 0, 0, si)),
        pl.BlockSpec((1, tm, kv_rank), lambda bi, si: (bi, si, 0)),
        pl.BlockSpec((1, tm, ROPE_DIM), lambda bi, si: (bi, si, 0)),
        pl.BlockSpec((1, CONV_K - 1, c), lambda bi, si: (bi, 0, 0)),
    ]
    return pl.pallas_call(
        _proj_prompt_kernel,
        grid=(b, s // tm),
        in_specs=in_specs,
        out_specs=out_specs,
        out_shape=out_shape,
        scratch_shapes=[pltpu.VMEM((tm + SUBLANES, c), F32)],
        compiler_params=_params(("arbitrary", "arbitrary")),
        name="proj_prompt",
    )(x, *weights, cos_t, sin_t)


def _attn_prompt_kernel(qi_ref, ki_ref, qt_ref, kt_ref, vt_ref, gout_ref, y_ref, m_ref, l_ref, acc_ref):
    p = pl.program_id(1)
    qi = qi_ref[p]
    ki = ki_ref[p]
    tq = qt_ref.shape[3]
    tk = kt_ref.shape[3]

    @pl.when(ki == 0)
    def _():
        m_ref[...] = jnp.full(m_ref.shape, -jnp.inf, F32)
        l_ref[...] = jnp.zeros(l_ref.shape, F32)
        acc_ref[...] = jnp.zeros(acc_ref.shape, F32)

    row = qi * tq + lax.broadcasted_iota(jnp.int32, (tq, tk), 0)
    col = ki * tk + lax.broadcasted_iota(jnp.int32, (tq, tk), 1)
    visible = col <= row
    for hd in range(N_HEADS):
        s = lax.dot_general(qt_ref[0, hd], kt_ref[0, hd], _TN, preferred_element_type=F32)
        s = jnp.where(visible, s, MASK_VALUE)
        m_prev = m_ref[hd]
        m_new = jnp.maximum(m_prev, jnp.max(s, axis=-1, keepdims=True))
        alpha = jnp.exp(m_prev - m_new)
        pr = jnp.exp(s - m_new)
        l_ref[hd] = alpha * l_ref[hd] + jnp.sum(pr, axis=-1, keepdims=True)
        pv = lax.dot_general(pr.astype(BF16), vt_ref[0, hd], _NT, preferred_element_type=F32)
        acc_ref[hd] = alpha * acc_ref[hd] + pv
        m_ref[hd] = m_new

    @pl.when(ki == qi)
    def _():
        outs = []
        for hd in range(N_HEADS):
            o = acc_ref[hd] / l_ref[hd]
            outs.append(o * _rms_lanes(o) * gout_ref[hd])
        y_ref[0] = jnp.concatenate(outs, axis=1).astype(BF16)


def _attn_prompt(qt, kt, vt, gout_a):
    b, _, _, s = qt.shape
    t = min(ATTN_TILE, s)
    assert s % t == 0
    n = s // t
    pairs = [(i, j) for i in range(n) for j in range(i + 1)]
    qi_tab = jnp.asarray([p[0] for p in pairs], jnp.int32)
    ki_tab = jnp.asarray([p[1] for p in pairs], jnp.int32)
    grid_spec = pltpu.PrefetchScalarGridSpec(
        num_scalar_prefetch=2,
        grid=(b, len(pairs)),
        in_specs=[
            pl.BlockSpec((1, N_HEADS, HEAD_QK, t), lambda bi, p, qi, ki: (bi, 0, 0, qi[p])),
            pl.BlockSpec((1, N_HEADS, HEAD_QK, t), lambda bi, p, qi, ki: (bi, 0, 0, ki[p])),
            pl.BlockSpec((1, N_HEADS, V_DIM, t), lambda bi, p, qi, ki: (bi, 0, 0, ki[p])),
            pl.BlockSpec(gout_a.shape, lambda bi, p, qi, ki: (0, 0, 0)),
        ],
        out_specs=pl.BlockSpec((1, t, N_HEADS * V_DIM), lambda bi, p, qi, ki: (bi, qi[p], 0)),
        scratch_shapes=[pltpu.VMEM((N_HEADS, t, 1), F32), pltpu.VMEM((N_HEADS, t, 1), F32),
                        pltpu.VMEM((N_HEADS, t, V_DIM), F32)],
    )
    return pl.pallas_call(
        _attn_prompt_kernel,
        grid_spec=grid_spec,
        out_shape=jax.ShapeDtypeStruct((b, s, N_HEADS * V_DIM), BF16),
        compiler_params=_params(("arbitrary", "arbitrary")),
        name="attn_prompt",
    )(qi_tab, ki_tab, qt, kt, vt, gout_a)


def _proj_sample_kernel(x_ref, tpos_ref, st1_ref, st2_ref, gmix_ref, win_c_ref, win_at_ref, convw_ref,
                        convb_ref, gout_c_ref, gqlat_ref, wuqt_ref, gkv_ref, gqn_ref, gqr_ref, gkn_ref,
                        gkr_ref, wuk_ref, cos_ref, sin_ref,
                        yconv_ref, u_ref, qa_ref, qr_ref, ckv_ref, krope_ref, ext_ref):
    tm = x_ref.shape[0]
    c = convw_ref.shape[1]
    xf = x_ref[...]
    h = (xf * _rms_lanes(xf) * gmix_ref[...]).astype(BF16)

    zc = jnp.dot(h, win_c_ref[...], preferred_element_type=F32)
    u = zc[:, 2 * c:] * zc[:, :c]
    ext_ref[0:SUBLANES, :] = jnp.zeros((SUBLANES, c), F32)
    ext_ref[SUBLANES:, :] = u
    tpos = tpos_ref[...]
    u_m2 = jnp.where(tpos >= 2, ext_ref[pl.ds(SUBLANES - 2, tm), :], st2_ref[...])
    u_m1 = jnp.where(tpos >= 1, ext_ref[pl.ds(SUBLANES - 1, tm), :], st1_ref[...])
    v = convb_ref[...] + convw_ref[0:1, :] * u_m2 + convw_ref[1:2, :] * u_m1 + convw_ref[2:3, :] * u
    yconv = zc[:, c:2 * c] * v
    yconv_ref[...] = _group_norm_lanes(yconv, gout_c_ref[...]).astype(BF16)
    u_ref[...] = u

    q_nope, q_rope, ckvt, (k1, k2) = _attention_side(
        h, win_at_ref, gqlat_ref, wuqt_ref, gkv_ref, gqn_ref, gqr_ref, gkr_ref, cos_ref[...], sin_ref[...])
    for hd in range(N_HEADS):
        qg = (q_nope[hd] * gkn_ref[...]).astype(BF16)
        qa_t = jnp.dot(wuk_ref[:, hd * NOPE_DIM:(hd + 1) * NOPE_DIM], qg, preferred_element_type=F32)
        qa_ref[hd] = qa_t.T
        qr_ref[hd] = _to_token_major(jnp.concatenate(q_rope[hd], axis=0), ROPE_DIM)
    ckv_ref[...] = ckvt.T
    krope_ref[...] = _to_token_major(jnp.concatenate([k1, k2], axis=0), ROPE_DIM)


def _proj_sample(x, tpos, st1, st2, w, cos_t, sin_t):
    tm, d = x.shape
    c = w["conv_w"].shape[1]
    kv_rank = w["g_kv"].shape[0]
    args = [x, tpos, st1, st2, w["g_mix"], w["win_c"], w["win_at"], w["conv_w"], w["conv_b"], w["g_out_c"],
            w["g_qlat"], w["wuq_t"], w["g_kv"], w["g_qn"], w["g_qr"], w["g_kn"], w["g_kr"], w["wuk"],
            cos_t, sin_t]
    out_shape = [
        jax.ShapeDtypeStruct((tm, c), BF16),
        jax.ShapeDtypeStruct((tm, c), F32),
        jax.ShapeDtypeStruct((N_HEADS, tm, kv_rank), F32),
        jax.ShapeDtypeStruct((N_HEADS, tm, ROPE_DIM), F32),
        jax.ShapeDtypeStruct((tm, kv_rank), F32),
        jax.ShapeDtypeStruct((tm, ROPE_DIM), F32),
    ]
    return pl.pallas_call(
        _proj_sample_kernel,
        out_shape=out_shape,
        scratch_shapes=[pltpu.VMEM((tm + SUBLANES, c), F32)],
        compiler_params=pltpu.CompilerParams(vmem_limit_bytes=VMEM_LIMIT_BYTES),
        name="proj_sample",
    )(*args)


def _attn_decode_kernel(pt_ref, ckv_hbm, kr_hbm, qa_ref, qr_ref, cnew_ref, krnew_ref, wukt_ref, wuv_ref,
                        gout_ref, y_ref, cbuf, kbuf, sem, *, n_pages, page, t_dec):
    b = pl.program_id(0)
    chunk_pages = cbuf.shape[1] // page
    n_chunks = n_pages // chunk_pages
    rows = N_HEADS * t_dec

    def copies(j, slot):
        out = []
        for pg in range(chunk_pages):
            pid = pt_ref[b * n_pages + j * chunk_pages + pg]
            out.append(pltpu.make_async_copy(ckv_hbm.at[pid], cbuf.at[slot, pl.ds(pg * page, page)], sem.at[0, slot]))
            out.append(pltpu.make_async_copy(kr_hbm.at[pid], kbuf.at[slot, pl.ds(pg * page, page)], sem.at[1, slot]))
        return out

    def start(j, slot):
        for cp in copies(j, slot):
            cp.start()

    def wait(j, slot):
        for cp in copies(j, slot):
            cp.wait()

    qa = qa_ref[...].reshape(rows, qa_ref.shape[2]).astype(BF16)
    qr = qr_ref[...].reshape(rows, ROPE_DIM).astype(BF16)

    def scores(c_b, kr_b):
        ktn = lax.dot_general(wukt_ref[...], c_b, _NT, preferred_element_type=F32)
        rs = []
        for hd in range(N_HEADS):
            blk = ktn[hd * NOPE_DIM:(hd + 1) * NOPE_DIM]
            rs.append(jnp.broadcast_to(_rms_rows(blk), (t_dec, blk.shape[1])))
        rk = jnp.concatenate(rs, axis=0)
        return (lax.dot_general(qa, c_b, _NT, preferred_element_type=F32) * rk
                + lax.dot_general(qr, kr_b, _NT, preferred_element_type=F32))

    def update(carry, s, c_b):
        m_prev, l_prev, acc = carry
        m_new = jnp.maximum(m_prev, jnp.max(s, axis=-1, keepdims=True))
        alpha = jnp.exp(m_prev - m_new)
        pr = jnp.exp(s - m_new)
        l_new = alpha * l_prev + jnp.sum(pr, axis=-1, keepdims=True)
        acc = alpha * acc + jnp.dot(pr.astype(BF16), c_b, preferred_element_type=F32)
        return m_new, l_new, acc

    start(0, 0)

    def body(j, carry):
        slot = lax.rem(j, 2)

        @pl.when(j + 1 < n_chunks)
        def _():
            start(j + 1, 1 - slot)

        wait(j, slot)
        c_b = cbuf[slot].astype(BF16)
        kr_b = kbuf[slot].astype(BF16)
        return update(carry, scores(c_b, kr_b), c_b)

    init = (jnp.full((rows, 1), -jnp.inf, F32), jnp.zeros((rows, 1), F32), jnp.zeros((rows, cbuf.shape[2]), F32))
    carry = lax.fori_loop(0, n_chunks, body, init)

    pad = LANES - t_dec
    c_new = jnp.concatenate([cnew_ref[...], jnp.zeros((pad, cnew_ref.shape[1]), F32)], axis=0).astype(BF16)
    kr_new = jnp.concatenate([krnew_ref[...], jnp.zeros((pad, ROPE_DIM), F32)], axis=0).astype(BF16)
    s_new = scores(c_new, kr_new)
    q_t = lax.rem(lax.broadcasted_iota(jnp.int32, (rows, LANES), 0), t_dec)
    key = lax.broadcasted_iota(jnp.int32, (rows, LANES), 1)
    s_new = jnp.where(key <= q_t, s_new, MASK_VALUE)
    _, l_fin, acc = update(carry, s_new, c_new)

    o_lat = (acc / l_fin).astype(BF16)
    ov = jnp.dot(o_lat, wuv_ref[...], preferred_element_type=F32)
    outs = []
    for hd in range(N_HEADS):
        o = ov[hd * t_dec:(hd + 1) * t_dec, hd * V_DIM:(hd + 1) * V_DIM]
        outs.append(o * _rms_lanes(o) * gout_ref[hd])
    y_ref[...] = jnp.concatenate(outs, axis=1)


def _attn_decode(page_table, cache_ckv, cache_kr, qa, qr, c_new, kr_new, w, t_dec):
    n_seq, n_pages = page_table.shape
    _, page, kv_rank = cache_ckv.shape
    chunk_pages = min(DECODE_CHUNK_PAGES, n_pages)
    assert n_pages % chunk_pages == 0 and t_dec % SUBLANES == 0 and t_dec <= LANES
    chunk = chunk_pages * page
    grid_spec = pltpu.PrefetchScalarGridSpec(
        num_scalar_prefetch=1,
        grid=(n_seq,),
        in_specs=[
            pl.BlockSpec(memory_space=pl.ANY),
            pl.BlockSpec(memory_space=pl.ANY),
            pl.BlockSpec((N_HEADS, t_dec, kv_rank), lambda b, pt: (0, b, 0)),
            pl.BlockSpec((N_HEADS, t_dec, ROPE_DIM), lambda b, pt: (0, b, 0)),
            pl.BlockSpec((t_dec, kv_rank), lambda b, pt: (b, 0)),
            pl.BlockSpec((t_dec, ROPE_DIM), lambda b, pt: (b, 0)),
            pl.BlockSpec(w["wuk_t"].shape, lambda b, pt: (0, 0)),
            pl.BlockSpec(w["wuv"].shape, lambda b, pt: (0, 0)),
            pl.BlockSpec(w["g_out_a"].shape, lambda b, pt: (0, 0, 0)),
        ],
        out_specs=pl.BlockSpec((t_dec, N_HEADS * V_DIM), lambda b, pt: (b, 0)),
        scratch_shapes=[pltpu.VMEM((2, chunk, kv_rank), F32), pltpu.VMEM((2, chunk, ROPE_DIM), F32),
                        pltpu.SemaphoreType.DMA((2, 2))],
    )
    return pl.pallas_call(
        functools.partial(_attn_decode_kernel, n_pages=n_pages, page=page, t_dec=t_dec),
        grid_spec=grid_spec,
        out_shape=jax.ShapeDtypeStruct((n_seq * t_dec, N_HEADS * V_DIM), F32),
        compiler_params=_params(("arbitrary",)),
        name="attn_decode",
    )(page_table.reshape(-1), cache_ckv, cache_kr, qa, qr, c_new, kr_new, w["wuk_t"], w["wuv"], w["g_out_a"])


ROUTE_E0, ROUTE_E1, ROUTE_R0, ROUTE_R1, ROUTE_W0, ROUTE_W1 = range(6)


def _lane_pick(x, lane, idx):
    return jnp.sum(jnp.where(lane == idx, x, 0.0), axis=-1, keepdims=True)


def _merge_route_kernel(ycp_ref, yap_ref, xp_ref, ycs_ref, yas_ref, xs_ref, wout_ref, gffn_ref, wr_ref, br_ref,
                        xmid_ref, route_ref, counts_ref, carry_ref, *, n_prompt_tiles):
    i = pl.program_id(0)
    tm = xp_ref.shape[0]
    is_p = i < n_prompt_tiles

    @pl.when(i == 0)
    def _():
        carry_ref[...] = jnp.zeros(carry_ref.shape, F32)

    yc = jnp.where(is_p, ycp_ref[...], ycs_ref[...])
    ya = jnp.where(is_p, yap_ref[...], yas_ref[...].astype(BF16))
    x = jnp.where(is_p, xp_ref[...], xs_ref[...])
    y = jnp.concatenate([yc, ya], axis=1)
    xm = x + jnp.dot(y, wout_ref[...], preferred_element_type=F32)
    xmid_ref[...] = xm
    h2 = (xm * _rms_lanes(xm) * gffn_ref[...]).astype(BF16)
    logits = jnp.dot(h2, wr_ref[...], preferred_element_type=F32) + br_ref[...]

    lane_i = lax.broadcasted_iota(jnp.int32, (tm, LANES), 1)
    lane = lane_i.astype(F32)
    neg = -jnp.inf
    far = float(LANES)
    in_groups = lane_i < N_GROUPS
    gl = jnp.where(in_groups, logits, neg)
    ge = jnp.exp(gl - jnp.max(gl, axis=-1, keepdims=True))
    pg = ge / jnp.sum(ge, axis=-1, keepdims=True)
    p_sel = jnp.max(pg, axis=-1, keepdims=True)
    g_sel = jnp.min(jnp.where((pg == p_sel) & in_groups, lane, far), axis=-1, keepdims=True)
    lo = N_GROUPS + g_sel * EXPERTS_PER_GROUP
    el = jnp.where((lane >= lo) & (lane < lo + EXPERTS_PER_GROUP), logits, neg)
    v1 = jnp.max(el, axis=-1, keepdims=True)
    i1 = jnp.min(jnp.where(el == v1, lane, far), axis=-1, keepdims=True)
    el2 = jnp.where(lane == i1, neg, el)
    v2 = jnp.max(el2, axis=-1, keepdims=True)
    i2 = jnp.min(jnp.where(el2 == v2, lane, far), axis=-1, keepdims=True)
    e2 = jnp.exp(v2 - v1)
    w0 = 1.0 / (1.0 + e2) * p_sel
    w1 = e2 / (1.0 + e2) * p_sel
    e0 = i1 - N_GROUPS
    e1 = i2 - N_GROUPS

    oh0 = lane == e0
    oh1 = lane == e1
    onehot = (oh0 | oh1).astype(BF16)
    tri = (lax.broadcasted_iota(jnp.int32, (tm, tm), 0) > lax.broadcasted_iota(jnp.int32, (tm, tm), 1)).astype(BF16)
    before = jnp.dot(tri, onehot, preferred_element_type=F32) + carry_ref[...]
    r0 = jnp.sum(jnp.where(oh0, before, 0.0), axis=-1, keepdims=True)
    r1 = jnp.sum(jnp.where(oh1, before, 0.0), axis=-1, keepdims=True)
    carry_ref[...] += jnp.sum(onehot.astype(F32), axis=0, keepdims=True)

    route = jnp.zeros((tm, LANES), F32)
    for idx, val in ((ROUTE_E0, e0), (ROUTE_E1, e1), (ROUTE_R0, r0), (ROUTE_R1, r1), (ROUTE_W0, w0), (ROUTE_W1, w1)):
        route = jnp.where(lane_i == idx, val, route)
    route_ref[...] = route

    @pl.when(i == pl.num_programs(0) - 1)
    def _():
        counts_ref[...] = carry_ref[...]


def _merge_route(ycp, yap, xp, ycs, yas, xs, w):
    tp, d = xp.shape
    ts = xs.shape[0]
    tm = TOKEN_TILE
    assert tp % tm == 0 and ts % tm == 0
    npt, nst = tp // tm, ts // tm
    half = ycp.shape[1]
    pmap = lambda i: (jnp.minimum(i, npt - 1), 0)
    smap = lambda i: (jnp.maximum(i - npt, 0), 0)
    cmap = lambda i: (0, 0)
    return pl.pallas_call(
        functools.partial(_merge_route_kernel, n_prompt_tiles=npt),
        grid=(npt + nst,),
        in_specs=[
            pl.BlockSpec((tm, half), pmap), pl.BlockSpec((tm, half), pmap), pl.BlockSpec((tm, d), pmap),
            pl.BlockSpec((tm, half), smap), pl.BlockSpec((tm, half), smap), pl.BlockSpec((tm, d), smap),
            pl.BlockSpec(w["w_out"].shape, cmap), pl.BlockSpec(w["g_ffn"].shape, cmap),
            pl.BlockSpec(w["w_r"].shape, cmap), pl.BlockSpec(w["b_r"].shape, cmap),
        ],
        out_specs=[pl.BlockSpec((tm, d), lambda i: (i, 0)), pl.BlockSpec((tm, LANES), lambda i: (i, 0)),
                   pl.BlockSpec((1, LANES), cmap)],
        out_shape=[jax.ShapeDtypeStruct((tp + ts, d), F32), jax.ShapeDtypeStruct((tp + ts, LANES), F32),
                   jax.ShapeDtypeStruct((1, LANES), F32)],
        scratch_shapes=[pltpu.VMEM((1, LANES), F32)],
        compiler_params=_params(("arbitrary",)),
        name="merge_route",
    )(ycp, yap, xp, ycs, yas, xs, w["w_out"], w["g_ffn"], w["w_r"], w["b_r"])


def _row_gather(src_hbm, idx_ref, dst, sem, n_rows):
    def body(r, _):
        pltpu.make_async_copy(src_hbm.at[pl.ds(idx_ref[0, 0, r], 1)], dst.at[pl.ds(r, 1)], sem).start()
        return 0
    lax.fori_loop(0, n_rows, body, 0, unroll=8)


def _row_gather_wait(src_hbm, dst, sem):
    for r in range(dst.shape[0]):
        pltpu.make_async_copy(src_hbm.at[pl.ds(0, 1)], dst.at[pl.ds(r, 1)], sem).wait()


def _experts_kernel(te_ref, nu_ref, idx_ref, idx_next_ref, x_hbm, gffn_ref, wgu_ref, wd_ref, y_ref, buf, sem):
    i = pl.program_id(0)
    n_used = nu_ref[0]
    tme = buf.shape[1]
    d_exp = wd_ref.shape[1]
    slot = lax.rem(i, 2)

    @pl.when(i == 0)
    def _():
        _row_gather(x_hbm, idx_ref, buf.at[0], sem.at[0], tme)

    @pl.when(i + 1 < n_used)
    def _():
        _row_gather(x_hbm, idx_next_ref, buf.at[1 - slot], sem.at[1 - slot], tme)

    @pl.when(i < n_used)
    def _():
        _row_gather_wait(x_hbm, buf.at[slot], sem.at[slot])
        xm = buf[slot]
        h2 = (xm * _rms_lanes(xm) * gffn_ref[...]).astype(BF16)
        gu = jnp.dot(h2, wgu_ref[0], preferred_element_type=F32)
        g = gu[:, :d_exp]
        a = (g / (1.0 + jnp.exp(-g))) * gu[:, d_exp:]
        y_ref[...] = jnp.dot(a.astype(BF16), wd_ref[0], preferred_element_type=F32)

    @pl.when(i >= n_used)
    def _():
        y_ref[...] = jnp.zeros(y_ref.shape, F32)


def _experts(tile_expert, n_used, tok_of_slot, xmid, w):
    n_tiles = tile_expert.shape[0]
    tme = EXPERT_TILE
    d = xmid.shape[1]
    idx3 = tok_of_slot.reshape(n_tiles, 1, tme)
    smem_blk = lambda f: pl.BlockSpec((1, 1, tme), f, memory_space=pltpu.SMEM)
    grid_spec = pltpu.PrefetchScalarGridSpec(
        num_scalar_prefetch=2,
        grid=(n_tiles,),
        in_specs=[
            smem_blk(lambda i, te, nu: (i, 0, 0)),
            smem_blk(lambda i, te, nu: (jnp.minimum(i + 1, n_tiles - 1), 0, 0)),
            pl.BlockSpec(memory_space=pl.ANY),
            pl.BlockSpec(w["g_ffn"].shape, lambda i, te, nu: (0, 0)),
            pl.BlockSpec((1,) + w["w_gu"].shape[1:], lambda i, te, nu: (te[i], 0, 0)),
            pl.BlockSpec((1,) + w["w_d"].shape[1:], lambda i, te, nu: (te[i], 0, 0)),
        ],
        out_specs=pl.BlockSpec((tme, d), lambda i, te, nu: (i, 0)),
        scratch_shapes=[pltpu.VMEM((2, tme, d), F32), pltpu.SemaphoreType.DMA((2,))],
    )
    return pl.pallas_call(
        _experts_kernel,
        grid_spec=grid_spec,
        out_shape=jax.ShapeDtypeStruct((n_tiles * tme, d), F32),
        compiler_params=_params(("arbitrary",)),
        name="experts",
    )(tile_expert, n_used, idx3, idx3, xmid, w["g_ffn"], w["w_gu"], w["w_d"])


def _combine_kernel(s0_ref, s1_ref, s0n_ref, s1n_ref, ys_hbm, xmid_ref, route_ref, yp_ref, ysmp_ref, buf, sem,
                    *, n_prompt_tiles):
    i = pl.program_id(0)
    n = pl.num_programs(0)
    tm = xmid_ref.shape[0]
    slot = lax.rem(i, 2)

    @pl.when(i == 0)
    def _():
        _row_gather(ys_hbm, s0_ref, buf.at[0, 0], sem.at[0, 0], tm)
        _row_gather(ys_hbm, s1_ref, buf.at[0, 1], sem.at[0, 1], tm)

    @pl.when(i + 1 < n)
    def _():
        _row_gather(ys_hbm, s0n_ref, buf.at[1 - slot, 0], sem.at[1 - slot, 0], tm)
        _row_gather(ys_hbm, s1n_ref, buf.at[1 - slot, 1], sem.at[1 - slot, 1], tm)

    _row_gather_wait(ys_hbm, buf.at[slot, 0], sem.at[slot, 0])
    _row_gather_wait(ys_hbm, buf.at[slot, 1], sem.at[slot, 1])
    route = route_ref[...]
    lane = lax.broadcasted_iota(jnp.int32, route.shape, 1)
    w0 = _lane_pick(route, lane, ROUTE_W0)
    w1 = _lane_pick(route, lane, ROUTE_W1)
    out = xmid_ref[...] + (w0 * buf[slot, 0] + w1 * buf[slot, 1])

    @pl.when(i < n_prompt_tiles)
    def _():
        yp_ref[...] = out

    @pl.when(i >= n_prompt_tiles)
    def _():
        ysmp_ref[...] = out


def _combine(slot0, slot1, ys, xmid, route, tp):
    ttot, d = xmid.shape
    tm = TOKEN_TILE
    n = ttot // tm
    npt = tp // tm
    s0 = slot0.reshape(n, 1, tm)
    s1 = slot1.reshape(n, 1, tm)
    cur = lambda i: (i, 0, 0)
    nxt = lambda i: (jnp.minimum(i + 1, n - 1), 0, 0)
    smem_blk = lambda f: pl.BlockSpec((1, 1, tm), f, memory_space=pltpu.SMEM)
    return pl.pallas_call(
        functools.partial(_combine_kernel, n_prompt_tiles=npt),
        grid=(n,),
        in_specs=[smem_blk(cur), smem_blk(cur), smem_blk(nxt), smem_blk(nxt),
                  pl.BlockSpec(memory_space=pl.ANY),
                  pl.BlockSpec((tm, d), lambda i: (i, 0)), pl.BlockSpec((tm, LANES), lambda i: (i, 0))],
        out_specs=[pl.BlockSpec((tm, d), lambda i: (jnp.minimum(i, npt - 1), 0)),
                   pl.BlockSpec((tm, d), lambda i: (jnp.maximum(i - npt, 0), 0))],
        out_shape=[jax.ShapeDtypeStruct((tp, d), F32), jax.ShapeDtypeStruct((ttot - tp, d), F32)],
        scratch_shapes=[pltpu.VMEM((2, 2, tm, d), F32), pltpu.SemaphoreType.DMA((2, 2))],
        compiler_params=_params(("arbitrary",)),
        name="combine",
    )(s0, s1, s0, s1, ys, xmid, route)


def _rope_tables(pos):
    inv_freq = ROPE_BASE ** (-jnp.arange(HALF_ROPE, dtype=F32) / HALF_ROPE)
    ang = pos.astype(F32)[:, None] * inv_freq[None, :]
    return jnp.cos(ang).T, jnp.sin(ang).T


def _layer_weights(l, g_mix, w_in, conv_w, conv_b, g_q_lat, w_uq, g_kv_lat, w_uk, w_uv, g_q_nope, g_q_rope,
                   g_k_nope, g_k_rope, g_out, w_out, g_ffn, w_router_group, b_router_group, w_router_expert,
                   b_router_expert, w_gate, w_up, w_down):
    c = conv_w.shape[2]
    col = lambda g: g[l].reshape(-1, 1)
    w_r = jnp.concatenate([w_router_group[l], w_router_expert[l]], axis=1)
    b_r = jnp.concatenate([b_router_group[l], b_router_expert[l]])
    return {
        "g_mix": g_mix[l].reshape(1, -1),
        "win_c": w_in[l][:, :3 * c].astype(BF16),
        "win_at": w_in[l][:, 3 * c:].T.astype(BF16),
        "conv_w": conv_w[l],
        "conv_b": conv_b[l].reshape(1, -1),
        "g_out_c": g_out[l][:c].reshape(1, -1),
        "g_out_a": g_out[l][c:].reshape(N_HEADS, 1, V_DIM),
        "g_qlat": col(g_q_lat), "g_kv": col(g_kv_lat), "g_qn": col(g_q_nope), "g_qr": col(g_q_rope),
        "g_kn": col(g_k_nope), "g_kr": col(g_k_rope),
        "wuq_t": w_uq[l].T.astype(BF16),
        "wuk_t": w_uk[l].T.astype(BF16),
        "wuk": w_uk[l].astype(BF16),
        "wuv_t": w_uv[l].T.astype(BF16),
        "wuv": w_uv[l].astype(BF16),
        "w_out": w_out[l].astype(BF16),
        "g_ffn": g_ffn[l].reshape(1, -1),
        "w_r": jnp.pad(w_r, ((0, 0), (0, LANES - w_r.shape[1]))).astype(BF16),
        "b_r": jnp.pad(b_r, (0, LANES - b_r.shape[0])).reshape(1, -1),
        "w_gu": jnp.concatenate([w_gate[l], w_up[l]], axis=-1).astype(BF16),
        "w_d": w_down[l].astype(BF16),
    }


def _moe_plan(route, counts):
    ttot = route.shape[0]
    tme = EXPERT_TILE
    n_tiles = (TOP_K * ttot) // tme + N_EXPERTS
    cnt = counts[0, :N_EXPERTS].astype(jnp.int32)
    tiles = (cnt + tme - 1) // tme
    tile_end = jnp.cumsum(tiles)
    base = (tile_end - tiles) * tme
    e0 = route[:, ROUTE_E0].astype(jnp.int32)
    e1 = route[:, ROUTE_E1].astype(jnp.int32)
    slot0 = base[e0] + route[:, ROUTE_R0].astype(jnp.int32)
    slot1 = base[e1] + route[:, ROUTE_R1].astype(jnp.int32)
    tok = jnp.arange(ttot, dtype=jnp.int32)
    tok_of_slot = jnp.zeros((n_tiles * tme,), jnp.int32).at[slot0].set(tok).at[slot1].set(tok)
    n_used = tile_end[-1:]
    tile_id = jnp.minimum(jnp.arange(n_tiles, dtype=jnp.int32), n_used[0] - 1)
    tile_expert = jnp.sum((tile_end[None, :] <= tile_id[:, None]).astype(jnp.int32), axis=1)
    return slot0, slot1, tok_of_slot, tile_expert, n_used.astype(jnp.int32)


def kernel(x_prompt, x_sample, state_conv, cache_ckv, cache_krope, page_table, g_mix, w_in, conv_w, conv_b, g_q_lat, w_uq, g_kv_lat, w_uk, w_uv, g_q_nope, g_q_rope, g_k_nope, g_k_rope, g_out, w_out, g_ffn, w_router_group, b_router_group, w_router_expert, b_router_expert, w_gate, w_up, w_down):
    b_p, s_p, d = x_prompt.shape
    b_s, t_s, _ = x_sample.shape
    depth = g_mix.shape[0]
    c = conv_w.shape[2]
    page = cache_ckv.shape[2]
    past_len = page_table.shape[1] * page
    kv_rank = cache_ckv.shape[3]

    cos_p, sin_p = _rope_tables(jnp.arange(s_p, dtype=jnp.int32))
    cos_s, sin_s = _rope_tables(jnp.tile(past_len + jnp.arange(t_s, dtype=jnp.int32), b_s))
    tpos = jnp.tile(jnp.arange(t_s, dtype=jnp.int32), b_s).reshape(-1, 1)

    xp, xs = x_prompt, x_sample.reshape(b_s * t_s, d)
    outs = [[] for _ in range(6)]
    for l in range(depth):
        w = _layer_weights(l, g_mix, w_in, conv_w, conv_b, g_q_lat, w_uq, g_kv_lat, w_uk, w_uv, g_q_nope,
                           g_q_rope, g_k_nope, g_k_rope, g_out, w_out, g_ffn, w_router_group, b_router_group,
                           w_router_expert, b_router_expert, w_gate, w_up, w_down)
        yconv_p, qt, kt, vt, ckv_p, kr_p, conv_p = _proj_prompt(xp, w, cos_p, sin_p)
        yattn_p = _attn_prompt(qt, kt, vt, w["g_out_a"])
        st = state_conv[l]
        zeros = lambda n: jnp.zeros((b_s, n, c), F32)
        st1 = jnp.concatenate([st[:, CONV_K - 2:], zeros(t_s - 1)], axis=1).reshape(b_s * t_s, c)
        st2 = jnp.concatenate([st, zeros(t_s - (CONV_K - 1))], axis=1).reshape(b_s * t_s, c)
        yconv_s, u_s, qa, qr, ckv_s, kr_s = _proj_sample(xs, tpos, st1, st2, w, cos_s, sin_s)
        yattn_s = _attn_decode(page_table, cache_ckv[l], cache_krope[l], qa, qr, ckv_s, kr_s, w, t_s)
        xmid, route, counts = _merge_route(yconv_p.reshape(b_p * s_p, c), yattn_p.reshape(b_p * s_p, -1),
                                           xp.reshape(b_p * s_p, d), yconv_s, yattn_s, xs, w)
        slot0, slot1, tok_of_slot, tile_expert, n_used = _moe_plan(route, counts)
        ys = _experts(tile_expert, n_used, tok_of_slot, xmid, w)
        yp, ysmp = _combine(slot0, slot1, ys, xmid, route, b_p * s_p)
        xp, xs = yp.reshape(b_p, s_p, d), ysmp
        for lst, val in zip(outs, (ckv_p, kr_p, conv_p, ckv_s.reshape(b_s, t_s, kv_rank),
                                   kr_s.reshape(b_s, t_s, ROPE_DIM),
                                   u_s.reshape(b_s, t_s, c)[:, t_s - (CONV_K - 1):])):
            lst.append(val)
    return (xp, xs.reshape(b_s, t_s, d)) + tuple(jnp.stack(o) for o in outs)
```

```python
import functools

import jax
import jax.numpy as jnp
from jax import lax
from jax.experimental import pallas as pl
from jax.experimental.pallas import tpu as pltpu

N_HEADS = 8
NOPE_DIM = 64
ROPE_DIM = 32
V_DIM = 64
HEAD_QK = NOPE_DIM + ROPE_DIM
HALF_ROPE = ROPE_DIM // 2
ROPE_BASE = 10000.0
CONV_K = 3
OUT_GROUP_DIM = 64
N_GROUPS = 4
EXPERTS_PER_GROUP = 8
N_EXPERTS = N_GROUPS * EXPERTS_PER_GROUP
TOP_K = 2
EPS = 1e-6
MASK_VALUE = -1e30
LOG2_E = 1.4426950408889634

LANES = 128
SUBLANES = 8
VMEM_LIMIT_BYTES = 48 * 1024 * 1024

PROJ_TILE = 512
ATTN_TILE = 512
TOKEN_TILE = 256
EXPERT_TILE = 256
DECODE_CHUNK_PAGES = 8

F32 = jnp.float32
BF16 = jnp.bfloat16

_NT = (((1,), (1,)), ((), ()))
_TN = (((0,), (0,)), ((), ()))


def _params(sem):
    return pltpu.CompilerParams(dimension_semantics=sem, vmem_limit_bytes=VMEM_LIMIT_BYTES)


def _rms_rows(x):
    return lax.rsqrt(jnp.mean(x * x, axis=0, keepdims=True) + EPS)


def _rms_lanes(x):
    return lax.rsqrt(jnp.mean(x * x, axis=-1, keepdims=True) + EPS)


def _group_norm_lanes(y, gain):
    lane = lax.broadcasted_iota(jnp.int32, (1, LANES), 1)
    low = lane < OUT_GROUP_DIM
    outs = []
    for j in range(y.shape[1] // LANES):
        t = y[:, j * LANES:(j + 1) * LANES]
        sq = t * t
        ss_lo = jnp.sum(jnp.where(low, sq, 0.0), axis=-1, keepdims=True)
        ss_hi = jnp.sum(jnp.where(low, 0.0, sq), axis=-1, keepdims=True)
        r = jnp.where(low, lax.rsqrt(ss_lo / OUT_GROUP_DIM + EPS), lax.rsqrt(ss_hi / OUT_GROUP_DIM + EPS))
        outs.append(t * r)
    return jnp.concatenate(outs, axis=1) * gain


def _rope_rows(x, cos, sin):
    x1, x2 = x[:HALF_ROPE], x[HALF_ROPE:]
    return x1 * cos - x2 * sin, x1 * sin + x2 * cos


def _attention_side(h_bf16, win_at_ref, gqlat_ref, wuqt_ref, gkv_ref, gqn_ref, gqr_ref, gkr_ref, cos, sin):
    q_rank = gqlat_ref.shape[0]
    kv_rank = gkv_ref.shape[0]
    scale = HEAD_QK ** -0.5 * LOG2_E
    zat = lax.dot_general(win_at_ref[...], h_bf16, _NT, preferred_element_type=F32)
    qlt = zat[:q_rank]
    kvt = zat[q_rank:q_rank + kv_rank]
    krt = zat[q_rank + kv_rank:]
    qln = (qlt * _rms_rows(qlt) * gqlat_ref[...]).astype(BF16)
    qt = jnp.dot(wuqt_ref[...], qln, preferred_element_type=F32)
    q_nope, q_rope = [], []
    for h in range(N_HEADS):
        nope = qt[h * HEAD_QK:h * HEAD_QK + NOPE_DIM]
        rope = qt[h * HEAD_QK + NOPE_DIM:(h + 1) * HEAD_QK]
        q_nope.append(nope * _rms_rows(nope) * gqn_ref[...] * scale)
        r1, r2 = _rope_rows(rope * _rms_rows(rope) * gqr_ref[...], cos, sin)
        q_rope.append((r1 * scale, r2 * scale))
    ckvt = kvt * _rms_rows(kvt) * gkv_ref[...]
    k1, k2 = _rope_rows(krt * _rms_rows(krt) * gkr_ref[...], cos, sin)
    return q_nope, q_rope, ckvt, (k1, k2)


def _to_token_major(xt, width):
    rows, toks = xt.shape
    if rows < LANES:
        xt = jnp.concatenate([xt, jnp.zeros((LANES - rows, toks), F32)], axis=0)
    return xt.T[:, :width]


def _proj_prompt_kernel(x_ref, gmix_ref, win_c_ref, win_at_ref, convw_ref, convb_ref, gout_c_ref,
                        gqlat_ref, wuqt_ref, gkv_ref, gqn_ref, gqr_ref, gkn_ref, gkr_ref,
                        wukt_ref, wuvt_ref, cos_ref, sin_ref,
                        yconv_ref, qt_ref, kt_ref, vt_ref, ckv_ref, krope_ref, convst_ref,
                        ext_ref):
    si = pl.program_id(1)
    tm = x_ref.shape[1]
    c = convw_ref.shape[1]

    xf = x_ref[0]
    h = (xf * _rms_lanes(xf) * gmix_ref[...]).astype(BF16)

    zc = jnp.dot(h, win_c_ref[...], preferred_element_type=F32)
    u = zc[:, 2 * c:] * zc[:, :c]

    @pl.when(si == 0)
    def _():
        ext_ref[0:SUBLANES, :] = jnp.zeros((SUBLANES, c), F32)

    ext_ref[SUBLANES:, :] = u
    v = (convb_ref[...]
         + convw_ref[0:1, :] * ext_ref[pl.ds(SUBLANES - 2, tm), :]
         + convw_ref[1:2, :] * ext_ref[pl.ds(SUBLANES - 1, tm), :]
         + convw_ref[2:3, :] * u)
    yconv = zc[:, c:2 * c] * v
    yconv_ref[0] = _group_norm_lanes(yconv, gout_c_ref[...]).astype(BF16)
    ext_ref[0:SUBLANES, :] = ext_ref[pl.ds(tm, SUBLANES), :]
    convst_ref[0] = ext_ref[pl.ds(SUBLANES - (CONV_K - 1), CONV_K - 1), :]

    q_nope, q_rope, ckvt, (k1, k2) = _attention_side(
        h, win_at_ref, gqlat_ref, wuqt_ref, gkv_ref, gqn_ref, gqr_ref, gkr_ref, cos_ref[...], sin_ref[...])
    for hd in range(N_HEADS):
        qt_ref[0, hd, 0:NOPE_DIM, :] = q_nope[hd].astype(BF16)
        qt_ref[0, hd, NOPE_DIM:NOPE_DIM + HALF_ROPE, :] = q_rope[hd][0].astype(BF16)
        qt_ref[0, hd, NOPE_DIM + HALF_ROPE:HEAD_QK, :] = q_rope[hd][1].astype(BF16)
    ckv_ref[0] = ckvt.T
    krt = jnp.concatenate([k1, k2], axis=0)
    krope_ref[0] = _to_token_major(krt, ROPE_DIM)
    ckv_b = ckvt.astype(BF16)
    ktn = jnp.dot(wukt_ref[...], ckv_b, preferred_element_type=F32)
    vt = jnp.dot(wuvt_ref[...], ckv_b, preferred_element_type=F32)
    krt_b = krt.astype(BF16)
    for hd in range(N_HEADS):
        blk = ktn[hd * NOPE_DIM:(hd + 1) * NOPE_DIM]
        kt_ref[0, hd, 0:NOPE_DIM, :] = (blk * _rms_rows(blk) * gkn_ref[...]).astype(BF16)
        kt_ref[0, hd, NOPE_DIM:HEAD_QK, :] = krt_b
        vt_ref[0, hd] = vt[hd * V_DIM:(hd + 1) * V_DIM].astype(BF16)


def _proj_prompt(x, w, cos_t, sin_t):
    b, s, d = x.shape
    tm = min(PROJ_TILE, s)
    assert s % tm == 0
    c = w["conv_w"].shape[1]
    kv_rank = w["g_kv"].shape[0]
    full = lambda a: pl.BlockSpec(a.shape, lambda bi, si: (0,) * a.ndim)
    weights = [w["g_mix"], w["win_c"], w["win_at"], w["conv_w"], w["conv_b"], w["g_out_c"],
               w["g_qlat"], w["wuq_t"], w["g_kv"], w["g_qn"], w["g_qr"], w["g_kn"], w["g_kr"],
               w["wuk_t"], w["wuv_t"]]
    in_specs = ([pl.BlockSpec((1, tm, d), lambda bi, si: (bi, si, 0))] + [full(a) for a in weights]
                + [pl.BlockSpec((HALF_ROPE, tm), lambda bi, si: (0, si))] * 2)
    out_shape = [
        jax.ShapeDtypeStruct((b, s, c), BF16),
        jax.ShapeDtypeStruct((b, N_HEADS, HEAD_QK, s), BF16),
        jax.ShapeDtypeStruct((b, N_HEADS, HEAD_QK, s), BF16),
        jax.ShapeDtypeStruct((b, N_HEADS, V_DIM, s), BF16),
        jax.ShapeDtypeStruct((b, s, kv_rank), F32),
        jax.ShapeDtypeStruct((b, s, ROPE_DIM), F32),
        jax.ShapeDtypeStruct((b, CONV_K - 1, c), F32),
    ]
    out_specs = [
        pl.BlockSpec((1, tm, c), lambda bi, si: (bi, si, 0)),
        pl.BlockSpec((1, N_HEADS, HEAD_QK, tm), lambda bi, si: (bi, 0, 0, si)),
        pl.BlockSpec((1, N_HEADS, HEAD_QK, tm), lambda bi, si: (bi, 0, 0, si)),
        pl.BlockSpec((1, N_HEADS, V_DIM, tm), lambda bi, si: (bi, 0, 0, si)),
        pl.BlockSpec((1, tm, kv_rank), lambda bi, si: (bi, si, 0)),
        pl.BlockSpec((1, tm, ROPE_DIM), lambda bi, si: (bi, si, 0)),
        pl.BlockSpec((1, CONV_K - 1, c), lambda bi, si: (bi, 0, 0)),
    ]
    return pl.pallas_call(
        _proj_prompt_kernel,
        grid=(b, s // tm),
        in_specs=in_specs,
        out_specs=out_specs,
        out_shape=out_shape,
        scratch_shapes=[pltpu.VMEM((tm + SUBLANES, c), F32)],
        compiler_params=_params(("arbitrary", "arbitrary")),
        name="proj_prompt",
    )(x, *weights, cos_t, sin_t)


def _attn_prompt_kernel(qi_ref, ki_ref, qt_ref, kt_ref, vt_ref, gout_ref, y_ref, m_ref, l_ref, acc_ref):
    p = pl.program_id(1)
    qi = qi_ref[p]
    ki = ki_ref[p]
    tq = qt_ref.shape[3]
    tk = kt_ref.shape[3]

    @pl.when(ki == 0)
    def _():
        m_ref[...] = jnp.full(m_ref.shape, -jnp.inf, F32)
        l_ref[...] = jnp.zeros(l_ref.shape, F32)
        acc_ref[...] = jnp.zeros(acc_ref.shape, F32)

    def block(masked):
        if masked:
            visible = (lax.broadcasted_iota(jnp.int32, (tk, tq), 0) <= lax.broadcasted_iota(jnp.int32, (tk, tq), 1))
        for hd in range(N_HEADS):
            s = lax.dot_general(kt_ref[0, hd], qt_ref[0, hd], _TN, preferred_element_type=F32)
            if masked:
                s = jnp.where(visible, s, MASK_VALUE)
            m_prev = m_ref[hd]
            m_new = jnp.maximum(m_prev, jnp.max(s, axis=0, keepdims=True))
            alpha = jnp.exp2(m_prev - m_new)
            pr = jnp.exp2(s - m_new)
            l_ref[hd] = alpha * l_ref[hd] + jnp.sum(pr, axis=0, keepdims=True)
            pv = jnp.dot(vt_ref[0, hd], pr.astype(BF16), preferred_element_type=F32)
            acc_ref[hd] = alpha * acc_ref[hd] + pv
            m_ref[hd] = m_new

    @pl.when(ki < qi)
    def _():
        block(False)

    @pl.when(ki == qi)
    def _():
        block(True)
        outs = []
        for hd in range(N_HEADS):
            o = acc_ref[hd] / l_ref[hd]
            outs.append(o * _rms_rows(o) * gout_ref[hd])
        y_ref[0] = jnp.concatenate(outs, axis=0).T.astype(BF16)


def _attn_prompt(qt, kt, vt, gout_a):
    b, _, _, s = qt.shape
    t = min(ATTN_TILE, s)
    assert s % t == 0
    n = s // t
    pairs = [(i, j) for i in range(n) for j in range(i + 1)]
    qi_tab = jnp.asarray([p[0] for p in pairs], jnp.int32)
    ki_tab = jnp.asarray([p[1] for p in pairs], jnp.int32)
    grid_spec = pltpu.PrefetchScalarGridSpec(
        num_scalar_prefetch=2,
        grid=(b, len(pairs)),
        in_specs=[
            pl.BlockSpec((1, N_HEADS, HEAD_QK, t), lambda bi, p, qi, ki: (bi, 0, 0, qi[p])),
            pl.BlockSpec((1, N_HEADS, HEAD_QK, t), lambda bi, p, qi, ki: (bi, 0, 0, ki[p])),
            pl.BlockSpec((1, N_HEADS, V_DIM, t), lambda bi, p, qi, ki: (bi, 0, 0, ki[p])),
            pl.BlockSpec(gout_a.shape, lambda bi, p, qi, ki: (0, 0, 0)),
        ],
        out_specs=pl.BlockSpec((1, t, N_HEADS * V_DIM), lambda bi, p, qi, ki: (bi, qi[p], 0)),
        scratch_shapes=[pltpu.VMEM((N_HEADS, 1, t), F32), pltpu.VMEM((N_HEADS, 1, t), F32),
                        pltpu.VMEM((N_HEADS, V_DIM, t), F32)],
    )
    return pl.pallas_call(
        _attn_prompt_kernel,
        grid_spec=grid_spec,
        out_shape=jax.ShapeDtypeStruct((b, s, N_HEADS * V_DIM), BF16),
        compiler_params=_params(("arbitrary", "arbitrary")),
        name="attn_prompt",
    )(qi_tab, ki_tab, qt, kt, vt, gout_a)


def _proj_sample_kernel(x_ref, tpos_ref, st1_ref, st2_ref, gmix_ref, win_c_ref, win_at_ref, convw_ref,
                        convb_ref, gout_c_ref, gqlat_ref, wuqt_ref, gkv_ref, gqn_ref, gqr_ref, gkn_ref,
                        gkr_ref, wuk_ref, cos_ref, sin_ref,
                        yconv_ref, u_ref, qa_ref, qr_ref, ckv_ref, krope_ref, ext_ref):
    tm = x_ref.shape[0]
    c = convw_ref.shape[1]
    xf = x_ref[...]
    h = (xf * _rms_lanes(xf) * gmix_ref[...]).astype(BF16)

    zc = jnp.dot(h, win_c_ref[...], preferred_element_type=F32)
    u = zc[:, 2 * c:] * zc[:, :c]
    ext_ref[0:SUBLANES, :] = jnp.zeros((SUBLANES, c), F32)
    ext_ref[SUBLANES:, :] = u
    tpos = tpos_ref[...]
    u_m2 = jnp.where(tpos >= 2, ext_ref[pl.ds(SUBLANES - 2, tm), :], st2_ref[...])
    u_m1 = jnp.where(tpos >= 1, ext_ref[pl.ds(SUBLANES - 1, tm), :], st1_ref[...])
    v = convb_ref[...] + convw_ref[0:1, :] * u_m2 + convw_ref[1:2, :] * u_m1 + convw_ref[2:3, :] * u
    yconv = zc[:, c:2 * c] * v
    yconv_ref[...] = _group_norm_lanes(yconv, gout_c_ref[...]).astype(BF16)
    u_ref[...] = u

    q_nope, q_rope, ckvt, (k1, k2) = _attention_side(
        h, win_at_ref, gqlat_ref, wuqt_ref, gkv_ref, gqn_ref, gqr_ref, gkr_ref, cos_ref[...], sin_ref[...])
    for hd in range(N_HEADS):
        qg = (q_nope[hd] * gkn_ref[...]).astype(BF16)
        qa_t = jnp.dot(wuk_ref[:, hd * NOPE_DIM:(hd + 1) * NOPE_DIM], qg, preferred_element_type=F32)
        qa_ref[hd] = qa_t.T
        qr_ref[hd] = _to_token_major(jnp.concatenate(q_rope[hd], axis=0), ROPE_DIM)
    ckv_ref[...] = ckvt.T
    krope_ref[...] = _to_token_major(jnp.concatenate([k1, k2], axis=0), ROPE_DIM)


def _proj_sample(x, tpos, st1, st2, w, cos_t, sin_t):
    tm, d = x.shape
    c = w["conv_w"].shape[1]
    kv_rank = w["g_kv"].shape[0]
    args = [x, tpos, st1, st2, w["g_mix"], w["win_c"], w["win_at"], w["conv_w"], w["conv_b"], w["g_out_c"],
            w["g_qlat"], w["wuq_t"], w["g_kv"], w["g_qn"], w["g_qr"], w["g_kn"], w["g_kr"], w["wuk"],
            cos_t, sin_t]
    out_shape = [
        jax.ShapeDtypeStruct((tm, c), BF16),
        jax.ShapeDtypeStruct((tm, c), F32),
        jax.ShapeDtypeStruct((N_HEADS, tm, kv_rank), F32),
        jax.ShapeDtypeStruct((N_HEADS, tm, ROPE_DIM), F32),
        jax.ShapeDtypeStruct((tm, kv_rank), F32),
        jax.ShapeDtypeStruct((tm, ROPE_DIM), F32),
    ]
    return pl.pallas_call(
        _proj_sample_kernel,
        out_shape=out_shape,
        scratch_shapes=[pltpu.VMEM((tm + SUBLANES, c), F32)],
        compiler_params=pltpu.CompilerParams(vmem_limit_bytes=VMEM_LIMIT_BYTES),
        name="proj_sample",
    )(*args)


def _attn_decode_kernel(pt_ref, ckv_hbm, kr_hbm, qa_ref, qr_ref, cnew_ref, krnew_ref, wukt_ref, wuv_ref,
                        gout_ref, y_ref, cbuf, kbuf, sem, *, layer, n_pages, page, t_dec):
    b = pl.program_id(0)
    chunk_pages = cbuf.shape[1] // page
    n_chunks = n_pages // chunk_pages
    rows = N_HEADS * t_dec

    def copies(j, slot):
        out = []
        for pg in range(chunk_pages):
            pid = pt_ref[b * n_pages + j * chunk_pages + pg]
            out.append(pltpu.make_async_copy(ckv_hbm.at[layer, pid], cbuf.at[slot, pl.ds(pg * page, page)], sem.at[0, slot]))
            out.append(pltpu.make_async_copy(kr_hbm.at[layer, pid], kbuf.at[slot, pl.ds(pg * page, page)], sem.at[1, slot]))
        return out

    def start(j, slot):
        for cp in copies(j, slot):
            cp.start()

    def wait(j, slot):
        for cp in copies(j, slot):
            cp.wait()

    qa = qa_ref[...].reshape(rows, qa_ref.shape[2]).astype(BF16)
    qr = qr_ref[...].reshape(rows, ROPE_DIM).astype(BF16)

    def scores(c_b, kr_b):
        ktn = lax.dot_general(wukt_ref[...], c_b, _NT, preferred_element_type=F32)
        rs = []
        for hd in range(N_HEADS):
            blk = ktn[hd * NOPE_DIM:(hd + 1) * NOPE_DIM]
            rs.append(jnp.broadcast_to(_rms_rows(blk), (t_dec, blk.shape[1])))
        rk = jnp.concatenate(rs, axis=0)
        return (lax.dot_general(qa, c_b, _NT, preferred_element_type=F32) * rk
                + lax.dot_general(qr, kr_b, _NT, preferred_element_type=F32))

    def update(carry, s, c_b):
        m_prev, l_prev, acc = carry
        m_new = jnp.maximum(m_prev, jnp.max(s, axis=-1, keepdims=True))
        alpha = jnp.exp2(m_prev - m_new)
        pr = jnp.exp2(s - m_new)
        l_new = alpha * l_prev + jnp.sum(pr, axis=-1, keepdims=True)
        acc = alpha * acc + jnp.dot(pr.astype(BF16), c_b, preferred_element_type=F32)
        return m_new, l_new, acc

    start(0, 0)

    def body(j, carry):
        slot = lax.rem(j, 2)

        @pl.when(j + 1 < n_chunks)
        def _():
            start(j + 1, 1 - slot)

        wait(j, slot)
        c_b = cbuf[slot].astype(BF16)
        kr_b = kbuf[slot].astype(BF16)
        return update(carry, scores(c_b, kr_b), c_b)

    init = (jnp.full((rows, 1), -jnp.inf, F32), jnp.zeros((rows, 1), F32), jnp.zeros((rows, cbuf.shape[2]), F32))
    carry = lax.fori_loop(0, n_chunks, body, init)

    pad = LANES - t_dec
    c_new = jnp.concatenate([cnew_ref[...], jnp.zeros((pad, cnew_ref.shape[1]), F32)], axis=0).astype(BF16)
    kr_new = jnp.concatenate([krnew_ref[...], jnp.zeros((pad, ROPE_DIM), F32)], axis=0).astype(BF16)
    s_new = scores(c_new, kr_new)
    q_t = lax.rem(lax.broadcasted_iota(jnp.int32, (rows, LANES), 0), t_dec)
    key = lax.broadcasted_iota(jnp.int32, (rows, LANES), 1)
    s_new = jnp.where(key <= q_t, s_new, MASK_VALUE)
    _, l_fin, acc = update(carry, s_new, c_new)

    o_lat = (acc / l_fin).astype(BF16)
    ov = jnp.dot(o_lat, wuv_ref[...], preferred_element_type=F32)
    outs = []
    for hd in range(N_HEADS):
        o = ov[hd * t_dec:(hd + 1) * t_dec, hd * V_DIM:(hd + 1) * V_DIM]
        outs.append(o * _rms_lanes(o) * gout_ref[hd])
    y_ref[...] = jnp.concatenate(outs, axis=1)


def _attn_decode(layer, page_table, cache_ckv, cache_kr, qa, qr, c_new, kr_new, w, t_dec):
    n_seq, n_pages = page_table.shape
    _, _, page, kv_rank = cache_ckv.shape
    chunk_pages = min(DECODE_CHUNK_PAGES, n_pages)
    assert n_pages % chunk_pages == 0 and t_dec % SUBLANES == 0 and t_dec <= LANES
    chunk = chunk_pages * page
    grid_spec = pltpu.PrefetchScalarGridSpec(
        num_scalar_prefetch=1,
        grid=(n_seq,),
        in_specs=[
            pl.BlockSpec(memory_space=pl.ANY),
            pl.BlockSpec(memory_space=pl.ANY),
            pl.BlockSpec((N_HEADS, t_dec, kv_rank), lambda b, pt: (0, b, 0)),
            pl.BlockSpec((N_HEADS, t_dec, ROPE_DIM), lambda b, pt: (0, b, 0)),
            pl.BlockSpec((t_dec, kv_rank), lambda b, pt: (b, 0)),
            pl.BlockSpec((t_dec, ROPE_DIM), lambda b, pt: (b, 0)),
            pl.BlockSpec(w["wuk_t"].shape, lambda b, pt: (0, 0)),
            pl.BlockSpec(w["wuv"].shape, lambda b, pt: (0, 0)),
            pl.BlockSpec(w["g_out_a_row"].shape, lambda b, pt: (0, 0, 0)),
        ],
        out_specs=pl.BlockSpec((t_dec, N_HEADS * V_DIM), lambda b, pt: (b, 0)),
        scratch_shapes=[pltpu.VMEM((2, chunk, kv_rank), F32), pltpu.VMEM((2, chunk, ROPE_DIM), F32),
                        pltpu.SemaphoreType.DMA((2, 2))],
    )
    return pl.pallas_call(
        functools.partial(_attn_decode_kernel, layer=layer, n_pages=n_pages, page=page, t_dec=t_dec),
        grid_spec=grid_spec,
        out_shape=jax.ShapeDtypeStruct((n_seq * t_dec, N_HEADS * V_DIM), F32),
        compiler_params=_params(("arbitrary",)),
        name="attn_decode",
    )(page_table.reshape(-1), cache_ckv, cache_kr, qa, qr, c_new, kr_new, w["wuk_t"], w["wuv"], w["g_out_a_row"])


ROUTE_E0, ROUTE_E1, ROUTE_R0, ROUTE_R1, ROUTE_W0, ROUTE_W1 = range(6)


def _lane_pick(x, lane, idx):
    return jnp.sum(jnp.where(lane == idx, x, 0.0), axis=-1, keepdims=True)


def _merge_route_kernel(ycp_ref, yap_ref, xp_ref, ycs_ref, yas_ref, xs_ref, wout_ref, gffn_ref, wr_ref, br_ref,
                        xmid_ref, route_ref, counts_ref, carry_ref, *, n_prompt_tiles):
    i = pl.program_id(0)
    tm = xp_ref.shape[0]
    is_p = i < n_prompt_tiles

    @pl.when(i == 0)
    def _():
        carry_ref[...] = jnp.zeros(carry_ref.shape, F32)

    yc = jnp.where(is_p, ycp_ref[...], ycs_ref[...])
    ya = jnp.where(is_p, yap_ref[...], yas_ref[...].astype(BF16))
    x = jnp.where(is_p, xp_ref[...], xs_ref[...])
    y = jnp.concatenate([yc, ya], axis=1)
    xm = x + jnp.dot(y, wout_ref[...], preferred_element_type=F32)
    xmid_ref[...] = xm
    h2 = (xm * _rms_lanes(xm) * gffn_ref[...]).astype(BF16)
    logits = jnp.dot(h2, wr_ref[...], preferred_element_type=F32) + br_ref[...]

    lane_i = lax.broadcasted_iota(jnp.int32, (tm, LANES), 1)
    lane = lane_i.astype(F32)
    neg = -jnp.inf
    far = float(LANES)
    in_groups = lane_i < N_GROUPS
    gl = jnp.where(in_groups, logits, neg)
    ge = jnp.exp(gl - jnp.max(gl, axis=-1, keepdims=True))
    pg = ge / jnp.sum(ge, axis=-1, keepdims=True)
    p_sel = jnp.max(pg, axis=-1, keepdims=True)
    g_sel = jnp.min(jnp.where((pg == p_sel) & in_groups, lane, far), axis=-1, keepdims=True)
    lo = N_GROUPS + g_sel * EXPERTS_PER_GROUP
    el = jnp.where((lane >= lo) & (lane < lo + EXPERTS_PER_GROUP), logits, neg)
    v1 = jnp.max(el, axis=-1, keepdims=True)
    i1 = jnp.min(jnp.where(el == v1, lane, far), axis=-1, keepdims=True)
    el2 = jnp.where(lane == i1, neg, el)
    v2 = jnp.max(el2, axis=-1, keepdims=True)
    i2 = jnp.min(jnp.where(el2 == v2, lane, far), axis=-1, keepdims=True)
    e2 = jnp.exp(v2 - v1)
    w0 = 1.0 / (1.0 + e2) * p_sel
    w1 = e2 / (1.0 + e2) * p_sel
    e0 = i1 - N_GROUPS
    e1 = i2 - N_GROUPS

    oh0 = lane == e0
    oh1 = lane == e1
    onehot = (oh0 | oh1).astype(BF16)
    tri = (lax.broadcasted_iota(jnp.int32, (tm, tm), 0) > lax.broadcasted_iota(jnp.int32, (tm, tm), 1)).astype(BF16)
    before = jnp.dot(tri, onehot, preferred_element_type=F32) + carry_ref[...]
    r0 = jnp.sum(jnp.where(oh0, before, 0.0), axis=-1, keepdims=True)
    r1 = jnp.sum(jnp.where(oh1, before, 0.0), axis=-1, keepdims=True)
    carry_ref[...] += jnp.sum(onehot.astype(F32), axis=0, keepdims=True)

    route = jnp.zeros((tm, LANES), F32)
    for idx, val in ((ROUTE_E0, e0), (ROUTE_E1, e1), (ROUTE_R0, r0), (ROUTE_R1, r1), (ROUTE_W0, w0), (ROUTE_W1, w1)):
        route = jnp.where(lane_i == idx, val, route)
    route_ref[...] = route

    @pl.when(i == pl.num_programs(0) - 1)
    def _():
        counts_ref[...] = carry_ref[...]


def _merge_route(ycp, yap, xp, ycs, yas, xs, w):
    tp, d = xp.shape
    ts = xs.shape[0]
    tm = TOKEN_TILE
    assert tp % tm == 0 and ts % tm == 0
    npt, nst = tp // tm, ts // tm
    half = ycp.shape[1]
    pmap = lambda i: (jnp.minimum(i, npt - 1), 0)
    smap = lambda i: (jnp.maximum(i - npt, 0), 0)
    cmap = lambda i: (0, 0)
    return pl.pallas_call(
        functools.partial(_merge_route_kernel, n_prompt_tiles=npt),
        grid=(npt + nst,),
        in_specs=[
            pl.BlockSpec((tm, half), pmap), pl.BlockSpec((tm, half), pmap), pl.BlockSpec((tm, d), pmap),
            pl.BlockSpec((tm, half), smap), pl.BlockSpec((tm, half), smap), pl.BlockSpec((tm, d), smap),
            pl.BlockSpec(w["w_out"].shape, cmap), pl.BlockSpec(w["g_ffn"].shape, cmap),
            pl.BlockSpec(w["w_r"].shape, cmap), pl.BlockSpec(w["b_r"].shape, cmap),
        ],
        out_specs=[pl.BlockSpec((tm, d), lambda i: (i, 0)), pl.BlockSpec((tm, LANES), lambda i: (i, 0)),
                   pl.BlockSpec((1, LANES), cmap)],
        out_shape=[jax.ShapeDtypeStruct((tp + ts, d), F32), jax.ShapeDtypeStruct((tp + ts, LANES), F32),
                   jax.ShapeDtypeStruct((1, LANES), F32)],
        scratch_shapes=[pltpu.VMEM((1, LANES), F32)],
        compiler_params=_params(("arbitrary",)),
        name="merge_route",
    )(ycp, yap, xp, ycs, yas, xs, w["w_out"], w["g_ffn"], w["w_r"], w["b_r"])


def _row_gather(src_hbm, idx_ref, dst, sem, n_rows):
    def body(r, _):
        pltpu.make_async_copy(src_hbm.at[pl.ds(idx_ref[0, 0, r], 1)], dst.at[pl.ds(r, 1)], sem).start()
        return 0
    lax.fori_loop(0, n_rows, body, 0, unroll=8)


def _row_gather_wait(src_hbm, dst, sem):
    for r in range(dst.shape[0]):
        pltpu.make_async_copy(src_hbm.at[pl.ds(0, 1)], dst.at[pl.ds(r, 1)], sem).wait()


def _experts_kernel(te_ref, nu_ref, idx_ref, idx_next_ref, x_hbm, gffn_ref, wgu_ref, wd_ref, y_ref, buf, sem):
    i = pl.program_id(0)
    n_used = nu_ref[0]
    tme = buf.shape[1]
    d_exp = wd_ref.shape[1]
    slot = lax.rem(i, 2)

    @pl.when(i == 0)
    def _():
        _row_gather(x_hbm, idx_ref, buf.at[0], sem.at[0], tme)

    @pl.when(i + 1 < n_used)
    def _():
        _row_gather(x_hbm, idx_next_ref, buf.at[1 - slot], sem.at[1 - slot], tme)

    @pl.when(i < n_used)
    def _():
        _row_gather_wait(x_hbm, buf.at[slot], sem.at[slot])
        xm = buf[slot]
        h2 = (xm * _rms_lanes(xm) * gffn_ref[...]).astype(BF16)
        gu = jnp.dot(h2, wgu_ref[0], preferred_element_type=F32)
        g = gu[:, :d_exp]
        a = (g / (1.0 + jnp.exp(-g))) * gu[:, d_exp:]
        y_ref[...] = jnp.dot(a.astype(BF16), wd_ref[0], preferred_element_type=F32)

    @pl.when(i >= n_used)
    def _():
        y_ref[...] = jnp.zeros(y_ref.shape, F32)


def _experts(tile_expert, n_used, tok_of_slot, xmid, w):
    n_tiles = tile_expert.shape[0]
    tme = EXPERT_TILE
    d = xmid.shape[1]
    idx3 = tok_of_slot.reshape(n_tiles, 1, tme)
    smem_blk = lambda f: pl.BlockSpec((1, 1, tme), f, memory_space=pltpu.SMEM)
    grid_spec = pltpu.PrefetchScalarGridSpec(
        num_scalar_prefetch=2,
        grid=(n_tiles,),
        in_specs=[
            smem_blk(lambda i, te, nu: (i, 0, 0)),
            smem_blk(lambda i, te, nu: (jnp.minimum(i + 1, n_tiles - 1), 0, 0)),
            pl.BlockSpec(memory_space=pl.ANY),
            pl.BlockSpec(w["g_ffn"].shape, lambda i, te, nu: (0, 0)),
            pl.BlockSpec((1,) + w["w_gu"].shape[1:], lambda i, te, nu: (te[i], 0, 0)),
            pl.BlockSpec((1,) + w["w_d"].shape[1:], lambda i, te, nu: (te[i], 0, 0)),
        ],
        out_specs=pl.BlockSpec((tme, d), lambda i, te, nu: (i, 0)),
        scratch_shapes=[pltpu.VMEM((2, tme, d), F32), pltpu.SemaphoreType.DMA((2,))],
    )
    return pl.pallas_call(
        _experts_kernel,
        grid_spec=grid_spec,
        out_shape=jax.ShapeDtypeStruct((n_tiles * tme, d), F32),
        compiler_params=_params(("arbitrary",)),
        name="experts",
    )(tile_expert, n_used, idx3, idx3, xmid, w["g_ffn"], w["w_gu"], w["w_d"])


def _combine_kernel(s0_ref, s1_ref, s0n_ref, s1n_ref, ys_hbm, xmid_ref, route_ref, yp_ref, ysmp_ref, buf, sem,
                    *, n_prompt_tiles):
    i = pl.program_id(0)
    n = pl.num_programs(0)
    tm = xmid_ref.shape[0]
    slot = lax.rem(i, 2)

    @pl.when(i == 0)
    def _():
        _row_gather(ys_hbm, s0_ref, buf.at[0, 0], sem.at[0, 0], tm)
        _row_gather(ys_hbm, s1_ref, buf.at[0, 1], sem.at[0, 1], tm)

    @pl.when(i + 1 < n)
    def _():
        _row_gather(ys_hbm, s0n_ref, buf.at[1 - slot, 0], sem.at[1 - slot, 0], tm)
        _row_gather(ys_hbm, s1n_ref, buf.at[1 - slot, 1], sem.at[1 - slot, 1], tm)

    _row_gather_wait(ys_hbm, buf.at[slot, 0], sem.at[slot, 0])
    _row_gather_wait(ys_hbm, buf.at[slot, 1], sem.at[slot, 1])
    route = route_ref[...]
    lane = lax.broadcasted_iota(jnp.int32, route.shape, 1)
    w0 = _lane_pick(route, lane, ROUTE_W0)
    w1 = _lane_pick(route, lane, ROUTE_W1)
    out = xmid_ref[...] + (w0 * buf[slot, 0] + w1 * buf[slot, 1])

    @pl.when(i < n_prompt_tiles)
    def _():
        yp_ref[...] = out

    @pl.when(i >= n_prompt_tiles)
    def _():
        ysmp_ref[...] = out


def _combine(slot0, slot1, ys, xmid, route, tp):
    ttot, d = xmid.shape
    tm = TOKEN_TILE
    n = ttot // tm
    npt = tp // tm
    s0 = slot0.reshape(n, 1, tm)
    s1 = slot1.reshape(n, 1, tm)
    cur = lambda i: (i, 0, 0)
    nxt = lambda i: (jnp.minimum(i + 1, n - 1), 0, 0)
    smem_blk = lambda f: pl.BlockSpec((1, 1, tm), f, memory_space=pltpu.SMEM)
    return pl.pallas_call(
        functools.partial(_combine_kernel, n_prompt_tiles=npt),
        grid=(n,),
        in_specs=[smem_blk(cur), smem_blk(cur), smem_blk(nxt), smem_blk(nxt),
                  pl.BlockSpec(memory_space=pl.ANY),
                  pl.BlockSpec((tm, d), lambda i: (i, 0)), pl.BlockSpec((tm, LANES), lambda i: (i, 0))],
        out_specs=[pl.BlockSpec((tm, d), lambda i: (jnp.minimum(i, npt - 1), 0)),
                   pl.BlockSpec((tm, d), lambda i: (jnp.maximum(i - npt, 0), 0))],
        out_shape=[jax.ShapeDtypeStruct((tp, d), F32), jax.ShapeDtypeStruct((ttot - tp, d), F32)],
        scratch_shapes=[pltpu.VMEM((2, 2, tm, d), F32), pltpu.SemaphoreType.DMA((2, 2))],
        compiler_params=_params(("arbitrary",)),
        name="combine",
    )(s0, s1, s0, s1, ys, xmid, route)


def _rope_tables(pos):
    inv_freq = ROPE_BASE ** (-jnp.arange(HALF_ROPE, dtype=F32) / HALF_ROPE)
    ang = pos.astype(F32)[:, None] * inv_freq[None, :]
    return jnp.cos(ang).T, jnp.sin(ang).T


def _layer_weights(l, g_mix, w_in, conv_w, conv_b, g_q_lat, w_uq, g_kv_lat, w_uk, w_uv, g_q_nope, g_q_rope,
                   g_k_nope, g_k_rope, g_out, w_out, g_ffn, w_router_group, b_router_group, w_router_expert,
                   b_router_expert, w_gate, w_up, w_down):
    c = conv_w.shape[2]
    col = lambda g: g[l].reshape(-1, 1)
    w_r = jnp.concatenate([w_router_group[l], w_router_expert[l]], axis=1)
    b_r = jnp.concatenate([b_router_group[l], b_router_expert[l]])
    return {
        "g_mix": g_mix[l].reshape(1, -1),
        "win_c": w_in[l][:, :3 * c].astype(BF16),
        "win_at": w_in[l][:, 3 * c:].T.astype(BF16),
        "conv_w": conv_w[l],
        "conv_b": conv_b[l].reshape(1, -1),
        "g_out_c": g_out[l][:c].reshape(1, -1),
        "g_out_a": g_out[l][c:].reshape(N_HEADS, V_DIM, 1),
        "g_out_a_row": g_out[l][c:].reshape(N_HEADS, 1, V_DIM),
        "g_qlat": col(g_q_lat), "g_kv": col(g_kv_lat), "g_qn": col(g_q_nope), "g_qr": col(g_q_rope),
        "g_kn": col(g_k_nope), "g_kr": col(g_k_rope),
        "wuq_t": w_uq[l].T.astype(BF16),
        "wuk_t": w_uk[l].T.astype(BF16),
        "wuk": w_uk[l].astype(BF16),
        "wuv_t": w_uv[l].T.astype(BF16),
        "wuv": w_uv[l].astype(BF16),
        "w_out": w_out[l].astype(BF16),
        "g_ffn": g_ffn[l].reshape(1, -1),
        "w_r": jnp.pad(w_r, ((0, 0), (0, LANES - w_r.shape[1]))).astype(BF16),
        "b_r": jnp.pad(b_r, (0, LANES - b_r.shape[0])).reshape(1, -1),
        "w_gu": jnp.concatenate([w_gate[l], w_up[l]], axis=-1).astype(BF16),
        "w_d": w_down[l].astype(BF16),
    }


def _moe_plan(route, counts):
    ttot = route.shape[0]
    tme = EXPERT_TILE
    n_tiles = (TOP_K * ttot) // tme + N_EXPERTS
    cnt = counts[0, :N_EXPERTS].astype(jnp.int32)
    tiles = (cnt + tme - 1) // tme
    tile_end = jnp.cumsum(tiles)
    base = (tile_end - tiles) * tme
    e0 = route[:, ROUTE_E0].astype(jnp.int32)
    e1 = route[:, ROUTE_E1].astype(jnp.int32)
    slot0 = base[e0] + route[:, ROUTE_R0].astype(jnp.int32)
    slot1 = base[e1] + route[:, ROUTE_R1].astype(jnp.int32)
    tok = jnp.arange(ttot, dtype=jnp.int32)
    tok_of_slot = jnp.zeros((n_tiles * tme,), jnp.int32).at[slot0].set(tok).at[slot1].set(tok)
    n_used = tile_end[-1:]
    tile_id = jnp.minimum(jnp.arange(n_tiles, dtype=jnp.int32), n_used[0] - 1)
    tile_expert = jnp.sum((tile_end[None, :] <= tile_id[:, None]).astype(jnp.int32), axis=1)
    return slot0, slot1, tok_of_slot, tile_expert, n_used.astype(jnp.int32)


def kernel(x_prompt, x_sample, state_conv, cache_ckv, cache_krope, page_table, g_mix, w_in, conv_w, conv_b, g_q_lat, w_uq, g_kv_lat, w_uk, w_uv, g_q_nope, g_q_rope, g_k_nope, g_k_rope, g_out, w_out, g_ffn, w_router_group, b_router_group, w_router_expert, b_router_expert, w_gate, w_up, w_down):
    b_p, s_p, d = x_prompt.shape
    b_s, t_s, _ = x_sample.shape
    depth = g_mix.shape[0]
    c = conv_w.shape[2]
    page = cache_ckv.shape[2]
    past_len = page_table.shape[1] * page
    kv_rank = cache_ckv.shape[3]

    cos_p, sin_p = _rope_tables(jnp.arange(s_p, dtype=jnp.int32))
    cos_s, sin_s = _rope_tables(jnp.tile(past_len + jnp.arange(t_s, dtype=jnp.int32), b_s))
    tpos = jnp.tile(jnp.arange(t_s, dtype=jnp.int32), b_s).reshape(-1, 1)

    xp, xs = x_prompt, x_sample.reshape(b_s * t_s, d)
    outs = [[] for _ in range(6)]
    for l in range(depth):
        w = _layer_weights(l, g_mix, w_in, conv_w, conv_b, g_q_lat, w_uq, g_kv_lat, w_uk, w_uv, g_q_nope,
                           g_q_rope, g_k_nope, g_k_rope, g_out, w_out, g_ffn, w_router_group, b_router_group,
                           w_router_expert, b_router_expert, w_gate, w_up, w_down)
        yconv_p, qt, kt, vt, ckv_p, kr_p, conv_p = _proj_prompt(xp, w, cos_p, sin_p)
        yattn_p = _attn_prompt(qt, kt, vt, w["g_out_a"])
        st = state_conv[l]
        zeros = lambda n: jnp.zeros((b_s, n, c), F32)
        st1 = jnp.concatenate([st[:, CONV_K - 2:], zeros(t_s - 1)], axis=1).reshape(b_s * t_s, c)
        st2 = jnp.concatenate([st, zeros(t_s - (CONV_K - 1))], axis=1).reshape(b_s * t_s, c)
        yconv_s, u_s, qa, qr, ckv_s, kr_s = _proj_sample(xs, tpos, st1, st2, w, cos_s, sin_s)
        yattn_s = _attn_decode(l, page_table, cache_ckv, cache_krope, qa, qr, ckv_s, kr_s, w, t_s)
        xmid, route, counts = _merge_route(yconv_p.reshape(b_p * s_p, c), yattn_p.reshape(b_p * s_p, -1),
                                           xp.reshape(b_p * s_p, d), yconv_s, yattn_s, xs, w)
        slot0, slot1, tok_of_slot, tile_expert, n_used = _moe_plan(route, counts)
        ys = _experts(tile_expert, n_used, tok_of_slot, xmid, w)
        yp, ysmp = _combine(slot0, slot1, ys, xmid, route, b_p * s_p)
        xp, xs = yp.reshape(b_p, s_p, d), ysmp
        for lst, val in zip(outs, (ckv_p, kr_p, conv_p, ckv_s.reshape(b_s, t_s, kv_rank),
                                   kr_s.reshape(b_s, t_s, ROPE_DIM),
                                   u_s.reshape(b_s, t_s, c)[:, t_s - (CONV_K - 1):])):
            lst.append(val)
    return (xp, xs.reshape(b_s, t_s, d)) + tuple(jnp.stack(o) for o in outs)
```

```python
import functools

import jax
import jax.numpy as jnp
from jax import lax
from jax.experimental import pallas as pl
from jax.experimental.pallas import tpu as pltpu

N_HEADS = 8
NOPE_DIM = 64
ROPE_DIM = 32
V_DIM = 64
HEAD_QK = NOPE_DIM + ROPE_DIM
HALF_ROPE = ROPE_DIM // 2
ROPE_BASE = 10000.0
CONV_K = 3
OUT_GROUP_DIM = 64
N_GROUPS = 4
EXPERTS_PER_GROUP = 8
N_EXPERTS = N_GROUPS * EXPERTS_PER_GROUP
TOP_K = 2
EPS = 1e-6
MASK_VALUE = -1e30
LOG2_E = 1.4426950408889634

LANES = 128
SUBLANES = 8
VMEM_LIMIT_BYTES = 48 * 1024 * 1024

PROJ_TILE = 512
ATTN_TILE = 512
TOKEN_TILE = 256
EXPERT_TILE = 256
DECODE_CHUNK_PAGES = 8
SLOT_BATCH = 8

F32 = jnp.float32
BF16 = jnp.bfloat16

_NT = (((1,), (1,)), ((), ()))
_TN = (((0,), (0,)), ((), ()))


def _params(sem):
    return pltpu.CompilerParams(dimension_semantics=sem, vmem_limit_bytes=VMEM_LIMIT_BYTES)


def _rms_rows(x):
    return lax.rsqrt(jnp.mean(x * x, axis=0, keepdims=True) + EPS)


def _rms_lanes(x):
    return lax.rsqrt(jnp.mean(x * x, axis=-1, keepdims=True) + EPS)


def _group_norm_lanes(y, gain):
    lane = lax.broadcasted_iota(jnp.int32, (1, LANES), 1)
    low = lane < OUT_GROUP_DIM
    outs = []
    for j in range(y.shape[1] // LANES):
        t = y[:, j * LANES:(j + 1) * LANES]
        sq = t * t
        ss_lo = jnp.sum(jnp.where(low, sq, 0.0), axis=-1, keepdims=True)
        ss_hi = jnp.sum(jnp.where(low, 0.0, sq), axis=-1, keepdims=True)
        r = jnp.where(low, lax.rsqrt(ss_lo / OUT_GROUP_DIM + EPS), lax.rsqrt(ss_hi / OUT_GROUP_DIM + EPS))
        outs.append(t * r)
    return jnp.concatenate(outs, axis=1) * gain


def _rope_rows(x, cos, sin):
    x1, x2 = x[:HALF_ROPE], x[HALF_ROPE:]
    return x1 * cos - x2 * sin, x1 * sin + x2 * cos


def _attention_side(h_bf16, win_at_ref, gqlat_ref, wuqt_ref, gkv_ref, gqn_ref, gqr_ref, gkr_ref, cos, sin):
    q_rank = gqlat_ref.shape[0]
    kv_rank = gkv_ref.shape[0]
    scale = HEAD_QK ** -0.5 * LOG2_E
    zat = lax.dot_general(win_at_ref[...], h_bf16, _NT, preferred_element_type=F32)
    qlt = zat[:q_rank]
    kvt = zat[q_rank:q_rank + kv_rank]
    krt = zat[q_rank + kv_rank:]
    qln = (qlt * _rms_rows(qlt) * gqlat_ref[...]).astype(BF16)
    qt = jnp.dot(wuqt_ref[...], qln, preferred_element_type=F32)
    q_nope, q_rope = [], []
    for h in range(N_HEADS):
        nope = qt[h * HEAD_QK:h * HEAD_QK + NOPE_DIM]
        rope = qt[h * HEAD_QK + NOPE_DIM:(h + 1) * HEAD_QK]
        q_nope.append(nope * _rms_rows(nope) * gqn_ref[...] * scale)
        r1, r2 = _rope_rows(rope * _rms_rows(rope) * gqr_ref[...], cos, sin)
        q_rope.append((r1 * scale, r2 * scale))
    ckvt = kvt * _rms_rows(kvt) * gkv_ref[...]
    k1, k2 = _rope_rows(krt * _rms_rows(krt) * gkr_ref[...], cos, sin)
    return q_nope, q_rope, ckvt, (k1, k2)


def _to_token_major(xt, width):
    rows, toks = xt.shape
    if rows < LANES:
        xt = jnp.concatenate([xt, jnp.zeros((LANES - rows, toks), F32)], axis=0)
    return xt.T[:, :width]


def _proj_prompt_kernel(x_ref, gmix_ref, win_c_ref, win_at_ref, convw_ref, convb_ref, gout_c_ref,
                        gqlat_ref, wuqt_ref, gkv_ref, gqn_ref, gqr_ref, gkn_ref, gkr_ref,
                        wukt_ref, wuvt_ref, cos_ref, sin_ref,
                        yconv_ref, qt_ref, kt_ref, vt_ref, ckv_ref, krope_ref, convst_ref,
                        ext_ref):
    si = pl.program_id(1)
    tm = x_ref.shape[1]
    c = convw_ref.shape[1]

    xf = x_ref[0]
    h = (xf * _rms_lanes(xf) * gmix_ref[...]).astype(BF16)

    zc = jnp.dot(h, win_c_ref[...], preferred_element_type=F32)
    u = zc[:, 2 * c:] * zc[:, :c]

    @pl.when(si == 0)
    def _():
        ext_ref[0:SUBLANES, :] = jnp.zeros((SUBLANES, c), F32)

    ext_ref[SUBLANES:, :] = u
    v = (convb_ref[...]
         + convw_ref[0:1, :] * ext_ref[pl.ds(SUBLANES - 2, tm), :]
         + convw_ref[1:2, :] * ext_ref[pl.ds(SUBLANES - 1, tm), :]
         + convw_ref[2:3, :] * u)
    yconv = zc[:, c:2 * c] * v
    yconv_ref[0] = _group_norm_lanes(yconv, gout_c_ref[...]).astype(BF16)
    ext_ref[0:SUBLANES, :] = ext_ref[pl.ds(tm, SUBLANES), :]
    convst_ref[0] = ext_ref[pl.ds(SUBLANES - (CONV_K - 1), CONV_K - 1), :]

    q_nope, q_rope, ckvt, (k1, k2) = _attention_side(
        h, win_at_ref, gqlat_ref, wuqt_ref, gkv_ref, gqn_ref, gqr_ref, gkr_ref, cos_ref[...], sin_ref[...])
    for hd in range(N_HEADS):
        qt_ref[0, hd, 0:NOPE_DIM, :] = q_nope[hd].astype(BF16)
        qt_ref[0, hd, NOPE_DIM:NOPE_DIM + HALF_ROPE, :] = q_rope[hd][0].astype(BF16)
        qt_ref[0, hd, NOPE_DIM + HALF_ROPE:HEAD_QK, :] = q_rope[hd][1].astype(BF16)
    ckv_ref[0] = ckvt.T
    krt = jnp.concatenate([k1, k2], axis=0)
    krope_ref[0] = _to_token_major(krt, ROPE_DIM)
    ckv_b = ckvt.astype(BF16)
    ktn = jnp.dot(wukt_ref[...], ckv_b, preferred_element_type=F32)
    vt = jnp.dot(wuvt_ref[...], ckv_b, preferred_element_type=F32)
    krt_b = krt.astype(BF16)
    for hd in range(N_HEADS):
        blk = ktn[hd * NOPE_DIM:(hd + 1) * NOPE_DIM]
        kt_ref[0, hd, 0:NOPE_DIM, :] = (blk * _rms_rows(blk) * gkn_ref[...]).astype(BF16)
        kt_ref[0, hd, NOPE_DIM:HEAD_QK, :] = krt_b
        vt_ref[0, hd] = vt[hd * V_DIM:(hd + 1) * V_DIM].astype(BF16)


def _proj_prompt(x, w, cos_t, sin_t):
    b, s, d = x.shape
    tm = min(PROJ_TILE, s)
    assert s % tm == 0
    c = w["conv_w"].shape[1]
    kv_rank = w["g_kv"].shape[0]
    full = lambda a: pl.BlockSpec(a.shape, lambda bi, si: (0,) * a.ndim)
    weights = [w["g_mix"], w["win_c"], w["win_at"], w["conv_w"], w["conv_b"], w["g_out_c"],
               w["g_qlat"], w["wuq_t"], w["g_kv"], w["g_qn"], w["g_qr"], w["g_kn"], w["g_kr"],
               w["wuk_t"], w["wuv_t"]]
    in_specs = ([pl.BlockSpec((1, tm, d), lambda bi, si: (bi, si, 0))] + [full(a) for a in weights]
                + [pl.BlockSpec((HALF_ROPE, tm), lambda bi, si: (0, si))] * 2)
    out_shape = [
        jax.ShapeDtypeStruct((b, s, c), BF16),
        jax.ShapeDtypeStruct((b, N_HEADS, HEAD_QK, s), BF16),
        jax.ShapeDtypeStruct((b, N_HEADS, HEAD_QK, s), BF16),
        jax.ShapeDtypeStruct((b, N_HEADS, V_DIM, s), BF16),
        jax.ShapeDtypeStruct((b, s, kv_rank), F32),
        jax.ShapeDtypeStruct((b, s, ROPE_DIM), F32),
        jax.ShapeDtypeStruct((b, CONV_K - 1, c), F32),
    ]
    out_specs = [
        pl.BlockSpec((1, tm, c), lambda bi, si: (bi, si, 0)),
        pl.BlockSpec((1, N_HEADS, HEAD_QK, tm), lambda bi, si: (bi, 0, 0, si)),
        pl.BlockSpec((1, N_HEADS, HEAD_QK, tm), lambda bi, si: (bi, 0, 0, si)),
        pl.BlockSpec((1, N_HEADS, V_DIM, tm), lambda bi, si: (bi, 0, 0, si)),
        pl.BlockSpec((1, tm, kv_rank), lambda bi, si: (bi, si, 0)),
        pl.BlockSpec((1, tm, ROPE_DIM), lambda bi, si: (bi, si, 0)),
        pl.BlockSpec((1, CONV_K - 1, c), lambda bi, si: (bi, 0, 0)),
    ]
    return pl.pallas_call(
        _proj_prompt_kernel,
        grid=(b, s // tm),
        in_specs=in_specs,
        out_specs=out_specs,
        out_shape=out_shape,
        scratch_shapes=[pltpu.VMEM((tm + SUBLANES, c), F32)],
        compiler_params=_params(("arbitrary", "arbitrary")),
        name="proj_prompt",
    )(x, *weights, cos_t, sin_t)


def _attn_prompt_kernel(qi_ref, ki_ref, qt_ref, kt_ref, vt_ref, gout_ref, y_ref, m_ref, l_ref, acc_ref):
    p = pl.program_id(1)
    qi = qi_ref[p]
    ki = ki_ref[p]
    tq = qt_ref.shape[3]
    tk = kt_ref.shape[3]

    @pl.when(ki == 0)
    def _():
        m_ref[...] = jnp.full(m_ref.shape, -jnp.inf, F32)
        l_ref[...] = jnp.zeros(l_ref.shape, F32)
        acc_ref[...] = jnp.zeros(acc_ref.shape, F32)

    def block(masked):
        if masked:
            visible = (lax.broadcasted_iota(jnp.int32, (tk, tq), 0) <= lax.broadcasted_iota(jnp.int32, (tk, tq), 1))
        for hd in range(N_HEADS):
            s = lax.dot_general(kt_ref[0, hd], qt_ref[0, hd], _TN, preferred_element_type=F32)
            if masked:
                s = jnp.where(visible, s, MASK_VALUE)
            m_prev = m_ref[hd]
            m_new = jnp.maximum(m_prev, jnp.max(s, axis=0, keepdims=True))
            alpha = jnp.exp2(m_prev - m_new)
            pr = jnp.exp2(s - m_new)
            l_ref[hd] = alpha * l_ref[hd] + jnp.sum(pr, axis=0, keepdims=True)
            pv = jnp.dot(vt_ref[0, hd], pr.astype(BF16), preferred_element_type=F32)
            acc_ref[hd] = alpha * acc_ref[hd] + pv
            m_ref[hd] = m_new

    @pl.when(ki < qi)
    def _():
        block(False)

    @pl.when(ki == qi)
    def _():
        block(True)
        outs = []
        for hd in range(N_HEADS):
            o = acc_ref[hd] / l_ref[hd]
            outs.append(o * _rms_rows(o) * gout_ref[hd])
        y_ref[0] = jnp.concatenate(outs, axis=0).T.astype(BF16)


def _attn_prompt(qt, kt, vt, gout_a):
    b, _, _, s = qt.shape
    t = min(ATTN_TILE, s)
    assert s % t == 0
    n = s // t
    pairs = [(i, j) for i in range(n) for j in range(i + 1)]
    qi_tab = jnp.asarray([p[0] for p in pairs], jnp.int32)
    ki_tab = jnp.asarray([p[1] for p in pairs], jnp.int32)
    grid_spec = pltpu.PrefetchScalarGridSpec(
        num_scalar_prefetch=2,
        grid=(b, len(pairs)),
        in_specs=[
            pl.BlockSpec((1, N_HEADS, HEAD_QK, t), lambda bi, p, qi, ki: (bi, 0, 0, qi[p])),
            pl.BlockSpec((1, N_HEADS, HEAD_QK, t), lambda bi, p, qi, ki: (bi, 0, 0, ki[p])),
            pl.BlockSpec((1, N_HEADS, V_DIM, t), lambda bi, p, qi, ki: (bi, 0, 0, ki[p])),
            pl.BlockSpec(gout_a.shape, lambda bi, p, qi, ki: (0, 0, 0)),
        ],
        out_specs=pl.BlockSpec((1, t, N_HEADS * V_DIM), lambda bi, p, qi, ki: (bi, qi[p], 0)),
        scratch_shapes=[pltpu.VMEM((N_HEADS, 1, t), F32), pltpu.VMEM((N_HEADS, 1, t), F32),
                        pltpu.VMEM((N_HEADS, V_DIM, t), F32)],
    )
    return pl.pallas_call(
        _attn_prompt_kernel,
        grid_spec=grid_spec,
        out_shape=jax.ShapeDtypeStruct((b, s, N_HEADS * V_DIM), BF16),
        compiler_params=_params(("arbitrary", "arbitrary")),
        name="attn_prompt",
    )(qi_tab, ki_tab, qt, kt, vt, gout_a)


def _proj_sample_kernel(x_ref, tpos_ref, st1_ref, st2_ref, gmix_ref, win_c_ref, win_at_ref, convw_ref,
                        convb_ref, gout_c_ref, gqlat_ref, wuqt_ref, gkv_ref, gqn_ref, gqr_ref, gkn_ref,
                        gkr_ref, wuk_ref, cos_ref, sin_ref,
                        yconv_ref, u_ref, qa_ref, qr_ref, ckv_ref, krope_ref, ext_ref):
    tm = x_ref.shape[0]
    c = convw_ref.shape[1]
    xf = x_ref[...]
    h = (xf * _rms_lanes(xf) * gmix_ref[...]).astype(BF16)

    zc = jnp.dot(h, win_c_ref[...], preferred_element_type=F32)
    u = zc[:, 2 * c:] * zc[:, :c]
    ext_ref[0:SUBLANES, :] = jnp.zeros((SUBLANES, c), F32)
    ext_ref[SUBLANES:, :] = u
    tpos = tpos_ref[...]
    u_m2 = jnp.where(tpos >= 2, ext_ref[pl.ds(SUBLANES - 2, tm), :], st2_ref[...])
    u_m1 = jnp.where(tpos >= 1, ext_ref[pl.ds(SUBLANES - 1, tm), :], st1_ref[...])
    v = convb_ref[...] + convw_ref[0:1, :] * u_m2 + convw_ref[1:2, :] * u_m1 + convw_ref[2:3, :] * u
    yconv = zc[:, c:2 * c] * v
    yconv_ref[...] = _group_norm_lanes(yconv, gout_c_ref[...]).astype(BF16)
    u_ref[...] = u

    q_nope, q_rope, ckvt, (k1, k2) = _attention_side(
        h, win_at_ref, gqlat_ref, wuqt_ref, gkv_ref, gqn_ref, gqr_ref, gkr_ref, cos_ref[...], sin_ref[...])
    for hd in range(N_HEADS):
        qg = (q_nope[hd] * gkn_ref[...]).astype(BF16)
        qa_t = jnp.dot(wuk_ref[:, hd * NOPE_DIM:(hd + 1) * NOPE_DIM], qg, preferred_element_type=F32)
        qa_ref[hd] = qa_t.T
        qr_ref[hd] = _to_token_major(jnp.concatenate(q_rope[hd], axis=0), ROPE_DIM)
    ckv_ref[...] = ckvt.T
    krope_ref[...] = _to_token_major(jnp.concatenate([k1, k2], axis=0), ROPE_DIM)


def _proj_sample(x, tpos, st1, st2, w, cos_t, sin_t):
    tm, d = x.shape
    c = w["conv_w"].shape[1]
    kv_rank = w["g_kv"].shape[0]
    args = [x, tpos, st1, st2, w["g_mix"], w["win_c"], w["win_at"], w["conv_w"], w["conv_b"], w["g_out_c"],
            w["g_qlat"], w["wuq_t"], w["g_kv"], w["g_qn"], w["g_qr"], w["g_kn"], w["g_kr"], w["wuk"],
            cos_t, sin_t]
    out_shape = [
        jax.ShapeDtypeStruct((tm, c), BF16),
        jax.ShapeDtypeStruct((tm, c), F32),
        jax.ShapeDtypeStruct((N_HEADS, tm, kv_rank), F32),
        jax.ShapeDtypeStruct((N_HEADS, tm, ROPE_DIM), F32),
        jax.ShapeDtypeStruct((tm, kv_rank), F32),
        jax.ShapeDtypeStruct((tm, ROPE_DIM), F32),
    ]
    return pl.pallas_call(
        _proj_sample_kernel,
        out_shape=out_shape,
        scratch_shapes=[pltpu.VMEM((tm + SUBLANES, c), F32)],
        compiler_params=pltpu.CompilerParams(vmem_limit_bytes=VMEM_LIMIT_BYTES),
        name="proj_sample",
    )(*args)


def _attn_decode_kernel(pt_ref, ckv_hbm, kr_hbm, qa_ref, qr_ref, cnew_ref, krnew_ref, wukt_ref, wuv_ref,
                        gout_ref, y_ref, cbuf, kbuf, sem, *, layer, n_pages, page, t_dec):
    b = pl.program_id(0)
    n_seq = pl.num_programs(0)
    chunk_pages = cbuf.shape[1] // page
    n_chunks = n_pages // chunk_pages
    rows = N_HEADS * t_dec

    def copies(first_page, slot):
        out = []
        for pg in range(chunk_pages):
            pid = 0 if first_page is None else pt_ref[first_page + pg]
            out.append(pltpu.make_async_copy(ckv_hbm.at[layer, pid], cbuf.at[slot, pl.ds(pg * page, page)], sem.at[0, slot]))
            out.append(pltpu.make_async_copy(kr_hbm.at[layer, pid], kbuf.at[slot, pg], sem.at[1, slot]))
        return out

    qa = qa_ref[...].reshape(rows, qa_ref.shape[2]).astype(BF16)
    qr = qr_ref[...].reshape(rows, ROPE_DIM).astype(BF16)

    n_up = wukt_ref.shape[0]
    wq = jnp.concatenate([wukt_ref[...], qa], axis=0)

    def nope_scores(c_b):
        both = lax.dot_general(wq, c_b, _NT, preferred_element_type=F32)
        rs = []
        for hd in range(N_HEADS):
            blk = both[hd * NOPE_DIM:(hd + 1) * NOPE_DIM]
            rs.append(jnp.broadcast_to(_rms_rows(blk), (t_dec, blk.shape[1])))
        return both[n_up:] * jnp.concatenate(rs, axis=0)

    def update(carry, s, c_b):
        m_prev, l_prev, acc = carry
        m_new = jnp.maximum(m_prev, jnp.max(s, axis=-1, keepdims=True))
        alpha = jnp.exp2(m_prev - m_new)
        pr = jnp.exp2(s - m_new)
        l_new = alpha * l_prev + jnp.sum(pr, axis=-1, keepdims=True)
        acc = alpha * acc + jnp.dot(pr.astype(BF16), c_b, preferred_element_type=F32)
        return m_new, l_new, acc

    @pl.when(b == 0)
    def _():
        for cp in copies(0, 0):
            cp.start()

    last = n_seq * n_chunks - 1

    def body(j, carry):
        g = b * n_chunks + j
        slot = lax.rem(g, 2)
        for cp in copies(jnp.minimum(g + 1, last) * chunk_pages, 1 - slot):
            cp.start()
        for cp in copies(None, slot):
            cp.wait()
        c_b = cbuf[slot].astype(BF16)
        s_rope = jnp.concatenate(
            [jnp.dot(qr, kbuf[slot, pg].astype(BF16), preferred_element_type=F32) for pg in range(chunk_pages)], axis=1)
        return update(carry, nope_scores(c_b) + s_rope, c_b)

    init = (jnp.full((rows, 1), -jnp.inf, F32), jnp.zeros((rows, 1), F32), jnp.zeros((rows, cbuf.shape[2]), F32))
    carry = lax.fori_loop(0, n_chunks, body, init, unroll=2)

    @pl.when(b == n_seq - 1)
    def _():
        for cp in copies(None, (last + 1) % 2):
            cp.wait()

    pad = LANES - t_dec
    c_new = jnp.concatenate([cnew_ref[...], jnp.zeros((pad, cnew_ref.shape[1]), F32)], axis=0).astype(BF16)
    kr_new = jnp.concatenate([krnew_ref[...], jnp.zeros((pad, ROPE_DIM), F32)], axis=0).astype(BF16)
    s_new = nope_scores(c_new) + lax.dot_general(qr, kr_new, _NT, preferred_element_type=F32)
    q_t = lax.rem(lax.broadcasted_iota(jnp.int32, (rows, LANES), 0), t_dec)
    key = lax.broadcasted_iota(jnp.int32, (rows, LANES), 1)
    s_new = jnp.where(key <= q_t, s_new, MASK_VALUE)
    _, l_fin, acc = update(carry, s_new, c_new)

    o_lat = (acc / l_fin).astype(BF16)
    ov = jnp.dot(o_lat, wuv_ref[...], preferred_element_type=F32)
    outs = []
    for hd in range(N_HEADS):
        o = ov[hd * t_dec:(hd + 1) * t_dec, hd * V_DIM:(hd + 1) * V_DIM]
        outs.append(o * _rms_lanes(o) * gout_ref[hd])
    y_ref[...] = jnp.concatenate(outs, axis=1)


def _attn_decode(layer, page_table, cache_ckv, cache_kr, qa, qr, c_new, kr_new, w, t_dec):
    n_seq, n_pages = page_table.shape
    _, _, page, kv_rank = cache_ckv.shape
    chunk_pages = min(DECODE_CHUNK_PAGES, n_pages)
    assert n_pages % chunk_pages == 0 and t_dec % SUBLANES == 0 and t_dec <= LANES
    chunk = chunk_pages * page
    grid_spec = pltpu.PrefetchScalarGridSpec(
        num_scalar_prefetch=1,
        grid=(n_seq,),
        in_specs=[
            pl.BlockSpec(memory_space=pl.ANY),
            pl.BlockSpec(memory_space=pl.ANY),
            pl.BlockSpec((N_HEADS, t_dec, kv_rank), lambda b, pt: (0, b, 0)),
            pl.BlockSpec((N_HEADS, t_dec, ROPE_DIM), lambda b, pt: (0, b, 0)),
            pl.BlockSpec((t_dec, kv_rank), lambda b, pt: (b, 0)),
            pl.BlockSpec((t_dec, ROPE_DIM), lambda b, pt: (b, 0)),
            pl.BlockSpec(w["wuk_t"].shape, lambda b, pt: (0, 0)),
            pl.BlockSpec(w["wuv"].shape, lambda b, pt: (0, 0)),
            pl.BlockSpec(w["g_out_a_row"].shape, lambda b, pt: (0, 0, 0)),
        ],
        out_specs=pl.BlockSpec((t_dec, N_HEADS * V_DIM), lambda b, pt: (b, 0)),
        scratch_shapes=[pltpu.VMEM((2, chunk, kv_rank), F32), pltpu.VMEM((2, chunk_pages, ROPE_DIM, page), F32),
                        pltpu.SemaphoreType.DMA((2, 2))],
    )
    return pl.pallas_call(
        functools.partial(_attn_decode_kernel, layer=layer, n_pages=n_pages, page=page, t_dec=t_dec),
        grid_spec=grid_spec,
        out_shape=jax.ShapeDtypeStruct((n_seq * t_dec, N_HEADS * V_DIM), F32),
        compiler_params=_params(("arbitrary",)),
        name="attn_decode",
    )(page_table.reshape(-1), cache_ckv, cache_kr, qa, qr, c_new, kr_new, w["wuk_t"], w["wuv"], w["g_out_a_row"])


ROUTE_E0, ROUTE_E1, ROUTE_R0, ROUTE_R1, ROUTE_W0, ROUTE_W1 = range(6)


def _lane_pick(x, lane, idx):
    return jnp.sum(jnp.where(lane == idx, x, 0.0), axis=-1, keepdims=True)


def _store_row_tiles(ref, x):
    rows, width = x.shape
    n = width // LANES
    for j in range(n):
        ref[pl.ds(j, rows, stride=n), :] = x[:, j * LANES:(j + 1) * LANES]


def _load_row_tiles(ref, rows, n):
    return jnp.concatenate([ref[pl.ds(j, rows, stride=n), :] for j in range(n)], axis=1)


def _merge_route_kernel(ycp_ref, yap_ref, xp_ref, ycs_ref, yas_ref, xs_ref, wout_ref, gffn_ref, wr_ref, br_ref,
                        xmid_ref, h2_ref, route_ref, counts_ref, carry_ref, *, n_prompt_tiles):
    i = pl.program_id(0)
    tm = xp_ref.shape[0]
    is_p = i < n_prompt_tiles

    @pl.when(i == 0)
    def _():
        carry_ref[...] = jnp.zeros(carry_ref.shape, F32)

    yc = jnp.where(is_p, ycp_ref[...], ycs_ref[...])
    ya = jnp.where(is_p, yap_ref[...], yas_ref[...].astype(BF16))
    x = jnp.where(is_p, xp_ref[...], xs_ref[...])
    y = jnp.concatenate([yc, ya], axis=1)
    xm = x + jnp.dot(y, wout_ref[...], preferred_element_type=F32)
    xmid_ref[...] = xm
    h2f = xm * _rms_lanes(xm) * gffn_ref[...]
    _store_row_tiles(h2_ref, h2f)
    h2 = h2f.astype(BF16)
    logits = jnp.dot(h2, wr_ref[...], preferred_element_type=F32) + br_ref[...]

    lane_i = lax.broadcasted_iota(jnp.int32, (tm, LANES), 1)
    lane = lane_i.astype(F32)
    neg = -jnp.inf
    far = float(LANES)
    in_groups = lane_i < N_GROUPS
    gl = jnp.where(in_groups, logits, neg)
    ge = jnp.exp(gl - jnp.max(gl, axis=-1, keepdims=True))
    pg = ge / jnp.sum(ge, axis=-1, keepdims=True)
    p_sel = jnp.max(pg, axis=-1, keepdims=True)
    g_sel = jnp.min(jnp.where((pg == p_sel) & in_groups, lane, far), axis=-1, keepdims=True)
    lo = N_GROUPS + g_sel * EXPERTS_PER_GROUP
    el = jnp.where((lane >= lo) & (lane < lo + EXPERTS_PER_GROUP), logits, neg)
    v1 = jnp.max(el, axis=-1, keepdims=True)
    i1 = jnp.min(jnp.where(el == v1, lane, far), axis=-1, keepdims=True)
    el2 = jnp.where(lane == i1, neg, el)
    v2 = jnp.max(el2, axis=-1, keepdims=True)
    i2 = jnp.min(jnp.where(el2 == v2, lane, far), axis=-1, keepdims=True)
    e2 = jnp.exp(v2 - v1)
    w0 = 1.0 / (1.0 + e2) * p_sel
    w1 = e2 / (1.0 + e2) * p_sel
    e0 = i1 - N_GROUPS
    e1 = i2 - N_GROUPS

    oh0 = lane == e0
    oh1 = lane == e1
    onehot = (oh0 | oh1).astype(BF16)
    tri = (lax.broadcasted_iota(jnp.int32, (tm, tm), 0) > lax.broadcasted_iota(jnp.int32, (tm, tm), 1)).astype(BF16)
    before = jnp.dot(tri, onehot, preferred_element_type=F32) + carry_ref[...]
    r0 = jnp.sum(jnp.where(oh0, before, 0.0), axis=-1, keepdims=True)
    r1 = jnp.sum(jnp.where(oh1, before, 0.0), axis=-1, keepdims=True)
    carry_ref[...] += jnp.sum(onehot.astype(F32), axis=0, keepdims=True)

    route = jnp.zeros((tm, LANES), F32)
    for idx, val in ((ROUTE_E0, e0), (ROUTE_E1, e1), (ROUTE_R0, r0), (ROUTE_R1, r1), (ROUTE_W0, w0), (ROUTE_W1, w1)):
        route = jnp.where(lane_i == idx, val, route)
    route_ref[...] = route.T[:SUBLANES]

    @pl.when(i == pl.num_programs(0) - 1)
    def _():
        counts_ref[...] = carry_ref[...]


def _merge_route(ycp, yap, xp, ycs, yas, xs, w):
    tp, d = xp.shape
    ts = xs.shape[0]
    tm = TOKEN_TILE
    assert tp % tm == 0 and ts % tm == 0
    npt, nst = tp // tm, ts // tm
    half = ycp.shape[1]
    pmap = lambda i: (jnp.minimum(i, npt - 1), 0)
    smap = lambda i: (jnp.maximum(i - npt, 0), 0)
    cmap = lambda i: (0, 0)
    return pl.pallas_call(
        functools.partial(_merge_route_kernel, n_prompt_tiles=npt),
        grid=(npt + nst,),
        in_specs=[
            pl.BlockSpec((tm, half), pmap), pl.BlockSpec((tm, half), pmap), pl.BlockSpec((tm, d), pmap),
            pl.BlockSpec((tm, half), smap), pl.BlockSpec((tm, half), smap), pl.BlockSpec((tm, d), smap),
            pl.BlockSpec(w["w_out"].shape, cmap), pl.BlockSpec(w["g_ffn"].shape, cmap),
            pl.BlockSpec(w["w_r"].shape, cmap), pl.BlockSpec(w["b_r"].shape, cmap),
        ],
        out_specs=[pl.BlockSpec((tm, d), lambda i: (i, 0)),
                   pl.BlockSpec((tm * (d // LANES), LANES), lambda i: (i, 0)),
                   pl.BlockSpec((SUBLANES, tm), lambda i: (0, i)),
                   pl.BlockSpec((1, LANES), cmap)],
        out_shape=[jax.ShapeDtypeStruct((tp + ts, d), F32),
                   jax.ShapeDtypeStruct(((tp + ts) * (d // LANES), LANES), F32),
                   jax.ShapeDtypeStruct((SUBLANES, tp + ts), F32),
                   jax.ShapeDtypeStruct((1, LANES), F32)],
        scratch_shapes=[pltpu.VMEM((1, LANES), F32)],
        compiler_params=_params(("arbitrary",)),
        name="merge_route",
    )(ycp, yap, xp, ycs, yas, xs, w["w_out"], w["g_ffn"], w["w_r"], w["b_r"])


def _tile_of(ref, row, n):
    start = row * n
    return ref.at[pl.ds(start if isinstance(row, int) else pl.multiple_of(start, n), n)]


def _pad_fill_copies(base_ref, cnt_ref, tiles_ref, nu_ref, zeros_hbm, xs_hbm, sem, tme, n, n_tiles):
    out = []
    for e in range(N_EXPERTS):
        pad = tiles_ref[e] * tme - cnt_ref[e]
        pos = base_ref[e] + cnt_ref[e]
        bit = tme // 2
        while bit >= 1:
            take = pad & bit
            out.append((take != 0, pltpu.make_async_copy(
                zeros_hbm.at[pl.ds(0, bit * n)], xs_hbm.at[pl.ds(pl.multiple_of(pos * n, n), bit * n)], sem)))
            pos = pos + take
            bit //= 2
    for k in range(N_EXPERTS + 1):
        tile = nu_ref[0] + k
        out.append((tile < n_tiles, pltpu.make_async_copy(
            zeros_hbm, xs_hbm.at[pl.ds(pl.multiple_of(jnp.minimum(tile, n_tiles - 1) * (tme * n), tme * n), tme * n)],
            sem)))
    return out


def _dispatch_kernel(base_ref, cnt_ref, tiles_ref, nu_ref, s0_ref, s1_ref, h2_hbm, zeros_hbm,
                     xs_hbm, sem, *, tm, tme, n, n_tiles):
    i = pl.program_id(0)
    for t0 in range(0, tm, SLOT_BATCH):
        slots = [(s0_ref[0, 0, t], s1_ref[0, 0, t]) for t in range(t0, t0 + SLOT_BATCH)]
        for t, pair in zip(range(t0, t0 + SLOT_BATCH), slots):
            src = _tile_of(h2_hbm, i * tm + t, n)
            for s in pair:
                pltpu.make_async_copy(src, _tile_of(xs_hbm, s, n), sem.at[0]).start()

    def fills():
        return _pad_fill_copies(base_ref, cnt_ref, tiles_ref, nu_ref, zeros_hbm, xs_hbm, sem.at[1], tme, n, n_tiles)

    @pl.when(i == 0)
    def _():
        for pred, cp in fills():
            pl.when(pred)(cp.start)

    for t in range(2 * tm):
        pltpu.make_async_copy(_tile_of(h2_hbm, 0, n), _tile_of(xs_hbm, 0, n), sem.at[0]).wait()

    @pl.when(i == 0)
    def _():
        for pred, cp in fills():
            pl.when(pred)(cp.wait)


def _dispatch(plan, h2_tiles, n_slots):
    base, cnt, tiles, n_used, slot0, slot1 = plan
    n_tok_tiles, _, tm = slot0.shape
    n = h2_tiles.shape[0] // (n_tok_tiles * tm)
    tme = EXPERT_TILE
    n_tiles = n_slots // tme
    zeros = jnp.zeros((tme * n, LANES), F32)
    smem_blk = pl.BlockSpec((1, 1, tm), lambda i, *_: (i, 0, 0), memory_space=pltpu.SMEM)
    grid_spec = pltpu.PrefetchScalarGridSpec(
        num_scalar_prefetch=4,
        grid=(n_tok_tiles,),
        in_specs=[smem_blk] * 2 + [pl.BlockSpec(memory_space=pl.ANY)] * 2,
        out_specs=pl.BlockSpec(memory_space=pl.ANY),
        scratch_shapes=[pltpu.SemaphoreType.DMA((2,))],
    )
    return pl.pallas_call(
        functools.partial(_dispatch_kernel, tm=tm, tme=tme, n=n, n_tiles=n_tiles),
        grid_spec=grid_spec,
        out_shape=jax.ShapeDtypeStruct((n_slots * n, LANES), F32),
        compiler_params=_params(("arbitrary",)),
        name="dispatch",
    )(base, cnt, tiles, n_used, slot0, slot1, h2_tiles, zeros)


def _experts_kernel(te_ref, nu_ref, xs_ref, wg_ref, wu_ref, wd_ref, ys_ref, wgu_bf, wd_bf):
    i = pl.program_id(0)
    n_used = nu_ref[0]
    d_exp = wd_ref.shape[2]
    d = wd_ref.shape[3]
    n = d // LANES
    tme = xs_ref.shape[0] // n
    new_expert = (i == 0) | (te_ref[i] != te_ref[jnp.maximum(i - 1, 0)])

    @pl.when(new_expert & (i < n_used))
    def _():
        wgu_bf[:, :d_exp] = wg_ref[0, 0].astype(BF16)
        wgu_bf[:, d_exp:] = wu_ref[0, 0].astype(BF16)
        wd_bf[...] = wd_ref[0, 0].astype(BF16)

    @pl.when(i < n_used)
    def _():
        h2 = _load_row_tiles(xs_ref, tme, n).astype(BF16)
        gu = jnp.dot(h2, wgu_bf[...], preferred_element_type=F32)
        g = gu[:, :d_exp]
        a = (g / (1.0 + jnp.exp(-g))) * gu[:, d_exp:]
        _store_row_tiles(ys_ref, jnp.dot(a.astype(BF16), wd_bf[...], preferred_element_type=F32))

    @pl.when(i >= n_used)
    def _():
        ys_ref[...] = jnp.zeros(ys_ref.shape, F32)


def _experts(layer, tile_expert, n_used, xs_tiles, w_gate, w_up, w_down):
    n_tiles = tile_expert.shape[0]
    _, _, d, d_exp = w_gate.shape
    blk = xs_tiles.shape[0] // n_tiles
    grid_spec = pltpu.PrefetchScalarGridSpec(
        num_scalar_prefetch=2,
        grid=(n_tiles,),
        in_specs=[
            pl.BlockSpec((blk, LANES), lambda i, te, nu: (jnp.minimum(i, nu[0] - 1), 0)),
            pl.BlockSpec((1, 1, d, d_exp), lambda i, te, nu: (layer, te[i], 0, 0)),
            pl.BlockSpec((1, 1, d, d_exp), lambda i, te, nu: (layer, te[i], 0, 0)),
            pl.BlockSpec((1, 1, d_exp, d), lambda i, te, nu: (layer, te[i], 0, 0)),
        ],
        out_specs=pl.BlockSpec((blk, LANES), lambda i, te, nu: (i, 0)),
        scratch_shapes=[pltpu.VMEM((d, 2 * d_exp), BF16), pltpu.VMEM((d_exp, d), BF16)],
    )
    return pl.pallas_call(
        _experts_kernel,
        grid_spec=grid_spec,
        out_shape=jax.ShapeDtypeStruct(xs_tiles.shape, F32),
        compiler_params=_params(("arbitrary",)),
        name="experts",
    )(tile_expert, n_used, xs_tiles, w_gate, w_up, w_down)


def _combine_kernel(s0_ref, s1_ref, s0n_ref, s1n_ref, ys_hbm, xmid_ref, route_ref, yp_ref, ysmp_ref, buf, sem,
                    *, n_prompt_tiles):
    i = pl.program_id(0)
    last = pl.num_programs(0) - 1
    tm, d = xmid_ref.shape
    n = d // LANES
    slot = lax.rem(i, 2)

    def gather(refs, dst_slot):
        s0, s1 = refs
        for t0 in range(0, tm, SLOT_BATCH):
            slots = [(s0[0, 0, t], s1[0, 0, t]) for t in range(t0, t0 + SLOT_BATCH)]
            for t, pair in zip(range(t0, t0 + SLOT_BATCH), slots):
                for k, s in enumerate(pair):
                    pltpu.make_async_copy(_tile_of(ys_hbm, s, n), _tile_of(buf.at[dst_slot, k], t, n),
                                          sem.at[dst_slot]).start(priority=k)

    def gather_wait(dst_slot):
        for t in range(2 * tm):
            pltpu.make_async_copy(_tile_of(ys_hbm, 0, n), _tile_of(buf.at[dst_slot, 0], 0, n), sem.at[dst_slot]).wait()

    @pl.when(i == 0)
    def _():
        gather((s0_ref, s1_ref), 0)

    gather((s0n_ref, s1n_ref), 1 - slot)
    gather_wait(slot)

    rt = route_ref[...]
    cols = jnp.concatenate([rt, jnp.zeros((LANES - rt.shape[0], tm), F32)], axis=0).T
    lane = lax.broadcasted_iota(jnp.int32, cols.shape, 1)
    w0 = _lane_pick(cols, lane, ROUTE_W0)
    w1 = _lane_pick(cols, lane, ROUTE_W1)
    y0 = _load_row_tiles(buf.at[slot, 0], tm, n)
    y1 = _load_row_tiles(buf.at[slot, 1], tm, n)
    out = xmid_ref[...] + (w0 * y0 + w1 * y1)

    @pl.when(i < n_prompt_tiles)
    def _():
        yp_ref[...] = out

    @pl.when(i >= n_prompt_tiles)
    def _():
        ysmp_ref[...] = out

    @pl.when(i == last)
    def _():
        gather_wait(1 - slot)


def _combine(plan, ys_tiles, xmid, route_t, tp):
    slot0, slot1 = plan[4:]
    ttot, d = xmid.shape
    n_tok_tiles, _, tm = slot0.shape
    npt = tp // tm
    n = d // LANES
    cur = pl.BlockSpec((1, 1, tm), lambda i: (i, 0, 0), memory_space=pltpu.SMEM)
    nxt = pl.BlockSpec((1, 1, tm), lambda i: (jnp.minimum(i + 1, n_tok_tiles - 1), 0, 0), memory_space=pltpu.SMEM)
    return pl.pallas_call(
        functools.partial(_combine_kernel, n_prompt_tiles=npt),
        grid=(n_tok_tiles,),
        in_specs=[cur, cur, nxt, nxt,
                  pl.BlockSpec(memory_space=pl.ANY),
                  pl.BlockSpec((tm, d), lambda i: (i, 0)),
                  pl.BlockSpec((SUBLANES, tm), lambda i: (0, i))],
        out_specs=[pl.BlockSpec((tm, d), lambda i: (jnp.minimum(i, npt - 1), 0)),
                   pl.BlockSpec((tm, d), lambda i: (jnp.maximum(i - npt, 0), 0))],
        out_shape=[jax.ShapeDtypeStruct((tp, d), F32), jax.ShapeDtypeStruct((ttot - tp, d), F32)],
        scratch_shapes=[pltpu.VMEM((2, 2, tm * n, LANES), F32), pltpu.SemaphoreType.DMA((2,))],
        compiler_params=_params(("arbitrary",)),
        name="combine",
    )(slot0, slot1, slot0, slot1, ys_tiles, xmid, route_t)


def _rope_tables(pos):
    inv_freq = ROPE_BASE ** (-jnp.arange(HALF_ROPE, dtype=F32) / HALF_ROPE)
    ang = pos.astype(F32)[:, None] * inv_freq[None, :]
    return jnp.cos(ang).T, jnp.sin(ang).T


def _layer_weights(l, g_mix, w_in, conv_w, conv_b, g_q_lat, w_uq, g_kv_lat, w_uk, w_uv, g_q_nope, g_q_rope,
                   g_k_nope, g_k_rope, g_out, w_out, g_ffn, w_router_group, b_router_group, w_router_expert,
                   b_router_expert, w_gate, w_up, w_down):
    c = conv_w.shape[2]
    col = lambda g: g[l].reshape(-1, 1)
    w_r = jnp.concatenate([w_router_group[l], w_router_expert[l]], axis=1)
    b_r = jnp.concatenate([b_router_group[l], b_router_expert[l]])
    return {
        "g_mix": g_mix[l].reshape(1, -1),
        "win_c": w_in[l][:, :3 * c].astype(BF16),
        "win_at": w_in[l][:, 3 * c:].T.astype(BF16),
        "conv_w": conv_w[l],
        "conv_b": conv_b[l].reshape(1, -1),
        "g_out_c": g_out[l][:c].reshape(1, -1),
        "g_out_a": g_out[l][c:].reshape(N_HEADS, V_DIM, 1),
        "g_out_a_row": g_out[l][c:].reshape(N_HEADS, 1, V_DIM),
        "g_qlat": col(g_q_lat), "g_kv": col(g_kv_lat), "g_qn": col(g_q_nope), "g_qr": col(g_q_rope),
        "g_kn": col(g_k_nope), "g_kr": col(g_k_rope),
        "wuq_t": w_uq[l].T.astype(BF16),
        "wuk_t": w_uk[l].T.astype(BF16),
        "wuk": w_uk[l].astype(BF16),
        "wuv_t": w_uv[l].T.astype(BF16),
        "wuv": w_uv[l].astype(BF16),
        "w_out": w_out[l].astype(BF16),
        "g_ffn": g_ffn[l].reshape(1, -1),
        "w_r": jnp.pad(w_r, ((0, 0), (0, LANES - w_r.shape[1]))).astype(BF16),
        "b_r": jnp.pad(b_r, (0, LANES - b_r.shape[0])).reshape(1, -1),
    }


def _moe_plan(route_t, counts):
    ttot = route_t.shape[1]
    tme = EXPERT_TILE
    tm = TOKEN_TILE
    n_tiles = (TOP_K * ttot) // tme + N_EXPERTS + 1
    cnt = counts[0, :N_EXPERTS].astype(jnp.int32)
    tiles = (cnt + tme - 1) // tme
    tile_end = jnp.cumsum(tiles)
    base = (tile_end - tiles) * tme
    n_used = tile_end[-1:]
    tile_id = jnp.minimum(jnp.arange(n_tiles, dtype=jnp.int32), n_used[0] - 1)
    tile_expert = jnp.sum((tile_end[None, :] <= tile_id[:, None]).astype(jnp.int32), axis=1)
    ids = route_t[:ROUTE_R1 + 1].astype(jnp.int32)
    expert_ids = jnp.arange(N_EXPERTS, dtype=jnp.int32)[:, None]

    def slots(e, r):
        s = r + jnp.sum(jnp.where(e[None, :] == expert_ids, base[:, None], 0), axis=0)
        return s.reshape(ttot // tm, 1, tm)

    plan = (base, cnt, tiles, n_used, slots(ids[ROUTE_E0], ids[ROUTE_R0]), slots(ids[ROUTE_E1], ids[ROUTE_R1]))
    return plan, tile_expert, n_tiles * tme


def kernel(x_prompt, x_sample, state_conv, cache_ckv, cache_krope, page_table, g_mix, w_in, conv_w, conv_b, g_q_lat, w_uq, g_kv_lat, w_uk, w_uv, g_q_nope, g_q_rope, g_k_nope, g_k_rope, g_out, w_out, g_ffn, w_router_group, b_router_group, w_router_expert, b_router_expert, w_gate, w_up, w_down):
    b_p, s_p, d = x_prompt.shape
    b_s, t_s, _ = x_sample.shape
    depth = g_mix.shape[0]
    c = conv_w.shape[2]
    page = cache_ckv.shape[2]
    past_len = page_table.shape[1] * page
    kv_rank = cache_ckv.shape[3]

    cos_p, sin_p = _rope_tables(jnp.arange(s_p, dtype=jnp.int32))
    cos_s, sin_s = _rope_tables(jnp.tile(past_len + jnp.arange(t_s, dtype=jnp.int32), b_s))
    tpos = jnp.tile(jnp.arange(t_s, dtype=jnp.int32), b_s).reshape(-1, 1)

    krope_pages = jnp.swapaxes(cache_krope, 2, 3)

    xp, xs = x_prompt, x_sample.reshape(b_s * t_s, d)
    outs = [[] for _ in range(6)]
    for l in range(depth):
        w = _layer_weights(l, g_mix, w_in, conv_w, conv_b, g_q_lat, w_uq, g_kv_lat, w_uk, w_uv, g_q_nope,
                           g_q_rope, g_k_nope, g_k_rope, g_out, w_out, g_ffn, w_router_group, b_router_group,
                           w_router_expert, b_router_expert, w_gate, w_up, w_down)
        yconv_p, qt, kt, vt, ckv_p, kr_p, conv_p = _proj_prompt(xp, w, cos_p, sin_p)
        yattn_p = _attn_prompt(qt, kt, vt, w["g_out_a"])
        st = state_conv[l]
        zeros = lambda n: jnp.zeros((b_s, n, c), F32)
        st1 = jnp.concatenate([st[:, CONV_K - 2:], zeros(t_s - 1)], axis=1).reshape(b_s * t_s, c)
        st2 = jnp.concatenate([st, zeros(t_s - (CONV_K - 1))], axis=1).reshape(b_s * t_s, c)
        yconv_s, u_s, qa, qr, ckv_s, kr_s = _proj_sample(xs, tpos, st1, st2, w, cos_s, sin_s)
        yattn_s = _attn_decode(l, page_table, cache_ckv, krope_pages, qa, qr, ckv_s, kr_s, w, t_s)
        xmid, h2_tiles, route_t, counts = _merge_route(
            yconv_p.reshape(b_p * s_p, c), yattn_p.reshape(b_p * s_p, -1), xp.reshape(b_p * s_p, d),
            yconv_s, yattn_s, xs, w)
        plan, tile_expert, n_slots = _moe_plan(route_t, counts)
        xs_tiles = _dispatch(plan, h2_tiles, n_slots)
        ys_tiles = _experts(l, tile_expert, plan[3], xs_tiles, w_gate, w_up, w_down)
        yp, ysmp = _combine(plan, ys_tiles, xmid, route_t, b_p * s_p)
        xp, xs = yp.reshape(b_p, s_p, d), ysmp
        for lst, val in zip(outs, (ckv_p, kr_p, conv_p, ckv_s.reshape(b_s, t_s, kv_rank),
                                   kr_s.reshape(b_s, t_s, ROPE_DIM),
                                   u_s.reshape(b_s, t_s, c)[:, t_s - (CONV_K - 1):])):
            lst.append(val)
    return (xp, xs.reshape(b_s, t_s, d)) + tuple(jnp.stack(o) for o in outs)
```

```python
import functools

import jax
import jax.numpy as jnp
from jax import lax
from jax.experimental import pallas as pl
from jax.experimental.pallas import tpu as pltpu

N_HEADS = 8
NOPE_DIM = 64
ROPE_DIM = 32
V_DIM = 64
HEAD_QK = NOPE_DIM + ROPE_DIM
HALF_ROPE = ROPE_DIM // 2
ROPE_BASE = 10000.0
CONV_K = 3
OUT_GROUP_DIM = 64
N_GROUPS = 4
EXPERTS_PER_GROUP = 8
N_EXPERTS = N_GROUPS * EXPERTS_PER_GROUP
TOP_K = 2
EPS = 1e-6
MASK_VALUE = -1e30
LOG2_E = 1.4426950408889634

LANES = 128
SUBLANES = 8
VMEM_LIMIT_BYTES = 48 * 1024 * 1024

PROJ_TILE = 512
ATTN_TILE = 512
TOKEN_TILE = 256
EXPERT_TILE = 256
DECODE_CHUNK_PAGES = 8
SLOT_BATCH = 8

F32 = jnp.float32
BF16 = jnp.bfloat16

_NT = (((1,), (1,)), ((), ()))
_TN = (((0,), (0,)), ((), ()))


def _params(sem):
    return pltpu.CompilerParams(dimension_semantics=sem, vmem_limit_bytes=VMEM_LIMIT_BYTES)


def _rms_rows(x):
    return lax.rsqrt(jnp.mean(x * x, axis=0, keepdims=True) + EPS)


def _rms_lanes(x):
    return lax.rsqrt(jnp.mean(x * x, axis=-1, keepdims=True) + EPS)


def _group_norm_lanes(y, gain):
    lane = lax.broadcasted_iota(jnp.int32, (1, LANES), 1)
    low = lane < OUT_GROUP_DIM
    outs = []
    for j in range(y.shape[1] // LANES):
        t = y[:, j * LANES:(j + 1) * LANES]
        sq = t * t
        ss_lo = jnp.sum(jnp.where(low, sq, 0.0), axis=-1, keepdims=True)
        ss_hi = jnp.sum(jnp.where(low, 0.0, sq), axis=-1, keepdims=True)
        r = jnp.where(low, lax.rsqrt(ss_lo / OUT_GROUP_DIM + EPS), lax.rsqrt(ss_hi / OUT_GROUP_DIM + EPS))
        outs.append(t * r)
    return jnp.concatenate(outs, axis=1) * gain


def _rope_rows(x, cos, sin):
    x1, x2 = x[:HALF_ROPE], x[HALF_ROPE:]
    return x1 * cos - x2 * sin, x1 * sin + x2 * cos


def _attention_side(h_bf16, win_at_ref, gqlat_ref, wuqt_ref, gkv_ref, gqn_ref, gqr_ref, gkr_ref, cos, sin):
    q_rank = gqlat_ref.shape[0]
    kv_rank = gkv_ref.shape[0]
    scale = HEAD_QK ** -0.5 * LOG2_E
    zat = lax.dot_general(win_at_ref[...], h_bf16, _NT, preferred_element_type=F32)
    qlt = zat[:q_rank]
    kvt = zat[q_rank:q_rank + kv_rank]
    krt = zat[q_rank + kv_rank:]
    qln = (qlt * _rms_rows(qlt) * gqlat_ref[...]).astype(BF16)
    qt = jnp.dot(wuqt_ref[...], qln, preferred_element_type=F32)
    q_nope, q_rope = [], []
    for h in range(N_HEADS):
        nope = qt[h * HEAD_QK:h * HEAD_QK + NOPE_DIM]
        rope = qt[h * HEAD_QK + NOPE_DIM:(h + 1) * HEAD_QK]
        q_nope.append(nope * _rms_rows(nope) * gqn_ref[...] * scale)
        r1, r2 = _rope_rows(rope * _rms_rows(rope) * gqr_ref[...], cos, sin)
        q_rope.append((r1 * scale, r2 * scale))
    ckvt = kvt * _rms_rows(kvt) * gkv_ref[...]
    k1, k2 = _rope_rows(krt * _rms_rows(krt) * gkr_ref[...], cos, sin)
    return q_nope, q_rope, ckvt, (k1, k2)


def _to_token_major(xt, width):
    rows, toks = xt.shape
    if rows < LANES:
        xt = jnp.concatenate([xt, jnp.zeros((LANES - rows, toks), F32)], axis=0)
    return xt.T[:, :width]


def _proj_prompt_kernel(x_ref, gmix_ref, win_c_ref, win_at_ref, convw_ref, convb_ref, gout_c_ref,
                        gqlat_ref, wuqt_ref, gkv_ref, gqn_ref, gqr_ref, gkn_ref, gkr_ref,
                        wukt_ref, wuvt_ref, cos_ref, sin_ref,
                        yconv_ref, qt_ref, kt_ref, vt_ref, ckv_ref, krope_ref, convst_ref,
                        ext_ref):
    si = pl.program_id(1)
    tm = x_ref.shape[1]
    c = convw_ref.shape[1]

    xf = x_ref[0]
    h = (xf * _rms_lanes(xf) * gmix_ref[...]).astype(BF16)

    zc = jnp.dot(h, win_c_ref[...], preferred_element_type=F32)
    u = zc[:, 2 * c:] * zc[:, :c]

    @pl.when(si == 0)
    def _():
        ext_ref[0:SUBLANES, :] = jnp.zeros((SUBLANES, c), F32)

    ext_ref[SUBLANES:, :] = u
    v = (convb_ref[...]
         + convw_ref[0:1, :] * ext_ref[pl.ds(SUBLANES - 2, tm), :]
         + convw_ref[1:2, :] * ext_ref[pl.ds(SUBLANES - 1, tm), :]
         + convw_ref[2:3, :] * u)
    yconv = zc[:, c:2 * c] * v
    yconv_ref[0] = _group_norm_lanes(yconv, gout_c_ref[...]).astype(BF16)
    ext_ref[0:SUBLANES, :] = ext_ref[pl.ds(tm, SUBLANES), :]
    convst_ref[0] = ext_ref[pl.ds(SUBLANES - (CONV_K - 1), CONV_K - 1), :]

    q_nope, q_rope, ckvt, (k1, k2) = _attention_side(
        h, win_at_ref, gqlat_ref, wuqt_ref, gkv_ref, gqn_ref, gqr_ref, gkr_ref, cos_ref[...], sin_ref[...])
    for hd in range(N_HEADS):
        qt_ref[0, hd, 0:NOPE_DIM, :] = q_nope[hd].astype(BF16)
        qt_ref[0, hd, NOPE_DIM:NOPE_DIM + HALF_ROPE, :] = q_rope[hd][0].astype(BF16)
        qt_ref[0, hd, NOPE_DIM + HALF_ROPE:HEAD_QK, :] = q_rope[hd][1].astype(BF16)
    ckv_ref[0] = ckvt.T
    krt = jnp.concatenate([k1, k2], axis=0)
    krope_ref[0] = _to_token_major(krt, ROPE_DIM)
    ckv_b = ckvt.astype(BF16)
    ktn = jnp.dot(wukt_ref[...], ckv_b, preferred_element_type=F32)
    vt = jnp.dot(wuvt_ref[...], ckv_b, preferred_element_type=F32)
    krt_b = krt.astype(BF16)
    for hd in range(N_HEADS):
        blk = ktn[hd * NOPE_DIM:(hd + 1) * NOPE_DIM]
        kt_ref[0, hd, 0:NOPE_DIM, :] = (blk * _rms_rows(blk) * gkn_ref[...]).astype(BF16)
        kt_ref[0, hd, NOPE_DIM:HEAD_QK, :] = krt_b
        vt_ref[0, hd] = vt[hd * V_DIM:(hd + 1) * V_DIM].astype(BF16)


def _proj_prompt(x, w, cos_t, sin_t):
    b, s, d = x.shape
    tm = min(PROJ_TILE, s)
    assert s % tm == 0
    c = w["conv_w"].shape[1]
    kv_rank = w["g_kv"].shape[0]
    full = lambda a: pl.BlockSpec(a.shape, lambda bi, si: (0,) * a.ndim)
    weights = [w["g_mix"], w["win_c"], w["win_at"], w["conv_w"], w["conv_b"], w["g_out_c"],
               w["g_qlat"], w["wuq_t"], w["g_kv"], w["g_qn"], w["g_qr"], w["g_kn"], w["g_kr"],
               w["wuk_t"], w["wuv_t"]]
    in_specs = ([pl.BlockSpec((1, tm, d), lambda bi, si: (bi, si, 0))] + [full(a) for a in weights]
                + [pl.BlockSpec((HALF_ROPE, tm), lambda bi, si: (0, si))] * 2)
    out_shape = [
        jax.ShapeDtypeStruct((b, s, c), BF16),
        jax.ShapeDtypeStruct((b, N_HEADS, HEAD_QK, s), BF16),
        jax.ShapeDtypeStruct((b, N_HEADS, HEAD_QK, s), BF16),
        jax.ShapeDtypeStruct((b, N_HEADS, V_DIM, s), BF16),
        jax.ShapeDtypeStruct((b, s, kv_rank), F32),
        jax.ShapeDtypeStruct((b, s, ROPE_DIM), F32),
        jax.ShapeDtypeStruct((b, CONV_K - 1, c), F32),
    ]
    out_specs = [
        pl.BlockSpec((1, tm, c), lambda bi, si: (bi, si, 0)),
        pl.BlockSpec((1, N_HEADS, HEAD_QK, tm), lambda bi, si: (bi, 0, 0, si)),
        pl.BlockSpec((1, N_HEADS, HEAD_QK, tm), lambda bi, si: (bi, 0, 0, si)),
        pl.BlockSpec((1, N_HEADS, V_DIM, tm), lambda bi, si: (bi, 0, 0, si)),
        pl.BlockSpec((1, tm, kv_rank), lambda bi, si: (bi, si, 0)),
        pl.BlockSpec((1, tm, ROPE_DIM), lambda bi, si: (bi, si, 0)),
        pl.BlockSpec((1, CONV_K - 1, c), lambda bi, si: (bi, 0, 0)),
    ]
    return pl.pallas_call(
        _proj_prompt_kernel,
        grid=(b, s // tm),
        in_specs=in_specs,
        out_specs=out_specs,
        out_shape=out_shape,
        scratch_shapes=[pltpu.VMEM((tm + SUBLANES, c), F32)],
        compiler_params=_params(("arbitrary", "arbitrary")),
        name="proj_prompt",
    )(x, *weights, cos_t, sin_t)


def _attn_prompt_kernel(qi_ref, ki_ref, qt_ref, kt_ref, vt_ref, gout_ref, y_ref, m_ref, l_ref, acc_ref):
    p = pl.program_id(1)
    qi = qi_ref[p]
    ki = ki_ref[p]
    tq = qt_ref.shape[3]
    tk = kt_ref.shape[3]

    @pl.when(ki == 0)
    def _():
        m_ref[...] = jnp.full(m_ref.shape, -jnp.inf, F32)
        l_ref[...] = jnp.zeros(l_ref.shape, F32)
        acc_ref[...] = jnp.zeros(acc_ref.shape, F32)

    def block(masked):
        if masked:
            visible = (lax.broadcasted_iota(jnp.int32, (tk, tq), 0) <= lax.broadcasted_iota(jnp.int32, (tk, tq), 1))
        for hd in range(N_HEADS):
            s = lax.dot_general(kt_ref[0, hd], qt_ref[0, hd], _TN, preferred_element_type=F32)
            if masked:
                s = jnp.where(visible, s, MASK_VALUE)
            m_prev = m_ref[hd]
            m_new = jnp.maximum(m_prev, jnp.max(s, axis=0, keepdims=True))
            alpha = jnp.exp2(m_prev - m_new)
            pr = jnp.exp2(s - m_new)
            l_ref[hd] = alpha * l_ref[hd] + jnp.sum(pr, axis=0, keepdims=True)
            pv = jnp.dot(vt_ref[0, hd], pr.astype(BF16), preferred_element_type=F32)
            acc_ref[hd] = alpha * acc_ref[hd] + pv
            m_ref[hd] = m_new

    @pl.when(ki < qi)
    def _():
        block(False)

    @pl.when(ki == qi)
    def _():
        block(True)
        outs = []
        for hd in range(N_HEADS):
            o = acc_ref[hd] / l_ref[hd]
            outs.append(o * _rms_rows(o) * gout_ref[hd])
        y_ref[0] = jnp.concatenate(outs, axis=0).T.astype(BF16)


def _attn_prompt(qt, kt, vt, gout_a):
    b, _, _, s = qt.shape
    t = min(ATTN_TILE, s)
    assert s % t == 0
    n = s // t
    pairs = [(i, j) for i in range(n) for j in range(i + 1)]
    qi_tab = jnp.asarray([p[0] for p in pairs], jnp.int32)
    ki_tab = jnp.asarray([p[1] for p in pairs], jnp.int32)
    grid_spec = pltpu.PrefetchScalarGridSpec(
        num_scalar_prefetch=2,
        grid=(b, len(pairs)),
        in_specs=[
            pl.BlockSpec((1, N_HEADS, HEAD_QK, t), lambda bi, p, qi, ki: (bi, 0, 0, qi[p])),
            pl.BlockSpec((1, N_HEADS, HEAD_QK, t), lambda bi, p, qi, ki: (bi, 0, 0, ki[p])),
            pl.BlockSpec((1, N_HEADS, V_DIM, t), lambda bi, p, qi, ki: (bi, 0, 0, ki[p])),
            pl.BlockSpec(gout_a.shape, lambda bi, p, qi, ki: (0, 0, 0)),
        ],
        out_specs=pl.BlockSpec((1, t, N_HEADS * V_DIM), lambda bi, p, qi, ki: (bi, qi[p], 0)),
        scratch_shapes=[pltpu.VMEM((N_HEADS, 1, t), F32), pltpu.VMEM((N_HEADS, 1, t), F32),
                        pltpu.VMEM((N_HEADS, V_DIM, t), F32)],
    )
    return pl.pallas_call(
        _attn_prompt_kernel,
        grid_spec=grid_spec,
        out_shape=jax.ShapeDtypeStruct((b, s, N_HEADS * V_DIM), BF16),
        compiler_params=_params(("arbitrary", "arbitrary")),
        name="attn_prompt",
    )(qi_tab, ki_tab, qt, kt, vt, gout_a)


def _proj_sample_kernel(x_ref, tpos_ref, st1_ref, st2_ref, gmix_ref, win_c_ref, win_at_ref, convw_ref,
                        convb_ref, gout_c_ref, gqlat_ref, wuqt_ref, gkv_ref, gqn_ref, gqr_ref, gkn_ref,
                        gkr_ref, wuk_ref, cos_ref, sin_ref,
                        yconv_ref, u_ref, qa_ref, qr_ref, ckv_ref, krope_ref, ext_ref):
    tm = x_ref.shape[0]
    c = convw_ref.shape[1]
    xf = x_ref[...]
    h = (xf * _rms_lanes(xf) * gmix_ref[...]).astype(BF16)

    zc = jnp.dot(h, win_c_ref[...], preferred_element_type=F32)
    u = zc[:, 2 * c:] * zc[:, :c]
    ext_ref[0:SUBLANES, :] = jnp.zeros((SUBLANES, c), F32)
    ext_ref[SUBLANES:, :] = u
    tpos = tpos_ref[...]
    u_m2 = jnp.where(tpos >= 2, ext_ref[pl.ds(SUBLANES - 2, tm), :], st2_ref[...])
    u_m1 = jnp.where(tpos >= 1, ext_ref[pl.ds(SUBLANES - 1, tm), :], st1_ref[...])
    v = convb_ref[...] + convw_ref[0:1, :] * u_m2 + convw_ref[1:2, :] * u_m1 + convw_ref[2:3, :] * u
    yconv = zc[:, c:2 * c] * v
    yconv_ref[...] = _group_norm_lanes(yconv, gout_c_ref[...]).astype(BF16)
    u_ref[...] = u

    q_nope, q_rope, ckvt, (k1, k2) = _attention_side(
        h, win_at_ref, gqlat_ref, wuqt_ref, gkv_ref, gqn_ref, gqr_ref, gkr_ref, cos_ref[...], sin_ref[...])
    for hd in range(N_HEADS):
        qg = (q_nope[hd] * gkn_ref[...]).astype(BF16)
        qa_t = jnp.dot(wuk_ref[:, hd * NOPE_DIM:(hd + 1) * NOPE_DIM], qg, preferred_element_type=F32)
        qa_ref[hd] = qa_t.T
        qr_ref[hd] = _to_token_major(jnp.concatenate(q_rope[hd], axis=0), ROPE_DIM)
    ckv_ref[...] = ckvt.T
    krope_ref[...] = _to_token_major(jnp.concatenate([k1, k2], axis=0), ROPE_DIM)


def _proj_sample(x, tpos, st1, st2, w, cos_t, sin_t):
    tm, d = x.shape
    c = w["conv_w"].shape[1]
    kv_rank = w["g_kv"].shape[0]
    args = [x, tpos, st1, st2, w["g_mix"], w["win_c"], w["win_at"], w["conv_w"], w["conv_b"], w["g_out_c"],
            w["g_qlat"], w["wuq_t"], w["g_kv"], w["g_qn"], w["g_qr"], w["g_kn"], w["g_kr"], w["wuk"],
            cos_t, sin_t]
    out_shape = [
        jax.ShapeDtypeStruct((tm, c), BF16),
        jax.ShapeDtypeStruct((tm, c), F32),
        jax.ShapeDtypeStruct((N_HEADS, tm, kv_rank), F32),
        jax.ShapeDtypeStruct((N_HEADS, tm, ROPE_DIM), F32),
        jax.ShapeDtypeStruct((tm, kv_rank), F32),
        jax.ShapeDtypeStruct((tm, ROPE_DIM), F32),
    ]
    return pl.pallas_call(
        _proj_sample_kernel,
        out_shape=out_shape,
        scratch_shapes=[pltpu.VMEM((tm + SUBLANES, c), F32)],
        compiler_params=pltpu.CompilerParams(vmem_limit_bytes=VMEM_LIMIT_BYTES),
        name="proj_sample",
    )(*args)


def _attn_decode_kernel(pt_ref, ckv_hbm, kr_hbm, qa_ref, qr_ref, cnew_ref, krnew_ref, wukt_ref, wuv_ref,
                        gout_ref, y_ref, cbuf, kbuf, sem, *, layer, n_pages, page, t_dec):
    b = pl.program_id(0)
    n_seq = pl.num_programs(0)
    chunk_pages = cbuf.shape[1] // page
    n_chunks = n_pages // chunk_pages
    rows = N_HEADS * t_dec

    def copies(first_page, slot):
        out = []
        for pg in range(chunk_pages):
            pid = 0 if first_page is None else pt_ref[first_page + pg]
            out.append(pltpu.make_async_copy(ckv_hbm.at[layer, pid], cbuf.at[slot, pl.ds(pg * page, page)], sem.at[0, slot]))
            out.append(pltpu.make_async_copy(kr_hbm.at[layer, pid], kbuf.at[slot, pg], sem.at[1, slot]))
        return out

    qa = qa_ref[...].reshape(rows, qa_ref.shape[2]).astype(BF16)
    qr = qr_ref[...].reshape(rows, ROPE_DIM).astype(BF16)

    n_up = wukt_ref.shape[0]
    wq = jnp.concatenate([wukt_ref[...], qa], axis=0)

    def nope_scores(c_b):
        both = lax.dot_general(wq, c_b, _NT, preferred_element_type=F32)
        rs = []
        for hd in range(N_HEADS):
            blk = both[hd * NOPE_DIM:(hd + 1) * NOPE_DIM]
            rs.append(jnp.broadcast_to(_rms_rows(blk), (t_dec, blk.shape[1])))
        return both[n_up:] * jnp.concatenate(rs, axis=0)

    def update(carry, s, c_b):
        m_prev, l_prev, acc = carry
        m_new = jnp.maximum(m_prev, jnp.max(s, axis=-1, keepdims=True))
        alpha = jnp.exp2(m_prev - m_new)
        pr = jnp.exp2(s - m_new)
        l_new = alpha * l_prev + jnp.sum(pr, axis=-1, keepdims=True)
        acc = alpha * acc + jnp.dot(pr.astype(BF16), c_b, preferred_element_type=F32)
        return m_new, l_new, acc

    @pl.when(b == 0)
    def _():
        for cp in copies(0, 0):
            cp.start()

    last = n_seq * n_chunks - 1

    def body(j, carry):
        g = b * n_chunks + j
        slot = lax.rem(g, 2)
        for cp in copies(jnp.minimum(g + 1, last) * chunk_pages, 1 - slot):
            cp.start()
        for cp in copies(None, slot):
            cp.wait()
        c_b = cbuf[slot].astype(BF16)
        s_rope = jnp.concatenate(
            [jnp.dot(qr, kbuf[slot, pg].astype(BF16), preferred_element_type=F32) for pg in range(chunk_pages)], axis=1)
        return update(carry, nope_scores(c_b) + s_rope, c_b)

    init = (jnp.full((rows, 1), -jnp.inf, F32), jnp.zeros((rows, 1), F32), jnp.zeros((rows, cbuf.shape[2]), F32))
    carry = lax.fori_loop(0, n_chunks, body, init, unroll=2)

    @pl.when(b == n_seq - 1)
    def _():
        for cp in copies(None, (last + 1) % 2):
            cp.wait()

    pad = LANES - t_dec
    c_new = jnp.concatenate([cnew_ref[...], jnp.zeros((pad, cnew_ref.shape[1]), F32)], axis=0).astype(BF16)
    kr_new = jnp.concatenate([krnew_ref[...], jnp.zeros((pad, ROPE_DIM), F32)], axis=0).astype(BF16)
    s_new = nope_scores(c_new) + lax.dot_general(qr, kr_new, _NT, preferred_element_type=F32)
    q_t = lax.rem(lax.broadcasted_iota(jnp.int32, (rows, LANES), 0), t_dec)
    key = lax.broadcasted_iota(jnp.int32, (rows, LANES), 1)
    s_new = jnp.where(key <= q_t, s_new, MASK_VALUE)
    _, l_fin, acc = update(carry, s_new, c_new)

    o_lat = (acc / l_fin).astype(BF16)
    ov = jnp.dot(o_lat, wuv_ref[...], preferred_element_type=F32)
    outs = []
    for hd in range(N_HEADS):
        o = ov[hd * t_dec:(hd + 1) * t_dec, hd * V_DIM:(hd + 1) * V_DIM]
        outs.append(o * _rms_lanes(o) * gout_ref[hd])
    y_ref[...] = jnp.concatenate(outs, axis=1)


def _attn_decode(layer, page_table, cache_ckv, cache_kr, qa, qr, c_new, kr_new, w, t_dec):
    n_seq, n_pages = page_table.shape
    _, _, page, kv_rank = cache_ckv.shape
    chunk_pages = min(DECODE_CHUNK_PAGES, n_pages)
    assert n_pages % chunk_pages == 0 and t_dec % SUBLANES == 0 and t_dec <= LANES
    chunk = chunk_pages * page
    grid_spec = pltpu.PrefetchScalarGridSpec(
        num_scalar_prefetch=1,
        grid=(n_seq,),
        in_specs=[
            pl.BlockSpec(memory_space=pl.ANY),
            pl.BlockSpec(memory_space=pl.ANY),
            pl.BlockSpec((N_HEADS, t_dec, kv_rank), lambda b, pt: (0, b, 0)),
            pl.BlockSpec((N_HEADS, t_dec, ROPE_DIM), lambda b, pt: (0, b, 0)),
            pl.BlockSpec((t_dec, kv_rank), lambda b, pt: (b, 0)),
            pl.BlockSpec((t_dec, ROPE_DIM), lambda b, pt: (b, 0)),
            pl.BlockSpec(w["wuk_t"].shape, lambda b, pt: (0, 0)),
            pl.BlockSpec(w["wuv"].shape, lambda b, pt: (0, 0)),
            pl.BlockSpec(w["g_out_a_row"].shape, lambda b, pt: (0, 0, 0)),
        ],
        out_specs=pl.BlockSpec((t_dec, N_HEADS * V_DIM), lambda b, pt: (b, 0)),
        scratch_shapes=[pltpu.VMEM((2, chunk, kv_rank), F32), pltpu.VMEM((2, chunk_pages, ROPE_DIM, page), F32),
                        pltpu.SemaphoreType.DMA((2, 2))],
    )
    return pl.pallas_call(
        functools.partial(_attn_decode_kernel, layer=layer, n_pages=n_pages, page=page, t_dec=t_dec),
        grid_spec=grid_spec,
        out_shape=jax.ShapeDtypeStruct((n_seq * t_dec, N_HEADS * V_DIM), F32),
        compiler_params=_params(("arbitrary",)),
        name="attn_decode",
    )(page_table.reshape(-1), cache_ckv, cache_kr, qa, qr, c_new, kr_new, w["wuk_t"], w["wuv"], w["g_out_a_row"])


ROUTE_E0, ROUTE_E1, ROUTE_R0, ROUTE_R1, ROUTE_W0, ROUTE_W1 = range(6)


def _lane_pick(x, lane, idx):
    return jnp.sum(jnp.where(lane == idx, x, 0.0), axis=-1, keepdims=True)


def _store_row_tiles(ref, x):
    rows, width = x.shape
    n = width // LANES
    for j in range(n):
        ref[pl.ds(j, rows, stride=n), :] = x[:, j * LANES:(j + 1) * LANES]


def _load_row_tiles(ref, rows, n):
    return jnp.concatenate([ref[pl.ds(j, rows, stride=n), :] for j in range(n)], axis=1)


def _merge_route_kernel(ycp_ref, yap_ref, xp_ref, ycs_ref, yas_ref, xs_ref, wout_ref, gffn_ref, wr_ref, br_ref,
                        xmid_ref, h2_ref, route_ref, counts_ref, carry_ref, *, n_prompt_tiles):
    i = pl.program_id(0)
    tm = xp_ref.shape[0]
    is_p = i < n_prompt_tiles

    @pl.when(i == 0)
    def _():
        carry_ref[...] = jnp.zeros(carry_ref.shape, F32)

    yc = jnp.where(is_p, ycp_ref[...], ycs_ref[...])
    ya = jnp.where(is_p, yap_ref[...], yas_ref[...].astype(BF16))
    x = jnp.where(is_p, xp_ref[...], xs_ref[...])
    y = jnp.concatenate([yc, ya], axis=1)
    xm = x + jnp.dot(y, wout_ref[...], preferred_element_type=F32)
    xmid_ref[...] = xm
    h2f = xm * _rms_lanes(xm) * gffn_ref[...]
    _store_row_tiles(h2_ref, h2f)
    h2 = h2f.astype(BF16)
    logits = jnp.dot(h2, wr_ref[...], preferred_element_type=F32) + br_ref[...]

    lane_i = lax.broadcasted_iota(jnp.int32, (tm, LANES), 1)
    lane = lane_i.astype(F32)
    neg = -jnp.inf
    far = float(LANES)
    in_groups = lane_i < N_GROUPS
    gl = jnp.where(in_groups, logits, neg)
    ge = jnp.exp(gl - jnp.max(gl, axis=-1, keepdims=True))
    pg = ge / jnp.sum(ge, axis=-1, keepdims=True)
    p_sel = jnp.max(pg, axis=-1, keepdims=True)
    g_sel = jnp.min(jnp.where((pg == p_sel) & in_groups, lane, far), axis=-1, keepdims=True)
    lo = N_GROUPS + g_sel * EXPERTS_PER_GROUP
    el = jnp.where((lane >= lo) & (lane < lo + EXPERTS_PER_GROUP), logits, neg)
    v1 = jnp.max(el, axis=-1, keepdims=True)
    i1 = jnp.min(jnp.where(el == v1, lane, far), axis=-1, keepdims=True)
    el2 = jnp.where(lane == i1, neg, el)
    v2 = jnp.max(el2, axis=-1, keepdims=True)
    i2 = jnp.min(jnp.where(el2 == v2, lane, far), axis=-1, keepdims=True)
    e2 = jnp.exp(v2 - v1)
    w0 = 1.0 / (1.0 + e2) * p_sel
    w1 = e2 / (1.0 + e2) * p_sel
    e0 = i1 - N_GROUPS
    e1 = i2 - N_GROUPS

    oh0 = lane == e0
    oh1 = lane == e1
    onehot = (oh0 | oh1).astype(BF16)
    tri = (lax.broadcasted_iota(jnp.int32, (tm, tm), 0) > lax.broadcasted_iota(jnp.int32, (tm, tm), 1)).astype(BF16)
    before = jnp.dot(tri, onehot, preferred_element_type=F32) + carry_ref[...]
    r0 = jnp.sum(jnp.where(oh0, before, 0.0), axis=-1, keepdims=True)
    r1 = jnp.sum(jnp.where(oh1, before, 0.0), axis=-1, keepdims=True)
    carry_ref[...] += jnp.sum(onehot.astype(F32), axis=0, keepdims=True)

    route = jnp.zeros((tm, LANES), F32)
    for idx, val in ((ROUTE_E0, e0), (ROUTE_E1, e1), (ROUTE_R0, r0), (ROUTE_R1, r1), (ROUTE_W0, w0), (ROUTE_W1, w1)):
        route = jnp.where(lane_i == idx, val, route)
    route_ref[...] = route.T[:SUBLANES]

    @pl.when(i == pl.num_programs(0) - 1)
    def _():
        counts_ref[...] = carry_ref[...]


def _merge_route(ycp, yap, xp, ycs, yas, xs, w):
    tp, d = xp.shape
    ts = xs.shape[0]
    tm = TOKEN_TILE
    assert tp % tm == 0 and ts % tm == 0
    npt, nst = tp // tm, ts // tm
    half = ycp.shape[1]
    pmap = lambda i: (jnp.minimum(i, npt - 1), 0)
    smap = lambda i: (jnp.maximum(i - npt, 0), 0)
    cmap = lambda i: (0, 0)
    return pl.pallas_call(
        functools.partial(_merge_route_kernel, n_prompt_tiles=npt),
        grid=(npt + nst,),
        in_specs=[
            pl.BlockSpec((tm, half), pmap), pl.BlockSpec((tm, half), pmap), pl.BlockSpec((tm, d), pmap),
            pl.BlockSpec((tm, half), smap), pl.BlockSpec((tm, half), smap), pl.BlockSpec((tm, d), smap),
            pl.BlockSpec(w["w_out"].shape, cmap), pl.BlockSpec(w["g_ffn"].shape, cmap),
            pl.BlockSpec(w["w_r"].shape, cmap), pl.BlockSpec(w["b_r"].shape, cmap),
        ],
        out_specs=[pl.BlockSpec((tm, d), lambda i: (i, 0)),
                   pl.BlockSpec((tm * (d // LANES), LANES), lambda i: (i, 0)),
                   pl.BlockSpec((SUBLANES, tm), lambda i: (0, i)),
                   pl.BlockSpec((1, LANES), cmap)],
        out_shape=[jax.ShapeDtypeStruct((tp + ts, d), F32),
                   jax.ShapeDtypeStruct(((tp + ts) * (d // LANES), LANES), F32),
                   jax.ShapeDtypeStruct((SUBLANES, tp + ts), F32),
                   jax.ShapeDtypeStruct((1, LANES), F32)],
        scratch_shapes=[pltpu.VMEM((1, LANES), F32)],
        compiler_params=_params(("arbitrary",)),
        name="merge_route",
    )(ycp, yap, xp, ycs, yas, xs, w["w_out"], w["g_ffn"], w["w_r"], w["b_r"])


def _tile_of(ref, row, n):
    start = row * n
    return ref.at[pl.ds(start if isinstance(row, int) else pl.multiple_of(start, n), n)]


def _pad_fill_copies(base_ref, cnt_ref, tiles_ref, nu_ref, zeros_ref, xs_hbm, sem, tme, n, n_tiles):
    out = []
    for e in range(N_EXPERTS):
        pad = tiles_ref[e] * tme - cnt_ref[e]
        pos = base_ref[e] + cnt_ref[e]
        bit = tme // 2
        while bit >= 1:
            take = pad & bit
            out.append((take != 0, pltpu.make_async_copy(
                zeros_ref.at[pl.ds(0, bit * n)], xs_hbm.at[pl.ds(pl.multiple_of(pos * n, n), bit * n)], sem)))
            pos = pos + take
            bit //= 2
    for k in range(N_EXPERTS + 1):
        tile = nu_ref[0] + k
        out.append((tile < n_tiles, pltpu.make_async_copy(
            zeros_ref, xs_hbm.at[pl.ds(pl.multiple_of(jnp.minimum(tile, n_tiles - 1) * (tme * n), tme * n), tme * n)],
            sem)))
    return out


def _dispatch_kernel(base_ref, cnt_ref, tiles_ref, nu_ref, s0_ref, s1_ref, h2_ref, xs_hbm, zeros, sem,
                     *, tm, tme, n, n_tiles):
    i = pl.program_id(0)
    for t0 in range(0, tm, SLOT_BATCH):
        slots = [(s0_ref[0, 0, t], s1_ref[0, 0, t]) for t in range(t0, t0 + SLOT_BATCH)]
        for t, pair in zip(range(t0, t0 + SLOT_BATCH), slots):
            for k, s in enumerate(pair):
                pltpu.make_async_copy(_tile_of(h2_ref, t, n), _tile_of(xs_hbm, s, n), sem.at[0]).start(priority=k)

    def fills():
        return _pad_fill_copies(base_ref, cnt_ref, tiles_ref, nu_ref, zeros, xs_hbm, sem.at[1], tme, n, n_tiles)

    @pl.when(i == 0)
    def _():
        zeros[...] = jnp.zeros(zeros.shape, F32)
        for pred, cp in fills():
            pl.when(pred)(cp.start)

    for t in range(2 * tm):
        pltpu.make_async_copy(_tile_of(h2_ref, 0, n), _tile_of(xs_hbm, 0, n), sem.at[0]).wait()

    @pl.when(i == 0)
    def _():
        for pred, cp in fills():
            pl.when(pred)(cp.wait)


def _dispatch(plan, h2_tiles, n_slots):
    base, cnt, tiles, n_used, slot0, slot1 = plan
    n_tok_tiles, _, tm = slot0.shape
    n = h2_tiles.shape[0] // (n_tok_tiles * tm)
    tme = EXPERT_TILE
    n_tiles = n_slots // tme
    smem_blk = pl.BlockSpec((1, 1, tm), lambda i, *_: (i, 0, 0), memory_space=pltpu.SMEM)
    grid_spec = pltpu.PrefetchScalarGridSpec(
        num_scalar_prefetch=4,
        grid=(n_tok_tiles,),
        in_specs=[smem_blk] * 2 + [pl.BlockSpec((tm * n, LANES), lambda i, *_: (i, 0))],
        out_specs=pl.BlockSpec(memory_space=pl.ANY),
        scratch_shapes=[pltpu.VMEM((tme * n, LANES), F32), pltpu.SemaphoreType.DMA((2,))],
    )
    return pl.pallas_call(
        functools.partial(_dispatch_kernel, tm=tm, tme=tme, n=n, n_tiles=n_tiles),
        grid_spec=grid_spec,
        out_shape=jax.ShapeDtypeStruct((n_slots * n, LANES), F32),
        compiler_params=_params(("arbitrary",)),
        name="dispatch",
    )(base, cnt, tiles, n_used, slot0, slot1, h2_tiles)


def _experts_kernel(te_ref, nu_ref, xs_ref, wg_ref, wu_ref, wd_ref, ys_ref, wgu_bf, wd_bf):
    i = pl.program_id(0)
    n_used = nu_ref[0]
    d_exp = wd_ref.shape[2]
    d = wd_ref.shape[3]
    n = d // LANES
    tme = xs_ref.shape[0] // n
    new_expert = (i == 0) | (te_ref[i] != te_ref[jnp.maximum(i - 1, 0)])

    @pl.when(new_expert & (i < n_used))
    def _():
        wgu_bf[:, :d_exp] = wg_ref[0, 0].astype(BF16)
        wgu_bf[:, d_exp:] = wu_ref[0, 0].astype(BF16)
        wd_bf[...] = wd_ref[0, 0].astype(BF16)

    @pl.when(i < n_used)
    def _():
        h2 = _load_row_tiles(xs_ref, tme, n).astype(BF16)
        gu = jnp.dot(h2, wgu_bf[...], preferred_element_type=F32)
        g = gu[:, :d_exp]
        a = (g / (1.0 + jnp.exp(-g))) * gu[:, d_exp:]
        _store_row_tiles(ys_ref, jnp.dot(a.astype(BF16), wd_bf[...], preferred_element_type=F32))

    @pl.when(i >= n_used)
    def _():
        ys_ref[...] = jnp.zeros(ys_ref.shape, F32)


def _experts(layer, tile_expert, n_used, xs_tiles, w_gate, w_up, w_down):
    n_tiles = tile_expert.shape[0]
    _, _, d, d_exp = w_gate.shape
    blk = xs_tiles.shape[0] // n_tiles
    grid_spec = pltpu.PrefetchScalarGridSpec(
        num_scalar_prefetch=2,
        grid=(n_tiles,),
        in_specs=[
            pl.BlockSpec((blk, LANES), lambda i, te, nu: (jnp.minimum(i, nu[0] - 1), 0)),
            pl.BlockSpec((1, 1, d, d_exp), lambda i, te, nu: (layer, te[i], 0, 0)),
            pl.BlockSpec((1, 1, d, d_exp), lambda i, te, nu: (layer, te[i], 0, 0)),
            pl.BlockSpec((1, 1, d_exp, d), lambda i, te, nu: (layer, te[i], 0, 0)),
        ],
        out_specs=pl.BlockSpec((blk, LANES), lambda i, te, nu: (i, 0)),
        scratch_shapes=[pltpu.VMEM((d, 2 * d_exp), BF16), pltpu.VMEM((d_exp, d), BF16)],
    )
    return pl.pallas_call(
        _experts_kernel,
        grid_spec=grid_spec,
        out_shape=jax.ShapeDtypeStruct(xs_tiles.shape, F32),
        compiler_params=_params(("arbitrary",)),
        name="experts",
    )(tile_expert, n_used, xs_tiles, w_gate, w_up, w_down)


def _combine_kernel(s0_ref, s1_ref, s0n_ref, s1n_ref, ys_hbm, xmid_ref, route_ref, yp_ref, ysmp_ref, buf, sem,
                    *, n_prompt_tiles):
    i = pl.program_id(0)
    last = pl.num_programs(0) - 1
    tm, d = xmid_ref.shape
    n = d // LANES
    slot = lax.rem(i, 2)

    def gather(refs, dst_slot):
        s0, s1 = refs
        for t0 in range(0, tm, SLOT_BATCH):
            slots = [(s0[0, 0, t], s1[0, 0, t]) for t in range(t0, t0 + SLOT_BATCH)]
            for t, pair in zip(range(t0, t0 + SLOT_BATCH), slots):
                for k, s in enumerate(pair):
                    pltpu.make_async_copy(_tile_of(ys_hbm, s, n), _tile_of(buf.at[dst_slot, k], t, n),
                                          sem.at[dst_slot]).start(priority=k)

    def gather_wait(dst_slot):
        for t in range(2 * tm):
            pltpu.make_async_copy(_tile_of(ys_hbm, 0, n), _tile_of(buf.at[dst_slot, 0], 0, n), sem.at[dst_slot]).wait()

    @pl.when(i == 0)
    def _():
        gather((s0_ref, s1_ref), 0)

    gather((s0n_ref, s1n_ref), 1 - slot)
    gather_wait(slot)

    rt = route_ref[...]
    cols = jnp.concatenate([rt, jnp.zeros((LANES - rt.shape[0], tm), F32)], axis=0).T
    lane = lax.broadcasted_iota(jnp.int32, cols.shape, 1)
    w0 = _lane_pick(cols, lane, ROUTE_W0)
    w1 = _lane_pick(cols, lane, ROUTE_W1)
    y0 = _load_row_tiles(buf.at[slot, 0], tm, n)
    y1 = _load_row_tiles(buf.at[slot, 1], tm, n)
    out = xmid_ref[...] + (w0 * y0 + w1 * y1)

    @pl.when(i < n_prompt_tiles)
    def _():
        yp_ref[...] = out

    @pl.when(i >= n_prompt_tiles)
    def _():
        ysmp_ref[...] = out

    @pl.when(i == last)
    def _():
        gather_wait(1 - slot)


def _combine(plan, ys_tiles, xmid, route_t, tp):
    slot0, slot1 = plan[4:]
    ttot, d = xmid.shape
    n_tok_tiles, _, tm = slot0.shape
    npt = tp // tm
    n = d // LANES
    cur = pl.BlockSpec((1, 1, tm), lambda i: (i, 0, 0), memory_space=pltpu.SMEM)
    nxt = pl.BlockSpec((1, 1, tm), lambda i: (jnp.minimum(i + 1, n_tok_tiles - 1), 0, 0), memory_space=pltpu.SMEM)
    return pl.pallas_call(
        functools.partial(_combine_kernel, n_prompt_tiles=npt),
        grid=(n_tok_tiles,),
        in_specs=[cur, cur, nxt, nxt,
                  pl.BlockSpec(memory_space=pl.ANY),
                  pl.BlockSpec((tm, d), lambda i: (i, 0)),
                  pl.BlockSpec((SUBLANES, tm), lambda i: (0, i))],
        out_specs=[pl.BlockSpec((tm, d), lambda i: (jnp.minimum(i, npt - 1), 0)),
                   pl.BlockSpec((tm, d), lambda i: (jnp.maximum(i - npt, 0), 0))],
        out_shape=[jax.ShapeDtypeStruct((tp, d), F32), jax.ShapeDtypeStruct((ttot - tp, d), F32)],
        scratch_shapes=[pltpu.VMEM((2, 2, tm * n, LANES), F32), pltpu.SemaphoreType.DMA((2,))],
        compiler_params=_params(("arbitrary",)),
        name="combine",
    )(slot0, slot1, slot0, slot1, ys_tiles, xmid, route_t)


def _rope_tables(pos):
    inv_freq = ROPE_BASE ** (-jnp.arange(HALF_ROPE, dtype=F32) / HALF_ROPE)
    ang = pos.astype(F32)[:, None] * inv_freq[None, :]
    return jnp.cos(ang).T, jnp.sin(ang).T


def _layer_weights(l, g_mix, w_in, conv_w, conv_b, g_q_lat, w_uq, g_kv_lat, w_uk, w_uv, g_q_nope, g_q_rope,
                   g_k_nope, g_k_rope, g_out, w_out, g_ffn, w_router_group, b_router_group, w_router_expert,
                   b_router_expert, w_gate, w_up, w_down):
    c = conv_w.shape[2]
    col = lambda g: g[l].reshape(-1, 1)
    w_r = jnp.concatenate([w_router_group[l], w_router_expert[l]], axis=1)
    b_r = jnp.concatenate([b_router_group[l], b_router_expert[l]])
    return {
        "g_mix": g_mix[l].reshape(1, -1),
        "win_c": w_in[l][:, :3 * c].astype(BF16),
        "win_at": w_in[l][:, 3 * c:].T.astype(BF16),
        "conv_w": conv_w[l],
        "conv_b": conv_b[l].reshape(1, -1),
        "g_out_c": g_out[l][:c].reshape(1, -1),
        "g_out_a": g_out[l][c:].reshape(N_HEADS, V_DIM, 1),
        "g_out_a_row": g_out[l][c:].reshape(N_HEADS, 1, V_DIM),
        "g_qlat": col(g_q_lat), "g_kv": col(g_kv_lat), "g_qn": col(g_q_nope), "g_qr": col(g_q_rope),
        "g_kn": col(g_k_nope), "g_kr": col(g_k_rope),
        "wuq_t": w_uq[l].T.astype(BF16),
        "wuk_t": w_uk[l].T.astype(BF16),
        "wuk": w_uk[l].astype(BF16),
        "wuv_t": w_uv[l].T.astype(BF16),
        "wuv": w_uv[l].astype(BF16),
        "w_out": w_out[l].astype(BF16),
        "g_ffn": g_ffn[l].reshape(1, -1),
        "w_r": jnp.pad(w_r, ((0, 0), (0, LANES - w_r.shape[1]))).astype(BF16),
        "b_r": jnp.pad(b_r, (0, LANES - b_r.shape[0])).reshape(1, -1),
    }


def _moe_plan(route_t, counts):
    ttot = route_t.shape[1]
    tme = EXPERT_TILE
    tm = TOKEN_TILE
    n_tiles = (TOP_K * ttot) // tme + N_EXPERTS + 1
    cnt = counts[0, :N_EXPERTS].astype(jnp.int32)
    tiles = (cnt + tme - 1) // tme
    tile_end = jnp.cumsum(tiles)
    base = (tile_end - tiles) * tme
    n_used = tile_end[-1:]
    tile_id = jnp.minimum(jnp.arange(n_tiles, dtype=jnp.int32), n_used[0] - 1)
    tile_expert = jnp.sum((tile_end[None, :] <= tile_id[:, None]).astype(jnp.int32), axis=1)
    ids = route_t[:ROUTE_R1 + 1].astype(jnp.int32)
    expert_ids = jnp.arange(N_EXPERTS, dtype=jnp.int32)[:, None]

    def slots(e, r):
        s = r + jnp.sum(jnp.where(e[None, :] == expert_ids, base[:, None], 0), axis=0)
        return s.reshape(ttot // tm, 1, tm)

    plan = (base, cnt, tiles, n_used, slots(ids[ROUTE_E0], ids[ROUTE_R0]), slots(ids[ROUTE_E1], ids[ROUTE_R1]))
    return plan, tile_expert, n_tiles * tme


def kernel(x_prompt, x_sample, state_conv, cache_ckv, cache_krope, page_table, g_mix, w_in, conv_w, conv_b, g_q_lat, w_uq, g_kv_lat, w_uk, w_uv, g_q_nope, g_q_rope, g_k_nope, g_k_rope, g_out, w_out, g_ffn, w_router_group, b_router_group, w_router_expert, b_router_expert, w_gate, w_up, w_down):
    b_p, s_p, d = x_prompt.shape
    b_s, t_s, _ = x_sample.shape
    depth = g_mix.shape[0]
    c = conv_w.shape[2]
    page = cache_ckv.shape[2]
    past_len = page_table.shape[1] * page
    kv_rank = cache_ckv.shape[3]

    cos_p, sin_p = _rope_tables(jnp.arange(s_p, dtype=jnp.int32))
    cos_s, sin_s = _rope_tables(jnp.tile(past_len + jnp.arange(t_s, dtype=jnp.int32), b_s))
    tpos = jnp.tile(jnp.arange(t_s, dtype=jnp.int32), b_s).reshape(-1, 1)

    krope_pages = jnp.swapaxes(cache_krope, 2, 3)

    xp, xs = x_prompt, x_sample.reshape(b_s * t_s, d)
    outs = [[] for _ in range(6)]
    for l in range(depth):
        w = _layer_weights(l, g_mix, w_in, conv_w, conv_b, g_q_lat, w_uq, g_kv_lat, w_uk, w_uv, g_q_nope,
                           g_q_rope, g_k_nope, g_k_rope, g_out, w_out, g_ffn, w_router_group, b_router_group,
                           w_router_expert, b_router_expert, w_gate, w_up, w_down)
        yconv_p, qt, kt, vt, ckv_p, kr_p, conv_p = _proj_prompt(xp, w, cos_p, sin_p)
        yattn_p = _attn_prompt(qt, kt, vt, w["g_out_a"])
        st = state_conv[l]
        zeros = lambda n: jnp.zeros((b_s, n, c), F32)
        st1 = jnp.concatenate([st[:, CONV_K - 2:], zeros(t_s - 1)], axis=1).reshape(b_s * t_s, c)
        st2 = jnp.concatenate([st, zeros(t_s - (CONV_K - 1))], axis=1).reshape(b_s * t_s, c)
        yconv_s, u_s, qa, qr, ckv_s, kr_s = _proj_sample(xs, tpos, st1, st2, w, cos_s, sin_s)
        yattn_s = _attn_decode(l, page_table, cache_ckv, krope_pages, qa, qr, ckv_s, kr_s, w, t_s)
        xmid, h2_tiles, route_t, counts = _merge_route(
            yconv_p.reshape(b_p * s_p, c), yattn_p.reshape(b_p * s_p, -1), xp.reshape(b_p * s_p, d),
            yconv_s, yattn_s, xs, w)
        plan, tile_expert, n_slots = _moe_plan(route_t, counts)
        xs_tiles = _dispatch(plan, h2_tiles, n_slots)
        ys_tiles = _experts(l, tile_expert, plan[3], xs_tiles, w_gate, w_up, w_down)
        yp, ysmp = _combine(plan, ys_tiles, xmid, route_t, b_p * s_p)
        xp, xs = yp.reshape(b_p, s_p, d), ysmp
        for lst, val in zip(outs, (ckv_p, kr_p, conv_p, ckv_s.reshape(b_s, t_s, kv_rank),
                                   kr_s.reshape(b_s, t_s, ROPE_DIM),
                                   u_s.reshape(b_s, t_s, c)[:, t_s - (CONV_K - 1):])):
            lst.append(val)
    return (xp, xs.reshape(b_s, t_s, d)) + tuple(jnp.stack(o) for o in outs)
```

```python
import functools

import jax
import jax.numpy as jnp
from jax import lax
from jax.experimental import pallas as pl
from jax.experimental.pallas import tpu as pltpu

N_HEADS = 8
NOPE_DIM = 64
ROPE_DIM = 32
V_DIM = 64
HEAD_QK = NOPE_DIM + ROPE_DIM
HALF_ROPE = ROPE_DIM // 2
ROPE_BASE = 10000.0
CONV_K = 3
OUT_GROUP_DIM = 64
N_GROUPS = 4
EXPERTS_PER_GROUP = 8
N_EXPERTS = N_GROUPS * EXPERTS_PER_GROUP
TOP_K = 2
EPS = 1e-6
MASK_VALUE = -1e30
LOG2_E = 1.4426950408889634

LANES = 128
SUBLANES = 8
VMEM_LIMIT_BYTES = 48 * 1024 * 1024

PROJ_TILE = 512
ATTN_TILE = 512
TOKEN_TILE = 256
EXPERT_TILE = 256
DECODE_CHUNK_PAGES = 8
RING = 3
SLOT_BATCH = 8

F32 = jnp.float32
BF16 = jnp.bfloat16

_NT = (((1,), (1,)), ((), ()))
_TN = (((0,), (0,)), ((), ()))


def _params(sem):
    return pltpu.CompilerParams(dimension_semantics=sem, vmem_limit_bytes=VMEM_LIMIT_BYTES)


def _rms_rows(x):
    return lax.rsqrt(jnp.mean(x * x, axis=0, keepdims=True) + EPS)


def _rms_lanes(x):
    return lax.rsqrt(jnp.mean(x * x, axis=-1, keepdims=True) + EPS)


def _group_norm_lanes(y, gain):
    lane = lax.broadcasted_iota(jnp.int32, (1, LANES), 1)
    low = lane < OUT_GROUP_DIM
    outs = []
    for j in range(y.shape[1] // LANES):
        t = y[:, j * LANES:(j + 1) * LANES]
        sq = t * t
        ss_lo = jnp.sum(jnp.where(low, sq, 0.0), axis=-1, keepdims=True)
        ss_hi = jnp.sum(jnp.where(low, 0.0, sq), axis=-1, keepdims=True)
        r = jnp.where(low, lax.rsqrt(ss_lo / OUT_GROUP_DIM + EPS), lax.rsqrt(ss_hi / OUT_GROUP_DIM + EPS))
        outs.append(t * r)
    return jnp.concatenate(outs, axis=1) * gain


def _rope_rows(x, cos, sin):
    x1, x2 = x[:HALF_ROPE], x[HALF_ROPE:]
    return x1 * cos - x2 * sin, x1 * sin + x2 * cos


def _attention_side(h_bf16, win_at_ref, gqlat_ref, wuqt_ref, gkv_ref, gqn_ref, gqr_ref, gkr_ref, cos, sin):
    q_rank = gqlat_ref.shape[0]
    kv_rank = gkv_ref.shape[0]
    scale = HEAD_QK ** -0.5 * LOG2_E
    zat = lax.dot_general(win_at_ref[...], h_bf16, _NT, preferred_element_type=F32)
    qlt = zat[:q_rank]
    kvt = zat[q_rank:q_rank + kv_rank]
    krt = zat[q_rank + kv_rank:]
    qln = (qlt * _rms_rows(qlt) * gqlat_ref[...]).astype(BF16)
    qt = jnp.dot(wuqt_ref[...], qln, preferred_element_type=F32)
    q_nope, q_rope = [], []
    for h in range(N_HEADS):
        nope = qt[h * HEAD_QK:h * HEAD_QK + NOPE_DIM]
        rope = qt[h * HEAD_QK + NOPE_DIM:(h + 1) * HEAD_QK]
        q_nope.append(nope * _rms_rows(nope) * gqn_ref[...] * scale)
        r1, r2 = _rope_rows(rope * _rms_rows(rope) * gqr_ref[...], cos, sin)
        q_rope.append((r1 * scale, r2 * scale))
    ckvt = kvt * _rms_rows(kvt) * gkv_ref[...]
    k1, k2 = _rope_rows(krt * _rms_rows(krt) * gkr_ref[...], cos, sin)
    return q_nope, q_rope, ckvt, (k1, k2)


def _to_token_major(xt, width):
    rows, toks = xt.shape
    if rows < LANES:
        xt = jnp.concatenate([xt, jnp.zeros((LANES - rows, toks), F32)], axis=0)
    return xt.T[:, :width]


def _proj_prompt_kernel(x_ref, gmix_ref, win_c_ref, win_at_ref, convw_ref, convb_ref, gout_c_ref,
                        gqlat_ref, wuqt_ref, gkv_ref, gqn_ref, gqr_ref, gkn_ref, gkr_ref,
                        wukt_ref, wuvt_ref, cos_ref, sin_ref,
                        yconv_ref, qt_ref, kt_ref, vt_ref, ckv_ref, krope_ref, convst_ref,
                        ext_ref):
    si = pl.program_id(1)
    tm = x_ref.shape[1]
    c = convw_ref.shape[1]

    xf = x_ref[0]
    h = (xf * _rms_lanes(xf) * gmix_ref[...]).astype(BF16)

    zc = jnp.dot(h, win_c_ref[...], preferred_element_type=F32)
    u = zc[:, 2 * c:] * zc[:, :c]

    @pl.when(si == 0)
    def _():
        ext_ref[0:SUBLANES, :] = jnp.zeros((SUBLANES, c), F32)

    ext_ref[SUBLANES:, :] = u
    v = (convb_ref[...]
         + convw_ref[0:1, :] * ext_ref[pl.ds(SUBLANES - 2, tm), :]
         + convw_ref[1:2, :] * ext_ref[pl.ds(SUBLANES - 1, tm), :]
         + convw_ref[2:3, :] * u)
    yconv = zc[:, c:2 * c] * v
    yconv_ref[0] = _group_norm_lanes(yconv, gout_c_ref[...]).astype(BF16)
    ext_ref[0:SUBLANES, :] = ext_ref[pl.ds(tm, SUBLANES), :]
    convst_ref[0] = ext_ref[pl.ds(SUBLANES - (CONV_K - 1), CONV_K - 1), :]

    q_nope, q_rope, ckvt, (k1, k2) = _attention_side(
        h, win_at_ref, gqlat_ref, wuqt_ref, gkv_ref, gqn_ref, gqr_ref, gkr_ref, cos_ref[...], sin_ref[...])
    for hd in range(N_HEADS):
        qt_ref[0, hd, 0:NOPE_DIM, :] = q_nope[hd].astype(BF16)
        qt_ref[0, hd, NOPE_DIM:NOPE_DIM + HALF_ROPE, :] = q_rope[hd][0].astype(BF16)
        qt_ref[0, hd, NOPE_DIM + HALF_ROPE:HEAD_QK, :] = q_rope[hd][1].astype(BF16)
    ckv_ref[0] = ckvt.T
    krt = jnp.concatenate([k1, k2], axis=0)
    krope_ref[0] = _to_token_major(krt, ROPE_DIM)
    ckv_b = ckvt.astype(BF16)
    ktn = jnp.dot(wukt_ref[...], ckv_b, preferred_element_type=F32)
    vt = jnp.dot(wuvt_ref[...], ckv_b, preferred_element_type=F32)
    krt_b = krt.astype(BF16)
    for hd in range(N_HEADS):
        blk = ktn[hd * NOPE_DIM:(hd + 1) * NOPE_DIM]
        kt_ref[0, hd, 0:NOPE_DIM, :] = (blk * _rms_rows(blk) * gkn_ref[...]).astype(BF16)
        kt_ref[0, hd, NOPE_DIM:HEAD_QK, :] = krt_b
        vt_ref[0, hd] = vt[hd * V_DIM:(hd + 1) * V_DIM].astype(BF16)


def _proj_prompt(x, w, cos_t, sin_t):
    b, s, d = x.shape
    tm = min(PROJ_TILE, s)
    assert s % tm == 0
    c = w["conv_w"].shape[1]
    kv_rank = w["g_kv"].shape[0]
    full = lambda a: pl.BlockSpec(a.shape, lambda bi, si: (0,) * a.ndim)
    weights = [w["g_mix"], w["win_c"], w["win_at"], w["conv_w"], w["conv_b"], w["g_out_c"],
               w["g_qlat"], w["wuq_t"], w["g_kv"], w["g_qn"], w["g_qr"], w["g_kn"], w["g_kr"],
               w["wuk_t"], w["wuv_t"]]
    in_specs = ([pl.BlockSpec((1, tm, d), lambda bi, si: (bi, si, 0))] + [full(a) for a in weights]
                + [pl.BlockSpec((HALF_ROPE, tm), lambda bi, si: (0, si))] * 2)
    out_shape = [
        jax.ShapeDtypeStruct((b, s, c), BF16),
        jax.ShapeDtypeStruct((b, N_HEADS, HEAD_QK, s), BF16),
        jax.ShapeDtypeStruct((b, N_HEADS, HEAD_QK, s), BF16),
        jax.ShapeDtypeStruct((b, N_HEADS, V_DIM, s), BF16),
        jax.ShapeDtypeStruct((b, s, kv_rank), F32),
        jax.ShapeDtypeStruct((b, s, ROPE_DIM), F32),
        jax.ShapeDtypeStruct((b, CONV_K - 1, c), F32),
    ]
    out_specs = [
        pl.BlockSpec((1, tm, c), lambda bi, si: (bi, si, 0)),
        pl.BlockSpec((1, N_HEADS, HEAD_QK, tm), lambda bi, si: (bi, 0, 0, si)),
        pl.BlockSpec((1, N_HEADS, HEAD_QK, tm), lambda bi, si: (bi, 0, 0, si)),
        pl.BlockSpec((1, N_HEADS, V_DIM, tm), lambda bi, si: (bi, 0, 0, si)),
        pl.BlockSpec((1, tm, kv_rank), lambda bi, si: (bi, si, 0)),
        pl.BlockSpec((1, tm, ROPE_DIM), lambda bi, si: (bi, si, 0)),
        pl.BlockSpec((1, CONV_K - 1, c), lambda bi, si: (bi, 0, 0)),
    ]
    return pl.pallas_call(
        _proj_prompt_kernel,
        grid=(b, s // tm),
        in_specs=in_specs,
        out_specs=out_specs,
        out_shape=out_shape,
        scratch_shapes=[pltpu.VMEM((tm + SUBLANES, c), F32)],
        compiler_params=_params(("arbitrary", "arbitrary")),
        name="proj_prompt",
    )(x, *weights, cos_t, sin_t)


def _attn_prompt_kernel(qi_ref, ki_ref, qt_ref, kt_ref, vt_ref, gout_ref, y_ref, m_ref, l_ref, acc_ref):
    p = pl.program_id(1)
    qi = qi_ref[p]
    ki = ki_ref[p]
    tq = qt_ref.shape[3]
    tk = kt_ref.shape[3]

    @pl.when(ki == 0)
    def _():
        m_ref[...] = jnp.full(m_ref.shape, -jnp.inf, F32)
        l_ref[...] = jnp.zeros(l_ref.shape, F32)
        acc_ref[...] = jnp.zeros(acc_ref.shape, F32)

    def block(masked):
        if masked:
            visible = (lax.broadcasted_iota(jnp.int32, (tk, tq), 0) <= lax.broadcasted_iota(jnp.int32, (tk, tq), 1))
        for hd in range(N_HEADS):
            s = lax.dot_general(kt_ref[0, hd], qt_ref[0, hd], _TN, preferred_element_type=F32)
            if masked:
                s = jnp.where(visible, s, MASK_VALUE)
            m_prev = m_ref[hd]
            m_new = jnp.maximum(m_prev, jnp.max(s, axis=0, keepdims=True))
            alpha = jnp.exp2(m_prev - m_new)
            pr = jnp.exp2(s - m_new)
            l_ref[hd] = alpha * l_ref[hd] + jnp.sum(pr, axis=0, keepdims=True)
            pv = jnp.dot(vt_ref[0, hd], pr.astype(BF16), preferred_element_type=F32)
            acc_ref[hd] = alpha * acc_ref[hd] + pv
            m_ref[hd] = m_new

    @pl.when(ki < qi)
    def _():
        block(False)

    @pl.when(ki == qi)
    def _():
        block(True)
        outs = []
        for hd in range(N_HEADS):
            o = acc_ref[hd] / l_ref[hd]
            outs.append(o * _rms_rows(o) * gout_ref[hd])
        y_ref[0] = jnp.concatenate(outs, axis=0).T.astype(BF16)


def _attn_prompt(qt, kt, vt, gout_a):
    b, _, _, s = qt.shape
    t = min(ATTN_TILE, s)
    assert s % t == 0
    n = s // t
    pairs = [(i, j) for i in range(n) for j in range(i + 1)]
    qi_tab = jnp.asarray([p[0] for p in pairs], jnp.int32)
    ki_tab = jnp.asarray([p[1] for p in pairs], jnp.int32)
    grid_spec = pltpu.PrefetchScalarGridSpec(
        num_scalar_prefetch=2,
        grid=(b, len(pairs)),
        in_specs=[
            pl.BlockSpec((1, N_HEADS, HEAD_QK, t), lambda bi, p, qi, ki: (bi, 0, 0, qi[p])),
            pl.BlockSpec((1, N_HEADS, HEAD_QK, t), lambda bi, p, qi, ki: (bi, 0, 0, ki[p])),
            pl.BlockSpec((1, N_HEADS, V_DIM, t), lambda bi, p, qi, ki: (bi, 0, 0, ki[p])),
            pl.BlockSpec(gout_a.shape, lambda bi, p, qi, ki: (0, 0, 0)),
        ],
        out_specs=pl.BlockSpec((1, t, N_HEADS * V_DIM), lambda bi, p, qi, ki: (bi, qi[p], 0)),
        scratch_shapes=[pltpu.VMEM((N_HEADS, 1, t), F32), pltpu.VMEM((N_HEADS, 1, t), F32),
                        pltpu.VMEM((N_HEADS, V_DIM, t), F32)],
    )
    return pl.pallas_call(
        _attn_prompt_kernel,
        grid_spec=grid_spec,
        out_shape=jax.ShapeDtypeStruct((b, s, N_HEADS * V_DIM), BF16),
        compiler_params=_params(("arbitrary", "arbitrary")),
        name="attn_prompt",
    )(qi_tab, ki_tab, qt, kt, vt, gout_a)


def _proj_sample_kernel(x_ref, tpos_ref, st1_ref, st2_ref, gmix_ref, win_c_ref, win_at_ref, convw_ref,
                        convb_ref, gout_c_ref, gqlat_ref, wuqt_ref, gkv_ref, gqn_ref, gqr_ref, gkn_ref,
                        gkr_ref, wuk_ref, cos_ref, sin_ref,
                        yconv_ref, u_ref, qa_ref, qr_ref, ckv_ref, krope_ref, ext_ref):
    tm = x_ref.shape[0]
    c = convw_ref.shape[1]
    xf = x_ref[...]
    h = (xf * _rms_lanes(xf) * gmix_ref[...]).astype(BF16)

    zc = jnp.dot(h, win_c_ref[...], preferred_element_type=F32)
    u = zc[:, 2 * c:] * zc[:, :c]
    ext_ref[0:SUBLANES, :] = jnp.zeros((SUBLANES, c), F32)
    ext_ref[SUBLANES:, :] = u
    tpos = tpos_ref[...]
    u_m2 = jnp.where(tpos >= 2, ext_ref[pl.ds(SUBLANES - 2, tm), :], st2_ref[...])
    u_m1 = jnp.where(tpos >= 1, ext_ref[pl.ds(SUBLANES - 1, tm), :], st1_ref[...])
    v = convb_ref[...] + convw_ref[0:1, :] * u_m2 + convw_ref[1:2, :] * u_m1 + convw_ref[2:3, :] * u
    yconv = zc[:, c:2 * c] * v
    yconv_ref[...] = _group_norm_lanes(yconv, gout_c_ref[...]).astype(BF16)
    u_ref[...] = u

    q_nope, q_rope, ckvt, (k1, k2) = _attention_side(
        h, win_at_ref, gqlat_ref, wuqt_ref, gkv_ref, gqn_ref, gqr_ref, gkr_ref, cos_ref[...], sin_ref[...])
    for hd in range(N_HEADS):
        qg = (q_nope[hd] * gkn_ref[...]).astype(BF16)
        qa_t = jnp.dot(wuk_ref[:, hd * NOPE_DIM:(hd + 1) * NOPE_DIM], qg, preferred_element_type=F32)
        qa_ref[hd] = qa_t.T
        qr_ref[hd] = _to_token_major(jnp.concatenate(q_rope[hd], axis=0), ROPE_DIM)
    ckv_ref[...] = ckvt.T
    krope_ref[...] = _to_token_major(jnp.concatenate([k1, k2], axis=0), ROPE_DIM)


def _proj_sample(x, tpos, st1, st2, w, cos_t, sin_t):
    tm, d = x.shape
    c = w["conv_w"].shape[1]
    kv_rank = w["g_kv"].shape[0]
    args = [x, tpos, st1, st2, w["g_mix"], w["win_c"], w["win_at"], w["conv_w"], w["conv_b"], w["g_out_c"],
            w["g_qlat"], w["wuq_t"], w["g_kv"], w["g_qn"], w["g_qr"], w["g_kn"], w["g_kr"], w["wuk"],
            cos_t, sin_t]
    out_shape = [
        jax.ShapeDtypeStruct((tm, c), BF16),
        jax.ShapeDtypeStruct((tm, c), F32),
        jax.ShapeDtypeStruct((N_HEADS, tm, kv_rank), F32),
        jax.ShapeDtypeStruct((N_HEADS, tm, ROPE_DIM), F32),
        jax.ShapeDtypeStruct((tm, kv_rank), F32),
        jax.ShapeDtypeStruct((tm, ROPE_DIM), F32),
    ]
    return pl.pallas_call(
        _proj_sample_kernel,
        out_shape=out_shape,
        scratch_shapes=[pltpu.VMEM((tm + SUBLANES, c), F32)],
        compiler_params=pltpu.CompilerParams(vmem_limit_bytes=VMEM_LIMIT_BYTES),
        name="proj_sample",
    )(*args)


def _attn_decode_kernel(pt_ref, ckv_hbm, kr_hbm, qa_ref, qr_ref, cnew_ref, krnew_ref, wukt_ref, wuv_ref,
                        gout_ref, y_ref, cbuf, kbuf, sem, *, layer, n_pages, page, t_dec):
    b = pl.program_id(0)
    n_seq = pl.num_programs(0)
    chunk_pages = cbuf.shape[1] // page
    n_chunks = n_pages // chunk_pages
    rows = N_HEADS * t_dec

    def copies(first_page, slot):
        out = []
        for pg in range(chunk_pages):
            pid = 0 if first_page is None else pt_ref[first_page + pg]
            out.append(pltpu.make_async_copy(ckv_hbm.at[layer, pid], cbuf.at[slot, pl.ds(pg * page, page)], sem.at[0, slot]))
            out.append(pltpu.make_async_copy(kr_hbm.at[layer, pid], kbuf.at[slot, pg], sem.at[1, slot]))
        return out

    qa = qa_ref[...].reshape(rows, qa_ref.shape[2]).astype(BF16)
    qr = qr_ref[...].reshape(rows, ROPE_DIM).astype(BF16)

    n_up = wukt_ref.shape[0]
    wq = jnp.concatenate([wukt_ref[...], qa], axis=0)

    def nope_scores(c_b):
        both = lax.dot_general(wq, c_b, _NT, preferred_element_type=F32)
        rs = []
        for hd in range(N_HEADS):
            blk = both[hd * NOPE_DIM:(hd + 1) * NOPE_DIM]
            rs.append(jnp.broadcast_to(_rms_rows(blk), (t_dec, blk.shape[1])))
        return both[n_up:] * jnp.concatenate(rs, axis=0)

    def update(carry, s, c_b):
        m_prev, l_prev, acc = carry
        m_new = jnp.maximum(m_prev, jnp.max(s, axis=-1, keepdims=True))
        alpha = jnp.exp2(m_prev - m_new)
        pr = jnp.exp2(s - m_new)
        l_new = alpha * l_prev + jnp.sum(pr, axis=-1, keepdims=True)
        acc = alpha * acc + jnp.dot(pr.astype(BF16), c_b, preferred_element_type=F32)
        return m_new, l_new, acc

    last = n_seq * n_chunks - 1
    g0 = b * n_chunks

    def fetch(g):
        for cp in copies(jnp.minimum(g, last) * chunk_pages, lax.rem(g, RING)):
            cp.start()

    def arrive(g):
        for cp in copies(None, lax.rem(g, RING)):
            cp.wait()

    def chunk_scores(g):
        slot = lax.rem(g, RING)
        s_rope = jnp.concatenate(
            [jnp.dot(qr, kbuf[slot, pg].astype(BF16), preferred_element_type=F32) for pg in range(chunk_pages)], axis=1)
        return nope_scores(cbuf[slot].astype(BF16)) + s_rope

    def chunk_update(carry, s, g):
        return update(carry, s, cbuf[lax.rem(g, RING)].astype(BF16))

    @pl.when(b == 0)
    def _():
        fetch(0)
        fetch(1)

    arrive(g0)
    s_first = chunk_scores(g0)

    def body(j, state):
        carry, s = state
        g = g0 + j
        fetch(g + 2)
        arrive(g + 1)
        s_next = chunk_scores(g + 1)
        return chunk_update(carry, s, g), s_next

    init = (jnp.full((rows, 1), -jnp.inf, F32), jnp.zeros((rows, 1), F32), jnp.zeros((rows, cbuf.shape[2]), F32))
    carry, s_last = lax.fori_loop(0, n_chunks - 1, body, (init, s_first))
    fetch(g0 + n_chunks + 1)
    carry = chunk_update(carry, s_last, g0 + n_chunks - 1)

    @pl.when(b == n_seq - 1)
    def _():
        arrive(last + 1)
        arrive(last + 2)

    pad = LANES - t_dec
    c_new = jnp.concatenate([cnew_ref[...], jnp.zeros((pad, cnew_ref.shape[1]), F32)], axis=0).astype(BF16)
    kr_new = jnp.concatenate([krnew_ref[...], jnp.zeros((pad, ROPE_DIM), F32)], axis=0).astype(BF16)
    s_new = nope_scores(c_new) + lax.dot_general(qr, kr_new, _NT, preferred_element_type=F32)
    q_t = lax.rem(lax.broadcasted_iota(jnp.int32, (rows, LANES), 0), t_dec)
    key = lax.broadcasted_iota(jnp.int32, (rows, LANES), 1)
    s_new = jnp.where(key <= q_t, s_new, MASK_VALUE)
    _, l_fin, acc = update(carry, s_new, c_new)

    o_lat = (acc / l_fin).astype(BF16)
    ov = jnp.dot(o_lat, wuv_ref[...], preferred_element_type=F32)
    outs = []
    for hd in range(N_HEADS):
        o = ov[hd * t_dec:(hd + 1) * t_dec, hd * V_DIM:(hd + 1) * V_DIM]
        outs.append(o * _rms_lanes(o) * gout_ref[hd])
    y_ref[...] = jnp.concatenate(outs, axis=1)


def _attn_decode(layer, page_table, cache_ckv, cache_kr, qa, qr, c_new, kr_new, w, t_dec):
    n_seq, n_pages = page_table.shape
    _, _, page, kv_rank = cache_ckv.shape
    chunk_pages = min(DECODE_CHUNK_PAGES, n_pages)
    assert n_pages % chunk_pages == 0 and t_dec % SUBLANES == 0 and t_dec <= LANES
    chunk = chunk_pages * page
    grid_spec = pltpu.PrefetchScalarGridSpec(
        num_scalar_prefetch=1,
        grid=(n_seq,),
        in_specs=[
            pl.BlockSpec(memory_space=pl.ANY),
            pl.BlockSpec(memory_space=pl.ANY),
            pl.BlockSpec((N_HEADS, t_dec, kv_rank), lambda b, pt: (0, b, 0)),
            pl.BlockSpec((N_HEADS, t_dec, ROPE_DIM), lambda b, pt: (0, b, 0)),
            pl.BlockSpec((t_dec, kv_rank), lambda b, pt: (b, 0)),
            pl.BlockSpec((t_dec, ROPE_DIM), lambda b, pt: (b, 0)),
            pl.BlockSpec(w["wuk_t"].shape, lambda b, pt: (0, 0)),
            pl.BlockSpec(w["wuv"].shape, lambda b, pt: (0, 0)),
            pl.BlockSpec(w["g_out_a_row"].shape, lambda b, pt: (0, 0, 0)),
        ],
        out_specs=pl.BlockSpec((t_dec, N_HEADS * V_DIM), lambda b, pt: (b, 0)),
        scratch_shapes=[pltpu.VMEM((RING, chunk, kv_rank), F32), pltpu.VMEM((RING, chunk_pages, ROPE_DIM, page), F32),
                        pltpu.SemaphoreType.DMA((2, RING))],
    )
    return pl.pallas_call(
        functools.partial(_attn_decode_kernel, layer=layer, n_pages=n_pages, page=page, t_dec=t_dec),
        grid_spec=grid_spec,
        out_shape=jax.ShapeDtypeStruct((n_seq * t_dec, N_HEADS * V_DIM), F32),
        compiler_params=_params(("arbitrary",)),
        name="attn_decode",
    )(page_table.reshape(-1), cache_ckv, cache_kr, qa, qr, c_new, kr_new, w["wuk_t"], w["wuv"], w["g_out_a_row"])


ROUTE_E0, ROUTE_E1, ROUTE_R0, ROUTE_R1, ROUTE_W0, ROUTE_W1 = range(6)


def _lane_pick(x, lane, idx):
    return jnp.sum(jnp.where(lane == idx, x, 0.0), axis=-1, keepdims=True)


def _store_row_tiles(ref, x):
    rows, width = x.shape
    n = width // LANES
    for j in range(n):
        ref[pl.ds(j, rows, stride=n), :] = x[:, j * LANES:(j + 1) * LANES]


def _load_row_tiles(ref, rows, n):
    return jnp.concatenate([ref[pl.ds(j, rows, stride=n), :] for j in range(n)], axis=1)


def _merge_route_kernel(ycp_ref, yap_ref, xp_ref, ycs_ref, yas_ref, xs_ref, wout_ref, gffn_ref, wr_ref, br_ref,
                        xmid_ref, h2_ref, route_ref, counts_ref, carry_ref, *, n_prompt_tiles):
    i = pl.program_id(0)
    tm = xp_ref.shape[0]
    is_p = i < n_prompt_tiles

    @pl.when(i == 0)
    def _():
        carry_ref[...] = jnp.zeros(carry_ref.shape, F32)

    yc = jnp.where(is_p, ycp_ref[...], ycs_ref[...])
    ya = jnp.where(is_p, yap_ref[...], yas_ref[...].astype(BF16))
    x = jnp.where(is_p, xp_ref[...], xs_ref[...])
    y = jnp.concatenate([yc, ya], axis=1)
    xm = x + jnp.dot(y, wout_ref[...], preferred_element_type=F32)
    xmid_ref[...] = xm
    h2f = xm * _rms_lanes(xm) * gffn_ref[...]
    _store_row_tiles(h2_ref, h2f)
    h2 = h2f.astype(BF16)
    logits = jnp.dot(h2, wr_ref[...], preferred_element_type=F32) + br_ref[...]

    lane_i = lax.broadcasted_iota(jnp.int32, (tm, LANES), 1)
    lane = lane_i.astype(F32)
    neg = -jnp.inf
    far = float(LANES)
    in_groups = lane_i < N_GROUPS
    gl = jnp.where(in_groups, logits, neg)
    ge = jnp.exp(gl - jnp.max(gl, axis=-1, keepdims=True))
    pg = ge / jnp.sum(ge, axis=-1, keepdims=True)
    p_sel = jnp.max(pg, axis=-1, keepdims=True)
    g_sel = jnp.min(jnp.where((pg == p_sel) & in_groups, lane, far), axis=-1, keepdims=True)
    lo = N_GROUPS + g_sel * EXPERTS_PER_GROUP
    el = jnp.where((lane >= lo) & (lane < lo + EXPERTS_PER_GROUP), logits, neg)
    v1 = jnp.max(el, axis=-1, keepdims=True)
    i1 = jnp.min(jnp.where(el == v1, lane, far), axis=-1, keepdims=True)
    el2 = jnp.where(lane == i1, neg, el)
    v2 = jnp.max(el2, axis=-1, keepdims=True)
    i2 = jnp.min(jnp.where(el2 == v2, lane, far), axis=-1, keepdims=True)
    e2 = jnp.exp(v2 - v1)
    w0 = 1.0 / (1.0 + e2) * p_sel
    w1 = e2 / (1.0 + e2) * p_sel
    e0 = i1 - N_GROUPS
    e1 = i2 - N_GROUPS

    oh0 = lane == e0
    oh1 = lane == e1
    onehot = (oh0 | oh1).astype(BF16)
    tri = (lax.broadcasted_iota(jnp.int32, (tm, tm), 0) > lax.broadcasted_iota(jnp.int32, (tm, tm), 1)).astype(BF16)
    before = jnp.dot(tri, onehot, preferred_element_type=F32) + carry_ref[...]
    r0 = jnp.sum(jnp.where(oh0, before, 0.0), axis=-1, keepdims=True)
    r1 = jnp.sum(jnp.where(oh1, before, 0.0), axis=-1, keepdims=True)
    carry_ref[...] += jnp.sum(onehot.astype(F32), axis=0, keepdims=True)

    route = jnp.zeros((tm, LANES), F32)
    for idx, val in ((ROUTE_E0, e0), (ROUTE_E1, e1), (ROUTE_R0, r0), (ROUTE_R1, r1), (ROUTE_W0, w0), (ROUTE_W1, w1)):
        route = jnp.where(lane_i == idx, val, route)
    route_ref[...] = route.T[:SUBLANES]

    @pl.when(i == pl.num_programs(0) - 1)
    def _():
        counts_ref[...] = carry_ref[...]


def _merge_route(ycp, yap, xp, ycs, yas, xs, w):
    tp, d = xp.shape
    ts = xs.shape[0]
    tm = TOKEN_TILE
    assert tp % tm == 0 and ts % tm == 0
    npt, nst = tp // tm, ts // tm
    half = ycp.shape[1]
    pmap = lambda i: (jnp.minimum(i, npt - 1), 0)
    smap = lambda i: (jnp.maximum(i - npt, 0), 0)
    cmap = lambda i: (0, 0)
    return pl.pallas_call(
        functools.partial(_merge_route_kernel, n_prompt_tiles=npt),
        grid=(npt + nst,),
        in_specs=[
            pl.BlockSpec((tm, half), pmap), pl.BlockSpec((tm, half), pmap), pl.BlockSpec((tm, d), pmap),
            pl.BlockSpec((tm, half), smap), pl.BlockSpec((tm, half), smap), pl.BlockSpec((tm, d), smap),
            pl.BlockSpec(w["w_out"].shape, cmap), pl.BlockSpec(w["g_ffn"].shape, cmap),
            pl.BlockSpec(w["w_r"].shape, cmap), pl.BlockSpec(w["b_r"].shape, cmap),
        ],
        out_specs=[pl.BlockSpec((tm, d), lambda i: (i, 0)),
                   pl.BlockSpec((tm * (d // LANES), LANES), lambda i: (i, 0)),
                   pl.BlockSpec((SUBLANES, tm), lambda i: (0, i)),
                   pl.BlockSpec((1, LANES), cmap)],
        out_shape=[jax.ShapeDtypeStruct((tp + ts, d), F32),
                   jax.ShapeDtypeStruct(((tp + ts) * (d // LANES), LANES), F32),
                   jax.ShapeDtypeStruct((SUBLANES, tp + ts), F32),
                   jax.ShapeDtypeStruct((1, LANES), F32)],
        scratch_shapes=[pltpu.VMEM((1, LANES), F32)],
        compiler_params=_params(("arbitrary",)),
        name="merge_route",
    )(ycp, yap, xp, ycs, yas, xs, w["w_out"], w["g_ffn"], w["w_r"], w["b_r"])


def _tile_of(ref, row, n):
    start = row * n
    return ref.at[pl.ds(start if isinstance(row, int) else pl.multiple_of(start, n), n)]


def _pad_fill_copies(base_ref, cnt_ref, tiles_ref, nu_ref, zeros_ref, xs_hbm, sem, tme, n, n_tiles):
    out = []
    for e in range(N_EXPERTS):
        pad = tiles_ref[e] * tme - cnt_ref[e]
        pos = base_ref[e] + cnt_ref[e]
        bit = tme // 2
        while bit >= 1:
            take = pad & bit
            out.append((take != 0, pltpu.make_async_copy(
                zeros_ref.at[pl.ds(0, bit * n)], xs_hbm.at[pl.ds(pl.multiple_of(pos * n, n), bit * n)], sem)))
            pos = pos + take
            bit //= 2
    for k in range(N_EXPERTS + 1):
        tile = nu_ref[0] + k
        out.append((tile < n_tiles, pltpu.make_async_copy(
            zeros_ref, xs_hbm.at[pl.ds(pl.multiple_of(jnp.minimum(tile, n_tiles - 1) * (tme * n), tme * n), tme * n)],
            sem)))
    return out


def _dispatch_kernel(base_ref, cnt_ref, tiles_ref, nu_ref, s0_ref, s1_ref, h2_ref, xs_hbm, zeros, sem,
                     *, tm, tme, n, n_tiles):
    i = pl.program_id(0)
    for t0 in range(0, tm, SLOT_BATCH):
        slots = [(s0_ref[0, 0, t], s1_ref[0, 0, t]) for t in range(t0, t0 + SLOT_BATCH)]
        for t, pair in zip(range(t0, t0 + SLOT_BATCH), slots):
            for k, s in enumerate(pair):
                pltpu.make_async_copy(_tile_of(h2_ref, t, n), _tile_of(xs_hbm, s, n), sem.at[0]).start(priority=k)

    def fills():
        return _pad_fill_copies(base_ref, cnt_ref, tiles_ref, nu_ref, zeros, xs_hbm, sem.at[1], tme, n, n_tiles)

    @pl.when(i == 0)
    def _():
        zeros[...] = jnp.zeros(zeros.shape, F32)
        for pred, cp in fills():
            pl.when(pred)(cp.start)

    for t in range(2 * tm):
        pltpu.make_async_copy(_tile_of(h2_ref, 0, n), _tile_of(xs_hbm, 0, n), sem.at[0]).wait()

    @pl.when(i == 0)
    def _():
        for pred, cp in fills():
            pl.when(pred)(cp.wait)


def _dispatch(plan, h2_tiles, n_slots):
    base, cnt, tiles, n_used, slot0, slot1 = plan
    n_tok_tiles, _, tm = slot0.shape
    n = h2_tiles.shape[0] // (n_tok_tiles * tm)
    tme = EXPERT_TILE
    n_tiles = n_slots // tme
    smem_blk = pl.BlockSpec((1, 1, tm), lambda i, *_: (i, 0, 0), memory_space=pltpu.SMEM)
    grid_spec = pltpu.PrefetchScalarGridSpec(
        num_scalar_prefetch=4,
        grid=(n_tok_tiles,),
        in_specs=[smem_blk] * 2 + [pl.BlockSpec((tm * n, LANES), lambda i, *_: (i, 0))],
        out_specs=pl.BlockSpec(memory_space=pl.ANY),
        scratch_shapes=[pltpu.VMEM((tme * n, LANES), F32), pltpu.SemaphoreType.DMA((2,))],
    )
    return pl.pallas_call(
        functools.partial(_dispatch_kernel, tm=tm, tme=tme, n=n, n_tiles=n_tiles),
        grid_spec=grid_spec,
        out_shape=jax.ShapeDtypeStruct((n_slots * n, LANES), F32),
        compiler_params=_params(("arbitrary",)),
        name="dispatch",
    )(base, cnt, tiles, n_used, slot0, slot1, h2_tiles)


def _experts_kernel(te_ref, nu_ref, xs_ref, wg_ref, wu_ref, wd_ref, ys_ref, wgu_bf, wd_bf):
    i = pl.program_id(0)
    n_used = nu_ref[0]
    d_exp = wd_ref.shape[2]
    d = wd_ref.shape[3]
    n = d // LANES
    tme = xs_ref.shape[0] // n
    new_expert = (i == 0) | (te_ref[i] != te_ref[jnp.maximum(i - 1, 0)])

    @pl.when(new_expert & (i < n_used))
    def _():
        wgu_bf[:, :d_exp] = wg_ref[0, 0].astype(BF16)
        wgu_bf[:, d_exp:] = wu_ref[0, 0].astype(BF16)
        wd_bf[...] = wd_ref[0, 0].astype(BF16)

    @pl.when(i < n_used)
    def _():
        h2 = _load_row_tiles(xs_ref, tme, n).astype(BF16)
        gu = jnp.dot(h2, wgu_bf[...], preferred_element_type=F32)
        g = gu[:, :d_exp]
        a = (g / (1.0 + jnp.exp(-g))) * gu[:, d_exp:]
        _store_row_tiles(ys_ref, jnp.dot(a.astype(BF16), wd_bf[...], preferred_element_type=F32))

    @pl.when(i >= n_used)
    def _():
        ys_ref[...] = jnp.zeros(ys_ref.shape, F32)


def _experts(layer, tile_expert, n_used, xs_tiles, w_gate, w_up, w_down):
    n_tiles = tile_expert.shape[0]
    _, _, d, d_exp = w_gate.shape
    blk = xs_tiles.shape[0] // n_tiles
    grid_spec = pltpu.PrefetchScalarGridSpec(
        num_scalar_prefetch=2,
        grid=(n_tiles,),
        in_specs=[
            pl.BlockSpec((blk, LANES), lambda i, te, nu: (jnp.minimum(i, nu[0] - 1), 0)),
            pl.BlockSpec((1, 1, d, d_exp), lambda i, te, nu: (layer, te[i], 0, 0)),
            pl.BlockSpec((1, 1, d, d_exp), lambda i, te, nu: (layer, te[i], 0, 0)),
            pl.BlockSpec((1, 1, d_exp, d), lambda i, te, nu: (layer, te[i], 0, 0)),
        ],
        out_specs=pl.BlockSpec((blk, LANES), lambda i, te, nu: (i, 0)),
        scratch_shapes=[pltpu.VMEM((d, 2 * d_exp), BF16), pltpu.VMEM((d_exp, d), BF16)],
    )
    return pl.pallas_call(
        _experts_kernel,
        grid_spec=grid_spec,
        out_shape=jax.ShapeDtypeStruct(xs_tiles.shape, F32),
        compiler_params=_params(("arbitrary",)),
        name="experts",
    )(tile_expert, n_used, xs_tiles, w_gate, w_up, w_down)


def _combine_kernel(s0_ref, s1_ref, s0n_ref, s1n_ref, ys_hbm, xmid_ref, route_ref, yp_ref, ysmp_ref, buf, sem,
                    *, n_prompt_tiles):
    i = pl.program_id(0)
    last = pl.num_programs(0) - 1
    tm, d = xmid_ref.shape
    n = d // LANES
    slot = lax.rem(i, 2)

    def gather(refs, dst_slot):
        s0, s1 = refs
        for t0 in range(0, tm, SLOT_BATCH):
            slots = [(s0[0, 0, t], s1[0, 0, t]) for t in range(t0, t0 + SLOT_BATCH)]
            for t, pair in zip(range(t0, t0 + SLOT_BATCH), slots):
                for k, s in enumerate(pair):
                    pltpu.make_async_copy(_tile_of(ys_hbm, s, n), _tile_of(buf.at[dst_slot, k], t, n),
                                          sem.at[dst_slot]).start(priority=k)

    def gather_wait(dst_slot):
        for t in range(2 * tm):
            pltpu.make_async_copy(_tile_of(ys_hbm, 0, n), _tile_of(buf.at[dst_slot, 0], 0, n), sem.at[dst_slot]).wait()

    @pl.when(i == 0)
    def _():
        gather((s0_ref, s1_ref), 0)

    gather((s0n_ref, s1n_ref), 1 - slot)
    gather_wait(slot)

    rt = route_ref[...]
    cols = jnp.concatenate([rt, jnp.zeros((LANES - rt.shape[0], tm), F32)], axis=0).T
    lane = lax.broadcasted_iota(jnp.int32, cols.shape, 1)
    w0 = _lane_pick(cols, lane, ROUTE_W0)
    w1 = _lane_pick(cols, lane, ROUTE_W1)
    y0 = _load_row_tiles(buf.at[slot, 0], tm, n)
    y1 = _load_row_tiles(buf.at[slot, 1], tm, n)
    out = xmid_ref[...] + (w0 * y0 + w1 * y1)

    @pl.when(i < n_prompt_tiles)
    def _():
        yp_ref[...] = out

    @pl.when(i >= n_prompt_tiles)
    def _():
        ysmp_ref[...] = out

    @pl.when(i == last)
    def _():
        gather_wait(1 - slot)


def _combine(plan, ys_tiles, xmid, route_t, tp):
    slot0, slot1 = plan[4:]
    ttot, d = xmid.shape
    n_tok_tiles, _, tm = slot0.shape
    npt = tp // tm
    n = d // LANES
    cur = pl.BlockSpec((1, 1, tm), lambda i: (i, 0, 0), memory_space=pltpu.SMEM)
    nxt = pl.BlockSpec((1, 1, tm), lambda i: (jnp.minimum(i + 1, n_tok_tiles - 1), 0, 0), memory_space=pltpu.SMEM)
    return pl.pallas_call(
        functools.partial(_combine_kernel, n_prompt_tiles=npt),
        grid=(n_tok_tiles,),
        in_specs=[cur, cur, nxt, nxt,
                  pl.BlockSpec(memory_space=pl.ANY),
                  pl.BlockSpec((tm, d), lambda i: (i, 0)),
                  pl.BlockSpec((SUBLANES, tm), lambda i: (0, i))],
        out_specs=[pl.BlockSpec((tm, d), lambda i: (jnp.minimum(i, npt - 1), 0)),
                   pl.BlockSpec((tm, d), lambda i: (jnp.maximum(i - npt, 0), 0))],
        out_shape=[jax.ShapeDtypeStruct((tp, d), F32), jax.ShapeDtypeStruct((ttot - tp, d), F32)],
        scratch_shapes=[pltpu.VMEM((2, 2, tm * n, LANES), F32), pltpu.SemaphoreType.DMA((2,))],
        compiler_params=_params(("arbitrary",)),
        name="combine",
    )(slot0, slot1, slot0, slot1, ys_tiles, xmid, route_t)


def _rope_tables(pos):
    inv_freq = ROPE_BASE ** (-jnp.arange(HALF_ROPE, dtype=F32) / HALF_ROPE)
    ang = pos.astype(F32)[:, None] * inv_freq[None, :]
    return jnp.cos(ang).T, jnp.sin(ang).T


def _layer_weights(l, g_mix, w_in, conv_w, conv_b, g_q_lat, w_uq, g_kv_lat, w_uk, w_uv, g_q_nope, g_q_rope,
                   g_k_nope, g_k_rope, g_out, w_out, g_ffn, w_router_group, b_router_group, w_router_expert,
                   b_router_expert, w_gate, w_up, w_down):
    c = conv_w.shape[2]
    col = lambda g: g[l].reshape(-1, 1)
    w_r = jnp.concatenate([w_router_group[l], w_router_expert[l]], axis=1)
    b_r = jnp.concatenate([b_router_group[l], b_router_expert[l]])
    return {
        "g_mix": g_mix[l].reshape(1, -1),
        "win_c": w_in[l][:, :3 * c].astype(BF16),
        "win_at": w_in[l][:, 3 * c:].T.astype(BF16),
        "conv_w": conv_w[l],
        "conv_b": conv_b[l].reshape(1, -1),
        "g_out_c": g_out[l][:c].reshape(1, -1),
        "g_out_a": g_out[l][c:].reshape(N_HEADS, V_DIM, 1),
        "g_out_a_row": g_out[l][c:].reshape(N_HEADS, 1, V_DIM),
        "g_qlat": col(g_q_lat), "g_kv": col(g_kv_lat), "g_qn": col(g_q_nope), "g_qr": col(g_q_rope),
        "g_kn": col(g_k_nope), "g_kr": col(g_k_rope),
        "wuq_t": w_uq[l].T.astype(BF16),
        "wuk_t": w_uk[l].T.astype(BF16),
        "wuk": w_uk[l].astype(BF16),
        "wuv_t": w_uv[l].T.astype(BF16),
        "wuv": w_uv[l].astype(BF16),
        "w_out": w_out[l].astype(BF16),
        "g_ffn": g_ffn[l].reshape(1, -1),
        "w_r": jnp.pad(w_r, ((0, 0), (0, LANES - w_r.shape[1]))).astype(BF16),
        "b_r": jnp.pad(b_r, (0, LANES - b_r.shape[0])).reshape(1, -1),
    }


def _moe_plan(route_t, counts):
    ttot = route_t.shape[1]
    tme = EXPERT_TILE
    tm = TOKEN_TILE
    n_tiles = (TOP_K * ttot) // tme + N_EXPERTS + 1
    cnt = counts[0, :N_EXPERTS].astype(jnp.int32)
    tiles = (cnt + tme - 1) // tme
    tile_end = jnp.cumsum(tiles)
    base = (tile_end - tiles) * tme
    n_used = tile_end[-1:]
    tile_id = jnp.minimum(jnp.arange(n_tiles, dtype=jnp.int32), n_used[0] - 1)
    tile_expert = jnp.sum((tile_end[None, :] <= tile_id[:, None]).astype(jnp.int32), axis=1)
    ids = route_t[:ROUTE_R1 + 1].astype(jnp.int32)
    expert_ids = jnp.arange(N_EXPERTS, dtype=jnp.int32)[:, None]

    def slots(e, r):
        s = r + jnp.sum(jnp.where(e[None, :] == expert_ids, base[:, None], 0), axis=0)
        return s.reshape(ttot // tm, 1, tm)

    plan = (base, cnt, tiles, n_used, slots(ids[ROUTE_E0], ids[ROUTE_R0]), slots(ids[ROUTE_E1], ids[ROUTE_R1]))
    return plan, tile_expert, n_tiles * tme


def kernel(x_prompt, x_sample, state_conv, cache_ckv, cache_krope, page_table, g_mix, w_in, conv_w, conv_b, g_q_lat, w_uq, g_kv_lat, w_uk, w_uv, g_q_nope, g_q_rope, g_k_nope, g_k_rope, g_out, w_out, g_ffn, w_router_group, b_router_group, w_router_expert, b_router_expert, w_gate, w_up, w_down):
    b_p, s_p, d = x_prompt.shape
    b_s, t_s, _ = x_sample.shape
    depth = g_mix.shape[0]
    c = conv_w.shape[2]
    page = cache_ckv.shape[2]
    past_len = page_table.shape[1] * page
    kv_rank = cache_ckv.shape[3]

    cos_p, sin_p = _rope_tables(jnp.arange(s_p, dtype=jnp.int32))
    cos_s, sin_s = _rope_tables(jnp.tile(past_len + jnp.arange(t_s, dtype=jnp.int32), b_s))
    tpos = jnp.tile(jnp.arange(t_s, dtype=jnp.int32), b_s).reshape(-1, 1)

    krope_pages = jnp.swapaxes(cache_krope, 2, 3)

    xp, xs = x_prompt, x_sample.reshape(b_s * t_s, d)
    outs = [[] for _ in range(6)]
    for l in range(depth):
        w = _layer_weights(l, g_mix, w_in, conv_w, conv_b, g_q_lat, w_uq, g_kv_lat, w_uk, w_uv, g_q_nope,
                           g_q_rope, g_k_nope, g_k_rope, g_out, w_out, g_ffn, w_router_group, b_router_group,
                           w_router_expert, b_router_expert, w_gate, w_up, w_down)
        yconv_p, qt, kt, vt, ckv_p, kr_p, conv_p = _proj_prompt(xp, w, cos_p, sin_p)
        yattn_p = _attn_prompt(qt, kt, vt, w["g_out_a"])
        st = state_conv[l]
        zeros = lambda n: jnp.zeros((b_s, n, c), F32)
        st1 = jnp.concatenate([st[:, CONV_K - 2:], zeros(t_s - 1)], axis=1).reshape(b_s * t_s, c)
        st2 = jnp.concatenate([st, zeros(t_s - (CONV_K - 1))], axis=1).reshape(b_s * t_s, c)
        yconv_s, u_s, qa, qr, ckv_s, kr_s = _proj_sample(xs, tpos, st1, st2, w, cos_s, sin_s)
        yattn_s = _attn_decode(l, page_table, cache_ckv, krope_pages, qa, qr, ckv_s, kr_s, w, t_s)
        xmid, h2_tiles, route_t, counts = _merge_route(
            yconv_p.reshape(b_p * s_p, c), yattn_p.reshape(b_p * s_p, -1), xp.reshape(b_p * s_p, d),
            yconv_s, yattn_s, xs, w)
        plan, tile_expert, n_slots = _moe_plan(route_t, counts)
        xs_tiles = _dispatch(plan, h2_tiles, n_slots)
        ys_tiles = _experts(l, tile_expert, plan[3], xs_tiles, w_gate, w_up, w_down)
        yp, ysmp = _combine(plan, ys_tiles, xmid, route_t, b_p * s_p)
        xp, xs = yp.reshape(b_p, s_p, d), ysmp
        for lst, val in zip(outs, (ckv_p, kr_p, conv_p, ckv_s.reshape(b_s, t_s, kv_rank),
                                   kr_s.reshape(b_s, t_s, ROPE_DIM),
                                   u_s.reshape(b_s, t_s, c)[:, t_s - (CONV_K - 1):])):
            lst.append(val)
    return (xp, xs.reshape(b_s, t_s, d)) + tuple(jnp.stack(o) for o in outs)
```

```python
import functools

import jax
import jax.numpy as jnp
from jax import lax
from jax.experimental import pallas as pl
from jax.experimental.pallas import tpu as pltpu

N_HEADS = 8
NOPE_DIM = 64
ROPE_DIM = 32
V_DIM = 64
HEAD_QK = NOPE_DIM + ROPE_DIM
HALF_ROPE = ROPE_DIM // 2
ROPE_BASE = 10000.0
CONV_K = 3
OUT_GROUP_DIM = 64
N_GROUPS = 4
EXPERTS_PER_GROUP = 8
N_EXPERTS = N_GROUPS * EXPERTS_PER_GROUP
TOP_K = 2
EPS = 1e-6
MASK_VALUE = -1e30
LOG2_E = 1.4426950408889634

LANES = 128
SUBLANES = 8
VMEM_LIMIT_BYTES = 48 * 1024 * 1024

PROJ_TILE = 512
ATTN_TILE = 512
TOKEN_TILE = 256
EXPERT_TILE = 512
DECODE_CHUNK_PAGES = 8
RING = 4
SLOT_BATCH = 8

F32 = jnp.float32
BF16 = jnp.bfloat16

_NT = (((1,), (1,)), ((), ()))
_TN = (((0,), (0,)), ((), ()))


def _params(sem):
    return pltpu.CompilerParams(dimension_semantics=sem, vmem_limit_bytes=VMEM_LIMIT_BYTES)


def _rms_rows(x):
    return lax.rsqrt(jnp.mean(x * x, axis=0, keepdims=True) + EPS)


def _rms_lanes(x):
    return lax.rsqrt(jnp.mean(x * x, axis=-1, keepdims=True) + EPS)


def _group_norm_lanes(y, gain):
    lane = lax.broadcasted_iota(jnp.int32, (1, LANES), 1)
    low = lane < OUT_GROUP_DIM
    outs = []
    for j in range(y.shape[1] // LANES):
        t = y[:, j * LANES:(j + 1) * LANES]
        sq = t * t
        ss_lo = jnp.sum(jnp.where(low, sq, 0.0), axis=-1, keepdims=True)
        ss_hi = jnp.sum(jnp.where(low, 0.0, sq), axis=-1, keepdims=True)
        r = jnp.where(low, lax.rsqrt(ss_lo / OUT_GROUP_DIM + EPS), lax.rsqrt(ss_hi / OUT_GROUP_DIM + EPS))
        outs.append(t * r)
    return jnp.concatenate(outs, axis=1) * gain


def _rope_rows(x, cos, sin):
    x1, x2 = x[:HALF_ROPE], x[HALF_ROPE:]
    return x1 * cos - x2 * sin, x1 * sin + x2 * cos


def _attention_side(h_bf16, win_at_ref, gqlat_ref, wuqt_ref, gkv_ref, gqn_ref, gqr_ref, gkr_ref, cos, sin):
    q_rank = gqlat_ref.shape[0]
    kv_rank = gkv_ref.shape[0]
    scale = HEAD_QK ** -0.5 * LOG2_E
    zat = lax.dot_general(win_at_ref[...], h_bf16, _NT, preferred_element_type=F32)
    qlt = zat[:q_rank]
    kvt = zat[q_rank:q_rank + kv_rank]
    krt = zat[q_rank + kv_rank:]
    qln = (qlt * _rms_rows(qlt) * gqlat_ref[...]).astype(BF16)
    qt = jnp.dot(wuqt_ref[...], qln, preferred_element_type=F32)
    q_nope, q_rope = [], []
    for h in range(N_HEADS):
        nope = qt[h * HEAD_QK:h * HEAD_QK + NOPE_DIM]
        rope = qt[h * HEAD_QK + NOPE_DIM:(h + 1) * HEAD_QK]
        q_nope.append(nope * _rms_rows(nope) * gqn_ref[...] * scale)
        r1, r2 = _rope_rows(rope * _rms_rows(rope) * gqr_ref[...], cos, sin)
        q_rope.append((r1 * scale, r2 * scale))
    ckvt = kvt * _rms_rows(kvt) * gkv_ref[...]
    k1, k2 = _rope_rows(krt * _rms_rows(krt) * gkr_ref[...], cos, sin)
    return q_nope, q_rope, ckvt, (k1, k2)


def _to_token_major(xt, width):
    rows, toks = xt.shape
    if rows < LANES:
        xt = jnp.concatenate([xt, jnp.zeros((LANES - rows, toks), F32)], axis=0)
    return xt.T[:, :width]


def _proj_prompt_kernel(x_ref, gmix_ref, win_c_ref, win_at_ref, convw_ref, convb_ref, gout_c_ref,
                        gqlat_ref, wuqt_ref, gkv_ref, gqn_ref, gqr_ref, gkn_ref, gkr_ref,
                        wukt_ref, wuvt_ref, cos_ref, sin_ref,
                        yconv_ref, qt_ref, kt_ref, vt_ref, ckv_ref, krope_ref, convst_ref,
                        ext_ref):
    si = pl.program_id(1)
    tm = x_ref.shape[1]
    c = convw_ref.shape[1]

    xf = x_ref[0]
    h = (xf * _rms_lanes(xf) * gmix_ref[...]).astype(BF16)

    zc = jnp.dot(h, win_c_ref[...], preferred_element_type=F32)
    u = zc[:, 2 * c:] * zc[:, :c]

    @pl.when(si == 0)
    def _():
        ext_ref[0:SUBLANES, :] = jnp.zeros((SUBLANES, c), F32)

    ext_ref[SUBLANES:, :] = u
    v = (convb_ref[...]
         + convw_ref[0:1, :] * ext_ref[pl.ds(SUBLANES - 2, tm), :]
         + convw_ref[1:2, :] * ext_ref[pl.ds(SUBLANES - 1, tm), :]
         + convw_ref[2:3, :] * u)
    yconv = zc[:, c:2 * c] * v
    yconv_ref[0] = _group_norm_lanes(yconv, gout_c_ref[...]).astype(BF16)
    ext_ref[0:SUBLANES, :] = ext_ref[pl.ds(tm, SUBLANES), :]
    convst_ref[0] = ext_ref[pl.ds(SUBLANES - (CONV_K - 1), CONV_K - 1), :]

    q_nope, q_rope, ckvt, (k1, k2) = _attention_side(
        h, win_at_ref, gqlat_ref, wuqt_ref, gkv_ref, gqn_ref, gqr_ref, gkr_ref, cos_ref[...], sin_ref[...])
    for hd in range(N_HEADS):
        qt_ref[0, hd, 0:NOPE_DIM, :] = q_nope[hd].astype(BF16)
        qt_ref[0, hd, NOPE_DIM:NOPE_DIM + HALF_ROPE, :] = q_rope[hd][0].astype(BF16)
        qt_ref[0, hd, NOPE_DIM + HALF_ROPE:HEAD_QK, :] = q_rope[hd][1].astype(BF16)
    ckv_ref[0] = ckvt.T
    krt = jnp.concatenate([k1, k2], axis=0)
    krope_ref[0] = _to_token_major(krt, ROPE_DIM)
    ckv_b = ckvt.astype(BF16)
    ktn = jnp.dot(wukt_ref[...], ckv_b, preferred_element_type=F32)
    vt = jnp.dot(wuvt_ref[...], ckv_b, preferred_element_type=F32)
    krt_b = krt.astype(BF16)
    for hd in range(N_HEADS):
        blk = ktn[hd * NOPE_DIM:(hd + 1) * NOPE_DIM]
        kt_ref[0, hd, 0:NOPE_DIM, :] = (blk * _rms_rows(blk) * gkn_ref[...]).astype(BF16)
        kt_ref[0, hd, NOPE_DIM:HEAD_QK, :] = krt_b
        vt_ref[0, hd] = vt[hd * V_DIM:(hd + 1) * V_DIM].astype(BF16)


def _proj_prompt(x, w, cos_t, sin_t):
    b, s, d = x.shape
    tm = min(PROJ_TILE, s)
    assert s % tm == 0
    c = w["conv_w"].shape[1]
    kv_rank = w["g_kv"].shape[0]
    full = lambda a: pl.BlockSpec(a.shape, lambda bi, si: (0,) * a.ndim)
    weights = [w["g_mix"], w["win_c"], w["win_at"], w["conv_w"], w["conv_b"], w["g_out_c"],
               w["g_qlat"], w["wuq_t"], w["g_kv"], w["g_qn"], w["g_qr"], w["g_kn"], w["g_kr"],
               w["wuk_t"], w["wuv_t"]]
    in_specs = ([pl.BlockSpec((1, tm, d), lambda bi, si: (bi, si, 0))] + [full(a) for a in weights]
                + [pl.BlockSpec((HALF_ROPE, tm), lambda bi, si: (0, si))] * 2)
    out_shape = [
        jax.ShapeDtypeStruct((b, s, c), BF16),
        jax.ShapeDtypeStruct((b, N_HEADS, HEAD_QK, s), BF16),
        jax.ShapeDtypeStruct((b, N_HEADS, HEAD_QK, s), BF16),
        jax.ShapeDtypeStruct((b, N_HEADS, V_DIM, s), BF16),
        jax.ShapeDtypeStruct((b, s, kv_rank), F32),
        jax.ShapeDtypeStruct((b, s, ROPE_DIM), F32),
        jax.ShapeDtypeStruct((b, CONV_K - 1, c), F32),
    ]
    out_specs = [
        pl.BlockSpec((1, tm, c), lambda bi, si: (bi, si, 0)),
        pl.BlockSpec((1, N_HEADS, HEAD_QK, tm), lambda bi, si: (bi, 0, 0, si)),
        pl.BlockSpec((1, N_HEADS, HEAD_QK, tm), lambda bi, si: (bi, 0, 0, si)),
        pl.BlockSpec((1, N_HEADS, V_DIM, tm), lambda bi, si: (bi, 0, 0, si)),
        pl.BlockSpec((1, tm, kv_rank), lambda bi, si: (bi, si, 0)),
        pl.BlockSpec((1, tm, ROPE_DIM), lambda bi, si: (bi, si, 0)),
        pl.BlockSpec((1, CONV_K - 1, c), lambda bi, si: (bi, 0, 0)),
    ]
    return pl.pallas_call(
        _proj_prompt_kernel,
        grid=(b, s // tm),
        in_specs=in_specs,
        out_specs=out_specs,
        out_shape=out_shape,
        scratch_shapes=[pltpu.VMEM((tm + SUBLANES, c), F32)],
        compiler_params=_params(("arbitrary", "arbitrary")),
        name="proj_prompt",
    )(x, *weights, cos_t, sin_t)


def _attn_prompt_kernel(qi_ref, ki_ref, qt_ref, kt_ref, vt_ref, gout_ref, y_ref, m_ref, l_ref, acc_ref):
    p = pl.program_id(1)
    qi = qi_ref[p]
    ki = ki_ref[p]
    tq = qt_ref.shape[3]
    tk = kt_ref.shape[3]

    @pl.when(ki == 0)
    def _():
        m_ref[...] = jnp.full(m_ref.shape, -jnp.inf, F32)
        l_ref[...] = jnp.zeros(l_ref.shape, F32)
        acc_ref[...] = jnp.zeros(acc_ref.shape, F32)

    def block(masked):
        if masked:
            visible = (lax.broadcasted_iota(jnp.int32, (tk, tq), 0) <= lax.broadcasted_iota(jnp.int32, (tk, tq), 1))
        def logits(hd):
            return lax.dot_general(kt_ref[0, hd], qt_ref[0, hd], _TN, preferred_element_type=F32)

        s_next = logits(0)
        for hd in range(N_HEADS):
            s = s_next
            if hd + 1 < N_HEADS:
                s_next = logits(hd + 1)
            if masked:
                s = jnp.where(visible, s, MASK_VALUE)
            m_prev = m_ref[hd]
            m_new = jnp.maximum(m_prev, jnp.max(s, axis=0, keepdims=True))
            alpha = jnp.exp2(m_prev - m_new)
            pr = jnp.exp2(s - m_new)
            l_ref[hd] = alpha * l_ref[hd] + jnp.sum(pr, axis=0, keepdims=True)
            pv = jnp.dot(vt_ref[0, hd], pr.astype(BF16), preferred_element_type=F32)
            acc_ref[hd] = alpha * acc_ref[hd] + pv
            m_ref[hd] = m_new

    @pl.when(ki < qi)
    def _():
        block(False)

    @pl.when(ki == qi)
    def _():
        block(True)
        outs = []
        for hd in range(N_HEADS):
            o = acc_ref[hd] / l_ref[hd]
            outs.append(o * _rms_rows(o) * gout_ref[hd])
        y_ref[0] = jnp.concatenate(outs, axis=0).T.astype(BF16)


def _attn_prompt(qt, kt, vt, gout_a):
    b, _, _, s = qt.shape
    t = min(ATTN_TILE, s)
    assert s % t == 0
    n = s // t
    pairs = [(i, j) for i in range(n) for j in range(i + 1)]
    qi_tab = jnp.asarray([p[0] for p in pairs], jnp.int32)
    ki_tab = jnp.asarray([p[1] for p in pairs], jnp.int32)
    grid_spec = pltpu.PrefetchScalarGridSpec(
        num_scalar_prefetch=2,
        grid=(b, len(pairs)),
        in_specs=[
            pl.BlockSpec((1, N_HEADS, HEAD_QK, t), lambda bi, p, qi, ki: (bi, 0, 0, qi[p])),
            pl.BlockSpec((1, N_HEADS, HEAD_QK, t), lambda bi, p, qi, ki: (bi, 0, 0, ki[p])),
            pl.BlockSpec((1, N_HEADS, V_DIM, t), lambda bi, p, qi, ki: (bi, 0, 0, ki[p])),
            pl.BlockSpec(gout_a.shape, lambda bi, p, qi, ki: (0, 0, 0)),
        ],
        out_specs=pl.BlockSpec((1, t, N_HEADS * V_DIM), lambda bi, p, qi, ki: (bi, qi[p], 0)),
        scratch_shapes=[pltpu.VMEM((N_HEADS, 1, t), F32), pltpu.VMEM((N_HEADS, 1, t), F32),
                        pltpu.VMEM((N_HEADS, V_DIM, t), F32)],
    )
    return pl.pallas_call(
        _attn_prompt_kernel,
        grid_spec=grid_spec,
        out_shape=jax.ShapeDtypeStruct((b, s, N_HEADS * V_DIM), BF16),
        compiler_params=_params(("arbitrary", "arbitrary")),
        name="attn_prompt",
    )(qi_tab, ki_tab, qt, kt, vt, gout_a)


def _proj_sample_kernel(x_ref, tpos_ref, st1_ref, st2_ref, gmix_ref, win_c_ref, win_at_ref, convw_ref,
                        convb_ref, gout_c_ref, gqlat_ref, wuqt_ref, gkv_ref, gqn_ref, gqr_ref, gkn_ref,
                        gkr_ref, wuk_ref, cos_ref, sin_ref,
                        yconv_ref, u_ref, qa_ref, qr_ref, ckv_ref, krope_ref, ext_ref):
    tm = x_ref.shape[0]
    c = convw_ref.shape[1]
    xf = x_ref[...]
    h = (xf * _rms_lanes(xf) * gmix_ref[...]).astype(BF16)

    zc = jnp.dot(h, win_c_ref[...], preferred_element_type=F32)
    u = zc[:, 2 * c:] * zc[:, :c]
    ext_ref[0:SUBLANES, :] = jnp.zeros((SUBLANES, c), F32)
    ext_ref[SUBLANES:, :] = u
    tpos = tpos_ref[...]
    u_m2 = jnp.where(tpos >= 2, ext_ref[pl.ds(SUBLANES - 2, tm), :], st2_ref[...])
    u_m1 = jnp.where(tpos >= 1, ext_ref[pl.ds(SUBLANES - 1, tm), :], st1_ref[...])
    v = convb_ref[...] + convw_ref[0:1, :] * u_m2 + convw_ref[1:2, :] * u_m1 + convw_ref[2:3, :] * u
    yconv = zc[:, c:2 * c] * v
    yconv_ref[...] = _group_norm_lanes(yconv, gout_c_ref[...]).astype(BF16)
    u_ref[...] = u

    q_nope, q_rope, ckvt, (k1, k2) = _attention_side(
        h, win_at_ref, gqlat_ref, wuqt_ref, gkv_ref, gqn_ref, gqr_ref, gkr_ref, cos_ref[...], sin_ref[...])
    for hd in range(N_HEADS):
        qg = (q_nope[hd] * gkn_ref[...]).astype(BF16)
        qa_t = jnp.dot(wuk_ref[:, hd * NOPE_DIM:(hd + 1) * NOPE_DIM], qg, preferred_element_type=F32)
        qa_ref[hd] = qa_t.T
        qr_ref[hd] = _to_token_major(jnp.concatenate(q_rope[hd], axis=0), ROPE_DIM)
    ckv_ref[...] = ckvt.T
    krope_ref[...] = _to_token_major(jnp.concatenate([k1, k2], axis=0), ROPE_DIM)


def _proj_sample(x, tpos, st1, st2, w, cos_t, sin_t):
    tm, d = x.shape
    c = w["conv_w"].shape[1]
    kv_rank = w["g_kv"].shape[0]
    args = [x, tpos, st1, st2, w["g_mix"], w["win_c"], w["win_at"], w["conv_w"], w["conv_b"], w["g_out_c"],
            w["g_qlat"], w["wuq_t"], w["g_kv"], w["g_qn"], w["g_qr"], w["g_kn"], w["g_kr"], w["wuk"],
            cos_t, sin_t]
    out_shape = [
        jax.ShapeDtypeStruct((tm, c), BF16),
        jax.ShapeDtypeStruct((tm, c), F32),
        jax.ShapeDtypeStruct((N_HEADS, tm, kv_rank), F32),
        jax.ShapeDtypeStruct((N_HEADS, tm, ROPE_DIM), F32),
        jax.ShapeDtypeStruct((tm, kv_rank), F32),
        jax.ShapeDtypeStruct((tm, ROPE_DIM), F32),
    ]
    return pl.pallas_call(
        _proj_sample_kernel,
        out_shape=out_shape,
        scratch_shapes=[pltpu.VMEM((tm + SUBLANES, c), F32)],
        compiler_params=pltpu.CompilerParams(vmem_limit_bytes=VMEM_LIMIT_BYTES),
        name="proj_sample",
    )(*args)


def _attn_decode_kernel(pt_ref, ckv_hbm, kr_hbm, qa_ref, qr_ref, cnew_ref, krnew_ref, wukt_ref, wuv_ref,
                        gout_ref, y_ref, cbuf, kbuf, sem, *, layer, n_pages, page, t_dec):
    b = pl.program_id(0)
    n_seq = pl.num_programs(0)
    chunk_pages = cbuf.shape[1] // page
    n_chunks = n_pages // chunk_pages
    rows = N_HEADS * t_dec

    def copies(first_page, slot):
        out = []
        for pg in range(chunk_pages):
            pid = 0 if first_page is None else pt_ref[first_page + pg]
            out.append(pltpu.make_async_copy(ckv_hbm.at[layer, pid], cbuf.at[slot, pl.ds(pg * page, page)], sem.at[0, slot]))
            out.append(pltpu.make_async_copy(kr_hbm.at[layer, pid], kbuf.at[slot, pg], sem.at[1, slot]))
        return out

    qa = qa_ref[...].reshape(rows, qa_ref.shape[2]).astype(BF16)
    qr = qr_ref[...].reshape(rows, ROPE_DIM).astype(BF16)

    n_up = wukt_ref.shape[0]
    wq = jnp.concatenate([wukt_ref[...], qa], axis=0)

    def nope_scores(c_b):
        both = lax.dot_general(wq, c_b, _NT, preferred_element_type=F32)
        rs = []
        for hd in range(N_HEADS):
            blk = both[hd * NOPE_DIM:(hd + 1) * NOPE_DIM]
            rs.append(jnp.broadcast_to(_rms_rows(blk), (t_dec, blk.shape[1])))
        return both[n_up:] * jnp.concatenate(rs, axis=0)

    def update(carry, s, c_b):
        m_prev, l_prev, acc = carry
        m_new = jnp.maximum(m_prev, jnp.max(s, axis=-1, keepdims=True))
        alpha = jnp.exp2(m_prev - m_new)
        pr = jnp.exp2(s - m_new)
        l_new = alpha * l_prev + jnp.sum(pr, axis=-1, keepdims=True)
        acc = alpha * acc + jnp.dot(pr.astype(BF16), c_b, preferred_element_type=F32)
        return m_new, l_new, acc

    last = n_seq * n_chunks - 1
    g0 = b * n_chunks

    def fetch(g):
        for cp in copies(jnp.minimum(g, last) * chunk_pages, lax.rem(g, RING)):
            cp.start()

    def arrive(g):
        for cp in copies(None, lax.rem(g, RING)):
            cp.wait()

    def chunk_scores(g):
        slot = lax.rem(g, RING)
        s_rope = jnp.concatenate(
            [jnp.dot(qr, kbuf[slot, pg].astype(BF16), preferred_element_type=F32) for pg in range(chunk_pages)], axis=1)
        return nope_scores(cbuf[slot].astype(BF16)) + s_rope

    def chunk_update(carry, s, g):
        return update(carry, s, cbuf[lax.rem(g, RING)].astype(BF16))

    ahead = RING - 1

    @pl.when(b == 0)
    def _():
        for k in range(ahead):
            fetch(k)

    arrive(g0)
    s_first = chunk_scores(g0)

    def body(j, state):
        carry, s = state
        g = g0 + j
        fetch(g + ahead)
        arrive(g + 1)
        s_next = chunk_scores(g + 1)
        return chunk_update(carry, s, g), s_next

    init = (jnp.full((rows, 1), -jnp.inf, F32), jnp.zeros((rows, 1), F32), jnp.zeros((rows, cbuf.shape[2]), F32))
    carry, s_last = lax.fori_loop(0, n_chunks - 1, body, (init, s_first))
    fetch(g0 + n_chunks - 1 + ahead)
    carry = chunk_update(carry, s_last, g0 + n_chunks - 1)

    @pl.when(b == n_seq - 1)
    def _():
        for k in range(1, ahead + 1):
            arrive(last + k)

    pad = LANES - t_dec
    c_new = jnp.concatenate([cnew_ref[...], jnp.zeros((pad, cnew_ref.shape[1]), F32)], axis=0).astype(BF16)
    kr_new = jnp.concatenate([krnew_ref[...], jnp.zeros((pad, ROPE_DIM), F32)], axis=0).astype(BF16)
    s_new = nope_scores(c_new) + lax.dot_general(qr, kr_new, _NT, preferred_element_type=F32)
    q_t = lax.rem(lax.broadcasted_iota(jnp.int32, (rows, LANES), 0), t_dec)
    key = lax.broadcasted_iota(jnp.int32, (rows, LANES), 1)
    s_new = jnp.where(key <= q_t, s_new, MASK_VALUE)
    _, l_fin, acc = update(carry, s_new, c_new)

    o_lat = (acc / l_fin).astype(BF16)
    ov = jnp.dot(o_lat, wuv_ref[...], preferred_element_type=F32)
    outs = []
    for hd in range(N_HEADS):
        o = ov[hd * t_dec:(hd + 1) * t_dec, hd * V_DIM:(hd + 1) * V_DIM]
        outs.append(o * _rms_lanes(o) * gout_ref[hd])
    y_ref[...] = jnp.concatenate(outs, axis=1)


def _attn_decode(layer, page_table, cache_ckv, cache_kr, qa, qr, c_new, kr_new, w, t_dec):
    n_seq, n_pages = page_table.shape
    _, _, page, kv_rank = cache_ckv.shape
    chunk_pages = min(DECODE_CHUNK_PAGES, n_pages)
    assert n_pages % chunk_pages == 0 and t_dec % SUBLANES == 0 and t_dec <= LANES
    chunk = chunk_pages * page
    grid_spec = pltpu.PrefetchScalarGridSpec(
        num_scalar_prefetch=1,
        grid=(n_seq,),
        in_specs=[
            pl.BlockSpec(memory_space=pl.ANY),
            pl.BlockSpec(memory_space=pl.ANY),
            pl.BlockSpec((N_HEADS, t_dec, kv_rank), lambda b, pt: (0, b, 0)),
            pl.BlockSpec((N_HEADS, t_dec, ROPE_DIM), lambda b, pt: (0, b, 0)),
            pl.BlockSpec((t_dec, kv_rank), lambda b, pt: (b, 0)),
            pl.BlockSpec((t_dec, ROPE_DIM), lambda b, pt: (b, 0)),
            pl.BlockSpec(w["wuk_t"].shape, lambda b, pt: (0, 0)),
            pl.BlockSpec(w["wuv"].shape, lambda b, pt: (0, 0)),
            pl.BlockSpec(w["g_out_a_row"].shape, lambda b, pt: (0, 0, 0)),
        ],
        out_specs=pl.BlockSpec((t_dec, N_HEADS * V_DIM), lambda b, pt: (b, 0)),
        scratch_shapes=[pltpu.VMEM((RING, chunk, kv_rank), F32), pltpu.VMEM((RING, chunk_pages, ROPE_DIM, page), F32),
                        pltpu.SemaphoreType.DMA((2, RING))],
    )
    return pl.pallas_call(
        functools.partial(_attn_decode_kernel, layer=layer, n_pages=n_pages, page=page, t_dec=t_dec),
        grid_spec=grid_spec,
        out_shape=jax.ShapeDtypeStruct((n_seq * t_dec, N_HEADS * V_DIM), F32),
        compiler_params=_params(("arbitrary",)),
        name="attn_decode",
    )(page_table.reshape(-1), cache_ckv, cache_kr, qa, qr, c_new, kr_new, w["wuk_t"], w["wuv"], w["g_out_a_row"])


ROUTE_E0, ROUTE_E1, ROUTE_R0, ROUTE_R1, ROUTE_W0, ROUTE_W1 = range(6)


def _lane_pick(x, lane, idx):
    return jnp.sum(jnp.where(lane == idx, x, 0.0), axis=-1, keepdims=True)


def _store_row_tiles(ref, x):
    rows, width = x.shape
    n = width // LANES
    for j in range(n):
        ref[pl.ds(j, rows, stride=n), :] = x[:, j * LANES:(j + 1) * LANES]


def _load_row_tiles(ref, rows, n):
    return jnp.concatenate([ref[pl.ds(j, rows, stride=n), :] for j in range(n)], axis=1)


def _merge_route_kernel(ycp_ref, yap_ref, xp_ref, ycs_ref, yas_ref, xs_ref, wout_ref, gffn_ref, wr_ref, br_ref,
                        xmid_ref, h2_ref, route_ref, counts_ref, carry_ref, *, n_prompt_tiles):
    i = pl.program_id(0)
    tm = xp_ref.shape[0]
    is_p = i < n_prompt_tiles

    @pl.when(i == 0)
    def _():
        carry_ref[...] = jnp.zeros(carry_ref.shape, F32)

    yc = jnp.where(is_p, ycp_ref[...], ycs_ref[...])
    ya = jnp.where(is_p, yap_ref[...], yas_ref[...].astype(BF16))
    x = jnp.where(is_p, xp_ref[...], xs_ref[...])
    y = jnp.concatenate([yc, ya], axis=1)
    xm = x + jnp.dot(y, wout_ref[...], preferred_element_type=F32)
    xmid_ref[...] = xm
    h2f = xm * _rms_lanes(xm) * gffn_ref[...]
    _store_row_tiles(h2_ref, h2f)
    h2 = h2f.astype(BF16)
    logits = jnp.dot(h2, wr_ref[...], preferred_element_type=F32) + br_ref[...]

    lane_i = lax.broadcasted_iota(jnp.int32, (tm, LANES), 1)
    lane = lane_i.astype(F32)
    neg = -jnp.inf
    far = float(LANES)
    in_groups = lane_i < N_GROUPS
    gl = jnp.where(in_groups, logits, neg)
    ge = jnp.exp(gl - jnp.max(gl, axis=-1, keepdims=True))
    pg = ge / jnp.sum(ge, axis=-1, keepdims=True)
    p_sel = jnp.max(pg, axis=-1, keepdims=True)
    g_sel = jnp.min(jnp.where((pg == p_sel) & in_groups, lane, far), axis=-1, keepdims=True)
    lo = N_GROUPS + g_sel * EXPERTS_PER_GROUP
    el = jnp.where((lane >= lo) & (lane < lo + EXPERTS_PER_GROUP), logits, neg)
    v1 = jnp.max(el, axis=-1, keepdims=True)
    i1 = jnp.min(jnp.where(el == v1, lane, far), axis=-1, keepdims=True)
    el2 = jnp.where(lane == i1, neg, el)
    v2 = jnp.max(el2, axis=-1, keepdims=True)
    i2 = jnp.min(jnp.where(el2 == v2, lane, far), axis=-1, keepdims=True)
    e2 = jnp.exp(v2 - v1)
    w0 = 1.0 / (1.0 + e2) * p_sel
    w1 = e2 / (1.0 + e2) * p_sel
    e0 = i1 - N_GROUPS
    e1 = i2 - N_GROUPS

    oh0 = lane == e0
    oh1 = lane == e1
    onehot = (oh0 | oh1).astype(BF16)
    tri = (lax.broadcasted_iota(jnp.int32, (tm, tm), 0) > lax.broadcasted_iota(jnp.int32, (tm, tm), 1)).astype(BF16)
    before = jnp.dot(tri, onehot, preferred_element_type=F32) + carry_ref[...]
    r0 = jnp.sum(jnp.where(oh0, before, 0.0), axis=-1, keepdims=True)
    r1 = jnp.sum(jnp.where(oh1, before, 0.0), axis=-1, keepdims=True)
    carry_ref[...] += jnp.sum(onehot.astype(F32), axis=0, keepdims=True)

    route = jnp.zeros((tm, LANES), F32)
    for idx, val in ((ROUTE_E0, e0), (ROUTE_E1, e1), (ROUTE_R0, r0), (ROUTE_R1, r1), (ROUTE_W0, w0), (ROUTE_W1, w1)):
        route = jnp.where(lane_i == idx, val, route)
    route_ref[...] = route.T[:SUBLANES]

    @pl.when(i == pl.num_programs(0) - 1)
    def _():
        counts_ref[...] = carry_ref[...]


def _merge_route(ycp, yap, xp, ycs, yas, xs, w):
    tp, d = xp.shape
    ts = xs.shape[0]
    tm = TOKEN_TILE
    assert tp % tm == 0 and ts % tm == 0
    npt, nst = tp // tm, ts // tm
    half = ycp.shape[1]
    pmap = lambda i: (jnp.minimum(i, npt - 1), 0)
    smap = lambda i: (jnp.maximum(i - npt, 0), 0)
    cmap = lambda i: (0, 0)
    return pl.pallas_call(
        functools.partial(_merge_route_kernel, n_prompt_tiles=npt),
        grid=(npt + nst,),
        in_specs=[
            pl.BlockSpec((tm, half), pmap), pl.BlockSpec((tm, half), pmap), pl.BlockSpec((tm, d), pmap),
            pl.BlockSpec((tm, half), smap), pl.BlockSpec((tm, half), smap), pl.BlockSpec((tm, d), smap),
            pl.BlockSpec(w["w_out"].shape, cmap), pl.BlockSpec(w["g_ffn"].shape, cmap),
            pl.BlockSpec(w["w_r"].shape, cmap), pl.BlockSpec(w["b_r"].shape, cmap),
        ],
        out_specs=[pl.BlockSpec((tm, d), lambda i: (i, 0)),
                   pl.BlockSpec((tm * (d // LANES), LANES), lambda i: (i, 0)),
                   pl.BlockSpec((SUBLANES, tm), lambda i: (0, i)),
                   pl.BlockSpec((1, LANES), cmap)],
        out_shape=[jax.ShapeDtypeStruct((tp + ts, d), F32),
                   jax.ShapeDtypeStruct(((tp + ts) * (d // LANES), LANES), F32),
                   jax.ShapeDtypeStruct((SUBLANES, tp + ts), F32),
                   jax.ShapeDtypeStruct((1, LANES), F32)],
        scratch_shapes=[pltpu.VMEM((1, LANES), F32)],
        compiler_params=_params(("arbitrary",)),
        name="merge_route",
    )(ycp, yap, xp, ycs, yas, xs, w["w_out"], w["g_ffn"], w["w_r"], w["b_r"])


def _tile_of(ref, row, n):
    start = row * n
    return ref.at[pl.ds(start if isinstance(row, int) else pl.multiple_of(start, n), n)]


def _pad_fill_copies(base_ref, cnt_ref, tiles_ref, nu_ref, zeros_ref, xs_hbm, sem, tme, n, n_tiles):
    out = []
    for e in range(N_EXPERTS):
        pad = tiles_ref[e] * tme - cnt_ref[e]
        pos = base_ref[e] + cnt_ref[e]
        bit = tme // 2
        while bit >= 1:
            take = pad & bit
            out.append((take != 0, pltpu.make_async_copy(
                zeros_ref.at[pl.ds(0, bit * n)], xs_hbm.at[pl.ds(pl.multiple_of(pos * n, n), bit * n)], sem)))
            pos = pos + take
            bit //= 2
    for k in range(N_EXPERTS + 1):
        tile = nu_ref[0] + k
        out.append((tile < n_tiles, pltpu.make_async_copy(
            zeros_ref, xs_hbm.at[pl.ds(pl.multiple_of(jnp.minimum(tile, n_tiles - 1) * (tme * n), tme * n), tme * n)],
            sem)))
    return out


def _dispatch_kernel(base_ref, cnt_ref, tiles_ref, nu_ref, s0_ref, s1_ref, h2_ref, xs_hbm, zeros, sem,
                     *, tm, tme, n, n_tiles):
    i = pl.program_id(0)
    for t0 in range(0, tm, SLOT_BATCH):
        slots = [(s0_ref[0, 0, t], s1_ref[0, 0, t]) for t in range(t0, t0 + SLOT_BATCH)]
        for t, pair in zip(range(t0, t0 + SLOT_BATCH), slots):
            for k, s in enumerate(pair):
                pltpu.make_async_copy(_tile_of(h2_ref, t, n), _tile_of(xs_hbm, s, n), sem.at[0]).start(priority=k)

    def fills():
        return _pad_fill_copies(base_ref, cnt_ref, tiles_ref, nu_ref, zeros, xs_hbm, sem.at[1], tme, n, n_tiles)

    @pl.when(i == 0)
    def _():
        zeros[...] = jnp.zeros(zeros.shape, F32)
        for pred, cp in fills():
            pl.when(pred)(cp.start)

    for t in range(2 * tm):
        pltpu.make_async_copy(_tile_of(h2_ref, 0, n), _tile_of(xs_hbm, 0, n), sem.at[0]).wait()

    @pl.when(i == 0)
    def _():
        for pred, cp in fills():
            pl.when(pred)(cp.wait)


def _dispatch(plan, h2_tiles, n_slots):
    base, cnt, tiles, n_used, slot0, slot1 = plan
    n_tok_tiles, _, tm = slot0.shape
    n = h2_tiles.shape[0] // (n_tok_tiles * tm)
    tme = EXPERT_TILE
    n_tiles = n_slots // tme
    smem_blk = pl.BlockSpec((1, 1, tm), lambda i, *_: (i, 0, 0), memory_space=pltpu.SMEM)
    grid_spec = pltpu.PrefetchScalarGridSpec(
        num_scalar_prefetch=4,
        grid=(n_tok_tiles,),
        in_specs=[smem_blk] * 2 + [pl.BlockSpec((tm * n, LANES), lambda i, *_: (i, 0))],
        out_specs=pl.BlockSpec(memory_space=pl.ANY),
        scratch_shapes=[pltpu.VMEM((tme * n, LANES), F32), pltpu.SemaphoreType.DMA((2,))],
    )
    return pl.pallas_call(
        functools.partial(_dispatch_kernel, tm=tm, tme=tme, n=n, n_tiles=n_tiles),
        grid_spec=grid_spec,
        out_shape=jax.ShapeDtypeStruct((n_slots * n, LANES), F32),
        compiler_params=_params(("arbitrary",)),
        name="dispatch",
    )(base, cnt, tiles, n_used, slot0, slot1, h2_tiles)


def _experts_kernel(te_ref, nu_ref, xs_ref, wg_ref, wu_ref, wd_ref, ys_ref, wgu_bf, wd_bf):
    i = pl.program_id(0)
    n_used = nu_ref[0]
    d_exp = wd_ref.shape[2]
    d = wd_ref.shape[3]
    n = d // LANES
    tme = xs_ref.shape[0] // n
    new_expert = (i == 0) | (te_ref[i] != te_ref[jnp.maximum(i - 1, 0)])

    @pl.when(new_expert & (i < n_used))
    def _():
        wgu_bf[:, :d_exp] = wg_ref[0, 0].astype(BF16)
        wgu_bf[:, d_exp:] = wu_ref[0, 0].astype(BF16)
        wd_bf[...] = wd_ref[0, 0].astype(BF16)

    @pl.when(i < n_used)
    def _():
        h2 = _load_row_tiles(xs_ref, tme, n).astype(BF16)
        gu = jnp.dot(h2, wgu_bf[...], preferred_element_type=F32)
        g = gu[:, :d_exp]
        a = (g / (1.0 + jnp.exp(-g))) * gu[:, d_exp:]
        _store_row_tiles(ys_ref, jnp.dot(a.astype(BF16), wd_bf[...], preferred_element_type=F32))

    @pl.when(i >= n_used)
    def _():
        ys_ref[...] = jnp.zeros(ys_ref.shape, F32)


def _experts(layer, tile_expert, n_used, xs_tiles, w_gate, w_up, w_down):
    n_tiles = tile_expert.shape[0]
    _, _, d, d_exp = w_gate.shape
    blk = xs_tiles.shape[0] // n_tiles
    grid_spec = pltpu.PrefetchScalarGridSpec(
        num_scalar_prefetch=2,
        grid=(n_tiles,),
        in_specs=[
            pl.BlockSpec((blk, LANES), lambda i, te, nu: (jnp.minimum(i, nu[0] - 1), 0)),
            pl.BlockSpec((1, 1, d, d_exp), lambda i, te, nu: (layer, te[i], 0, 0)),
            pl.BlockSpec((1, 1, d, d_exp), lambda i, te, nu: (layer, te[i], 0, 0)),
            pl.BlockSpec((1, 1, d_exp, d), lambda i, te, nu: (layer, te[i], 0, 0)),
        ],
        out_specs=pl.BlockSpec((blk, LANES), lambda i, te, nu: (i, 0)),
        scratch_shapes=[pltpu.VMEM((d, 2 * d_exp), BF16), pltpu.VMEM((d_exp, d), BF16)],
    )
    return pl.pallas_call(
        _experts_kernel,
        grid_spec=grid_spec,
        out_shape=jax.ShapeDtypeStruct(xs_tiles.shape, F32),
        compiler_params=_params(("arbitrary",)),
        name="experts",
    )(tile_expert, n_used, xs_tiles, w_gate, w_up, w_down)


def _combine_kernel(s0_ref, s1_ref, s0n_ref, s1n_ref, ys_hbm, xmid_ref, route_ref, yp_ref, ysmp_ref, buf, sem,
                    *, n_prompt_tiles):
    i = pl.program_id(0)
    last = pl.num_programs(0) - 1
    tm, d = xmid_ref.shape
    n = d // LANES
    slot = lax.rem(i, 2)

    def gather(refs, dst_slot):
        s0, s1 = refs
        for t0 in range(0, tm, SLOT_BATCH):
            slots = [(s0[0, 0, t], s1[0, 0, t]) for t in range(t0, t0 + SLOT_BATCH)]
            for t, pair in zip(range(t0, t0 + SLOT_BATCH), slots):
                for k, s in enumerate(pair):
                    pltpu.make_async_copy(_tile_of(ys_hbm, s, n), _tile_of(buf.at[dst_slot, k], t, n),
                                          sem.at[dst_slot]).start(priority=k)

    def gather_wait(dst_slot):
        for t in range(2 * tm):
            pltpu.make_async_copy(_tile_of(ys_hbm, 0, n), _tile_of(buf.at[dst_slot, 0], 0, n), sem.at[dst_slot]).wait()

    @pl.when(i == 0)
    def _():
        gather((s0_ref, s1_ref), 0)

    gather((s0n_ref, s1n_ref), 1 - slot)
    gather_wait(slot)

    rt = route_ref[...]
    cols = jnp.concatenate([rt, jnp.zeros((LANES - rt.shape[0], tm), F32)], axis=0).T
    lane = lax.broadcasted_iota(jnp.int32, cols.shape, 1)
    w0 = _lane_pick(cols, lane, ROUTE_W0)
    w1 = _lane_pick(cols, lane, ROUTE_W1)
    y0 = _load_row_tiles(buf.at[slot, 0], tm, n)
    y1 = _load_row_tiles(buf.at[slot, 1], tm, n)
    out = xmid_ref[...] + (w0 * y0 + w1 * y1)

    @pl.when(i < n_prompt_tiles)
    def _():
        yp_ref[...] = out

    @pl.when(i >= n_prompt_tiles)
    def _():
        ysmp_ref[...] = out

    @pl.when(i == last)
    def _():
        gather_wait(1 - slot)


def _combine(plan, ys_tiles, xmid, route_t, tp):
    slot0, slot1 = plan[4:]
    ttot, d = xmid.shape
    n_tok_tiles, _, tm = slot0.shape
    npt = tp // tm
    n = d // LANES
    cur = pl.BlockSpec((1, 1, tm), lambda i: (i, 0, 0), memory_space=pltpu.SMEM)
    nxt = pl.BlockSpec((1, 1, tm), lambda i: (jnp.minimum(i + 1, n_tok_tiles - 1), 0, 0), memory_space=pltpu.SMEM)
    return pl.pallas_call(
        functools.partial(_combine_kernel, n_prompt_tiles=npt),
        grid=(n_tok_tiles,),
        in_specs=[cur, cur, nxt, nxt,
                  pl.BlockSpec(memory_space=pl.ANY),
                  pl.BlockSpec((tm, d), lambda i: (i, 0)),
                  pl.BlockSpec((SUBLANES, tm), lambda i: (0, i))],
        out_specs=[pl.BlockSpec((tm, d), lambda i: (jnp.minimum(i, npt - 1), 0)),
                   pl.BlockSpec((tm, d), lambda i: (jnp.maximum(i - npt, 0), 0))],
        out_shape=[jax.ShapeDtypeStruct((tp, d), F32), jax.ShapeDtypeStruct((ttot - tp, d), F32)],
        scratch_shapes=[pltpu.VMEM((2, 2, tm * n, LANES), F32), pltpu.SemaphoreType.DMA((2,))],
        compiler_params=_params(("arbitrary",)),
        name="combine",
    )(slot0, slot1, slot0, slot1, ys_tiles, xmid, route_t)


def _rope_tables(pos):
    inv_freq = ROPE_BASE ** (-jnp.arange(HALF_ROPE, dtype=F32) / HALF_ROPE)
    ang = pos.astype(F32)[:, None] * inv_freq[None, :]
    return jnp.cos(ang).T, jnp.sin(ang).T


def _layer_weights(l, g_mix, w_in, conv_w, conv_b, g_q_lat, w_uq, g_kv_lat, w_uk, w_uv, g_q_nope, g_q_rope,
                   g_k_nope, g_k_rope, g_out, w_out, g_ffn, w_router_group, b_router_group, w_router_expert,
                   b_router_expert, w_gate, w_up, w_down):
    c = conv_w.shape[2]
    col = lambda g: g[l].reshape(-1, 1)
    w_r = jnp.concatenate([w_router_group[l], w_router_expert[l]], axis=1)
    b_r = jnp.concatenate([b_router_group[l], b_router_expert[l]])
    return {
        "g_mix": g_mix[l].reshape(1, -1),
        "win_c": w_in[l][:, :3 * c].astype(BF16),
        "win_at": w_in[l][:, 3 * c:].T.astype(BF16),
        "conv_w": conv_w[l],
        "conv_b": conv_b[l].reshape(1, -1),
        "g_out_c": g_out[l][:c].reshape(1, -1),
        "g_out_a": g_out[l][c:].reshape(N_HEADS, V_DIM, 1),
        "g_out_a_row": g_out[l][c:].reshape(N_HEADS, 1, V_DIM),
        "g_qlat": col(g_q_lat), "g_kv": col(g_kv_lat), "g_qn": col(g_q_nope), "g_qr": col(g_q_rope),
        "g_kn": col(g_k_nope), "g_kr": col(g_k_rope),
        "wuq_t": w_uq[l].T.astype(BF16),
        "wuk_t": w_uk[l].T.astype(BF16),
        "wuk": w_uk[l].astype(BF16),
        "wuv_t": w_uv[l].T.astype(BF16),
        "wuv": w_uv[l].astype(BF16),
        "w_out": w_out[l].astype(BF16),
        "g_ffn": g_ffn[l].reshape(1, -1),
        "w_r": jnp.pad(w_r, ((0, 0), (0, LANES - w_r.shape[1]))).astype(BF16),
        "b_r": jnp.pad(b_r, (0, LANES - b_r.shape[0])).reshape(1, -1),
    }


def _moe_plan(route_t, counts):
    ttot = route_t.shape[1]
    tme = EXPERT_TILE
    tm = TOKEN_TILE
    n_tiles = (TOP_K * ttot) // tme + N_EXPERTS + 1
    cnt = counts[0, :N_EXPERTS].astype(jnp.int32)
    tiles = (cnt + tme - 1) // tme
    tile_end = jnp.cumsum(tiles)
    base = (tile_end - tiles) * tme
    n_used = tile_end[-1:]
    tile_id = jnp.minimum(jnp.arange(n_tiles, dtype=jnp.int32), n_used[0] - 1)
    tile_expert = jnp.sum((tile_end[None, :] <= tile_id[:, None]).astype(jnp.int32), axis=1)
    ids = route_t[:ROUTE_R1 + 1].astype(jnp.int32)
    expert_ids = jnp.arange(N_EXPERTS, dtype=jnp.int32)[:, None]

    def slots(e, r):
        s = r + jnp.sum(jnp.where(e[None, :] == expert_ids, base[:, None], 0), axis=0)
        return s.reshape(ttot // tm, 1, tm)

    plan = (base, cnt, tiles, n_used, slots(ids[ROUTE_E0], ids[ROUTE_R0]), slots(ids[ROUTE_E1], ids[ROUTE_R1]))
    return plan, tile_expert, n_tiles * tme


def kernel(x_prompt, x_sample, state_conv, cache_ckv, cache_krope, page_table, g_mix, w_in, conv_w, conv_b, g_q_lat, w_uq, g_kv_lat, w_uk, w_uv, g_q_nope, g_q_rope, g_k_nope, g_k_rope, g_out, w_out, g_ffn, w_router_group, b_router_group, w_router_expert, b_router_expert, w_gate, w_up, w_down):
    b_p, s_p, d = x_prompt.shape
    b_s, t_s, _ = x_sample.shape
    depth = g_mix.shape[0]
    c = conv_w.shape[2]
    page = cache_ckv.shape[2]
    past_len = page_table.shape[1] * page
    kv_rank = cache_ckv.shape[3]

    cos_p, sin_p = _rope_tables(jnp.arange(s_p, dtype=jnp.int32))
    cos_s, sin_s = _rope_tables(jnp.tile(past_len + jnp.arange(t_s, dtype=jnp.int32), b_s))
    tpos = jnp.tile(jnp.arange(t_s, dtype=jnp.int32), b_s).reshape(-1, 1)

    krope_pages = jnp.swapaxes(cache_krope, 2, 3)

    xp, xs = x_prompt, x_sample.reshape(b_s * t_s, d)
    outs = [[] for _ in range(6)]
    for l in range(depth):
        w = _layer_weights(l, g_mix, w_in, conv_w, conv_b, g_q_lat, w_uq, g_kv_lat, w_uk, w_uv, g_q_nope,
                           g_q_rope, g_k_nope, g_k_rope, g_out, w_out, g_ffn, w_router_group, b_router_group,
                           w_router_expert, b_router_expert, w_gate, w_up, w_down)
        yconv_p, qt, kt, vt, ckv_p, kr_p, conv_p = _proj_prompt(xp, w, cos_p, sin_p)
        yattn_p = _attn_prompt(qt, kt, vt, w["g_out_a"])
        st = state_conv[l]
        zeros = lambda n: jnp.zeros((b_s, n, c), F32)
        st1 = jnp.concatenate([st[:, CONV_K - 2:], zeros(t_s - 1)], axis=1).reshape(b_s * t_s, c)
        st2 = jnp.concatenate([st, zeros(t_s - (CONV_K - 1))], axis=1).reshape(b_s * t_s, c)
        yconv_s, u_s, qa, qr, ckv_s, kr_s = _proj_sample(xs, tpos, st1, st2, w, cos_s, sin_s)
        yattn_s = _attn_decode(l, page_table, cache_ckv, krope_pages, qa, qr, ckv_s, kr_s, w, t_s)
        xmid, h2_tiles, route_t, counts = _merge_route(
            yconv_p.reshape(b_p * s_p, c), yattn_p.reshape(b_p * s_p, -1), xp.reshape(b_p * s_p, d),
            yconv_s, yattn_s, xs, w)
        plan, tile_expert, n_slots = _moe_plan(route_t, counts)
        xs_tiles = _dispatch(plan, h2_tiles, n_slots)
        ys_tiles = _experts(l, tile_expert, plan[3], xs_tiles, w_gate, w_up, w_down)
        yp, ysmp = _combine(plan, ys_tiles, xmid, route_t, b_p * s_p)
        xp, xs = yp.reshape(b_p, s_p, d), ysmp
        for lst, val in zip(outs, (ckv_p, kr_p, conv_p, ckv_s.reshape(b_s, t_s, kv_rank),
                                   kr_s.reshape(b_s, t_s, ROPE_DIM),
                                   u_s.reshape(b_s, t_s, c)[:, t_s - (CONV_K - 1):])):
            lst.append(val)
    return (xp, xs.reshape(b_s, t_s, d)) + tuple(jnp.stack(o) for o in outs)
```

```python
import functools

import jax
import jax.numpy as jnp
from jax import lax
from jax.experimental import pallas as pl
from jax.experimental.pallas import tpu as pltpu

N_HEADS = 8
NOPE_DIM = 64
ROPE_DIM = 32
V_DIM = 64
HEAD_QK = NOPE_DIM + ROPE_DIM
HALF_ROPE = ROPE_DIM // 2
ROPE_BASE = 10000.0
CONV_K = 3
OUT_GROUP_DIM = 64
N_GROUPS = 4
EXPERTS_PER_GROUP = 8
N_EXPERTS = N_GROUPS * EXPERTS_PER_GROUP
TOP_K = 2
EPS = 1e-6
MASK_VALUE = -1e30
LOG2_E = 1.4426950408889634

LANES = 128
SUBLANES = 8
VMEM_LIMIT_BYTES = 48 * 1024 * 1024

PROJ_TILE = 512
ATTN_Q_TILE = 512
ATTN_K_TILE = 512
TOKEN_TILE = 256
EXPERT_TILE = 512
DECODE_CHUNK_PAGES = 8
RING = 4
SLOT_BATCH = 8

F32 = jnp.float32
BF16 = jnp.bfloat16

_NT = (((1,), (1,)), ((), ()))
_TN = (((0,), (0,)), ((), ()))


def _params(sem):
    return pltpu.CompilerParams(dimension_semantics=sem, vmem_limit_bytes=VMEM_LIMIT_BYTES)


def _rms_rows(x):
    return lax.rsqrt(jnp.mean(x * x, axis=0, keepdims=True) + EPS)


def _rms_lanes(x):
    return lax.rsqrt(jnp.mean(x * x, axis=-1, keepdims=True) + EPS)


def _group_norm_lanes(y, gain):
    lane = lax.broadcasted_iota(jnp.int32, (1, LANES), 1)
    low = lane < OUT_GROUP_DIM
    outs = []
    for j in range(y.shape[1] // LANES):
        t = y[:, j * LANES:(j + 1) * LANES]
        sq = t * t
        ss_lo = jnp.sum(jnp.where(low, sq, 0.0), axis=-1, keepdims=True)
        ss_hi = jnp.sum(jnp.where(low, 0.0, sq), axis=-1, keepdims=True)
        r = jnp.where(low, lax.rsqrt(ss_lo / OUT_GROUP_DIM + EPS), lax.rsqrt(ss_hi / OUT_GROUP_DIM + EPS))
        outs.append(t * r)
    return jnp.concatenate(outs, axis=1) * gain


def _rope_rows(x, cos, sin):
    x1, x2 = x[:HALF_ROPE], x[HALF_ROPE:]
    return x1 * cos - x2 * sin, x1 * sin + x2 * cos


def _attention_side(h_bf16, win_at_ref, gqlat_ref, wuqt_ref, gkv_ref, gqn_ref, gqr_ref, gkr_ref, cos, sin):
    q_rank = gqlat_ref.shape[0]
    kv_rank = gkv_ref.shape[0]
    scale = HEAD_QK ** -0.5 * LOG2_E
    zat = lax.dot_general(win_at_ref[...], h_bf16, _NT, preferred_element_type=F32)
    qlt = zat[:q_rank]
    kvt = zat[q_rank:q_rank + kv_rank]
    krt = zat[q_rank + kv_rank:]
    qln = (qlt * _rms_rows(qlt) * gqlat_ref[...]).astype(BF16)
    qt = jnp.dot(wuqt_ref[...], qln, preferred_element_type=F32)
    q_nope, q_rope = [], []
    for h in range(N_HEADS):
        nope = qt[h * HEAD_QK:h * HEAD_QK + NOPE_DIM]
        rope = qt[h * HEAD_QK + NOPE_DIM:(h + 1) * HEAD_QK]
        q_nope.append(nope * _rms_rows(nope) * gqn_ref[...] * scale)
        r1, r2 = _rope_rows(rope * _rms_rows(rope) * gqr_ref[...], cos, sin)
        q_rope.append((r1 * scale, r2 * scale))
    ckvt = kvt * _rms_rows(kvt) * gkv_ref[...]
    k1, k2 = _rope_rows(krt * _rms_rows(krt) * gkr_ref[...], cos, sin)
    return q_nope, q_rope, ckvt, (k1, k2)


def _to_token_major(xt, width):
    rows, toks = xt.shape
    if rows < LANES:
        xt = jnp.concatenate([xt, jnp.zeros((LANES - rows, toks), F32)], axis=0)
    return xt.T[:, :width]


def _proj_prompt_kernel(x_ref, gmix_ref, win_c_ref, win_at_ref, convw_ref, convb_ref, gout_c_ref,
                        gqlat_ref, wuqt_ref, gkv_ref, gqn_ref, gqr_ref, gkn_ref, gkr_ref,
                        wukt_ref, wuvt_ref, cos_ref, sin_ref,
                        yconv_ref, qt_ref, kt_ref, vt_ref, ckv_ref, krope_ref, convst_ref,
                        ext_ref):
    si = pl.program_id(1)
    tm = x_ref.shape[1]
    c = convw_ref.shape[1]

    xf = x_ref[0]
    h = (xf * _rms_lanes(xf) * gmix_ref[...]).astype(BF16)

    zc = jnp.dot(h, win_c_ref[...], preferred_element_type=F32)
    u = zc[:, 2 * c:] * zc[:, :c]

    @pl.when(si == 0)
    def _():
        ext_ref[0:SUBLANES, :] = jnp.zeros((SUBLANES, c), F32)

    ext_ref[SUBLANES:, :] = u
    v = (convb_ref[...]
         + convw_ref[0:1, :] * ext_ref[pl.ds(SUBLANES - 2, tm), :]
         + convw_ref[1:2, :] * ext_ref[pl.ds(SUBLANES - 1, tm), :]
         + convw_ref[2:3, :] * u)
    yconv = zc[:, c:2 * c] * v
    yconv_ref[0] = _group_norm_lanes(yconv, gout_c_ref[...]).astype(BF16)
    ext_ref[0:SUBLANES, :] = ext_ref[pl.ds(tm, SUBLANES), :]
    convst_ref[0] = ext_ref[pl.ds(SUBLANES - (CONV_K - 1), CONV_K - 1), :]

    q_nope, q_rope, ckvt, (k1, k2) = _attention_side(
        h, win_at_ref, gqlat_ref, wuqt_ref, gkv_ref, gqn_ref, gqr_ref, gkr_ref, cos_ref[...], sin_ref[...])
    for hd in range(N_HEADS):
        qt_ref[0, hd, 0:NOPE_DIM, :] = q_nope[hd].astype(BF16)
        qt_ref[0, hd, NOPE_DIM:NOPE_DIM + HALF_ROPE, :] = q_rope[hd][0].astype(BF16)
        qt_ref[0, hd, NOPE_DIM + HALF_ROPE:HEAD_QK, :] = q_rope[hd][1].astype(BF16)
    ckv_ref[0] = ckvt.T
    krt = jnp.concatenate([k1, k2], axis=0)
    krope_ref[0] = _to_token_major(krt, ROPE_DIM)
    ckv_b = ckvt.astype(BF16)
    ktn = jnp.dot(wukt_ref[...], ckv_b, preferred_element_type=F32)
    vt = jnp.dot(wuvt_ref[...], ckv_b, preferred_element_type=F32)
    krt_b = krt.astype(BF16)
    for hd in range(N_HEADS):
        blk = ktn[hd * NOPE_DIM:(hd + 1) * NOPE_DIM]
        kt_ref[0, hd, 0:NOPE_DIM, :] = (blk * _rms_rows(blk) * gkn_ref[...]).astype(BF16)
        kt_ref[0, hd, NOPE_DIM:HEAD_QK, :] = krt_b
        vt_ref[0, hd] = vt[hd * V_DIM:(hd + 1) * V_DIM].astype(BF16)


def _proj_prompt(x, w, cos_t, sin_t):
    b, s, d = x.shape
    tm = min(PROJ_TILE, s)
    assert s % tm == 0
    c = w["conv_w"].shape[1]
    kv_rank = w["g_kv"].shape[0]
    full = lambda a: pl.BlockSpec(a.shape, lambda bi, si: (0,) * a.ndim)
    weights = [w["g_mix"], w["win_c"], w["win_at"], w["conv_w"], w["conv_b"], w["g_out_c"],
               w["g_qlat"], w["wuq_t"], w["g_kv"], w["g_qn"], w["g_qr"], w["g_kn"], w["g_kr"],
               w["wuk_t"], w["wuv_t"]]
    in_specs = ([pl.BlockSpec((1, tm, d), lambda bi, si: (bi, si, 0))] + [full(a) for a in weights]
                + [pl.BlockSpec((HALF_ROPE, tm), lambda bi, si: (0, si))] * 2)
    out_shape = [
        jax.ShapeDtypeStruct((b, s, c), BF16),
        jax.ShapeDtypeStruct((b, N_HEADS, HEAD_QK, s), BF16),
        jax.ShapeDtypeStruct((b, N_HEADS, HEAD_QK, s), BF16),
        jax.ShapeDtypeStruct((b, N_HEADS, V_DIM, s), BF16),
        jax.ShapeDtypeStruct((b, s, kv_rank), F32),
        jax.ShapeDtypeStruct((b, s, ROPE_DIM), F32),
        jax.ShapeDtypeStruct((b, CONV_K - 1, c), F32),
    ]
    out_specs = [
        pl.BlockSpec((1, tm, c), lambda bi, si: (bi, si, 0)),
        pl.BlockSpec((1, N_HEADS, HEAD_QK, tm), lambda bi, si: (bi, 0, 0, si)),
        pl.BlockSpec((1, N_HEADS, HEAD_QK, tm), lambda bi, si: (bi, 0, 0, si)),
        pl.BlockSpec((1, N_HEADS, V_DIM, tm), lambda bi, si: (bi, 0, 0, si)),
        pl.BlockSpec((1, tm, kv_rank), lambda bi, si: (bi, si, 0)),
        pl.BlockSpec((1, tm, ROPE_DIM), lambda bi, si: (bi, si, 0)),
        pl.BlockSpec((1, CONV_K - 1, c), lambda bi, si: (bi, 0, 0)),
    ]
    return pl.pallas_call(
        _proj_prompt_kernel,
        grid=(b, s // tm),
        in_specs=in_specs,
        out_specs=out_specs,
        out_shape=out_shape,
        scratch_shapes=[pltpu.VMEM((tm + SUBLANES, c), F32)],
        compiler_params=_params(("arbitrary", "arbitrary")),
        name="proj_prompt",
    )(x, *weights, cos_t, sin_t)


ATTN_FULL, ATTN_MASKED, ATTN_LAST = 0, 1, 2


def _attn_prompt_kernel(qi_ref, ki_ref, kind_ref, qt_ref, kt_ref, vt_ref, gout_ref, y_ref, m_ref, l_ref, acc_ref):
    p = pl.program_id(1)
    qi = qi_ref[p]
    ki = ki_ref[p]
    kind = kind_ref[p]
    tq = qt_ref.shape[3]
    tk = kt_ref.shape[3]

    @pl.when(ki == 0)
    def _():
        m_ref[...] = jnp.full(m_ref.shape, -jnp.inf, F32)
        l_ref[...] = jnp.zeros(l_ref.shape, F32)
        acc_ref[...] = jnp.zeros(acc_ref.shape, F32)

    def block(masked):
        if masked:
            offset = 0 if tq == tk else qi * tq - ki * tk
            visible = (lax.broadcasted_iota(jnp.int32, (tk, tq), 0) - lax.broadcasted_iota(jnp.int32, (tk, tq), 1)
                       <= offset)
        def logits(hd):
            return lax.dot_general(kt_ref[0, hd], qt_ref[0, hd], _TN, preferred_element_type=F32)

        s_next = logits(0)
        for hd in range(N_HEADS):
            s = s_next
            if hd + 1 < N_HEADS:
                s_next = logits(hd + 1)
            if masked:
                s = jnp.where(visible, s, MASK_VALUE)
            m_prev = m_ref[hd]
            m_new = jnp.maximum(m_prev, jnp.max(s, axis=0, keepdims=True))
            alpha = jnp.exp2(m_prev - m_new)
            pr = jnp.exp2(s - m_new)
            l_ref[hd] = alpha * l_ref[hd] + jnp.sum(pr, axis=0, keepdims=True)
            pv = jnp.dot(vt_ref[0, hd], pr.astype(BF16), preferred_element_type=F32)
            acc_ref[hd] = alpha * acc_ref[hd] + pv
            m_ref[hd] = m_new

    @pl.when(kind == ATTN_FULL)
    def _():
        block(False)

    @pl.when(kind != ATTN_FULL)
    def _():
        block(True)

    @pl.when(kind == ATTN_LAST)
    def _():
        outs = []
        for hd in range(N_HEADS):
            o = acc_ref[hd] / l_ref[hd]
            outs.append(o * _rms_rows(o) * gout_ref[hd])
        y_ref[0] = jnp.concatenate(outs, axis=0).T.astype(BF16)


def _attn_prompt(qt, kt, vt, gout_a):
    b, _, _, s = qt.shape
    tq, tk = min(ATTN_Q_TILE, s), min(ATTN_K_TILE, s)
    assert s % tq == 0 and s % tk == 0
    steps = []
    for i in range(s // tq):
        j_last = ((i + 1) * tq - 1) // tk
        for j in range(j_last + 1):
            crosses_diagonal = (j + 1) * tk - 1 > i * tq
            steps.append((i, j, ATTN_LAST if j == j_last else ATTN_MASKED if crosses_diagonal else ATTN_FULL))
    qi_tab, ki_tab, kind_tab = (jnp.asarray([st[k] for st in steps], jnp.int32) for k in range(3))
    grid_spec = pltpu.PrefetchScalarGridSpec(
        num_scalar_prefetch=3,
        grid=(b, len(steps)),
        in_specs=[
            pl.BlockSpec((1, N_HEADS, HEAD_QK, tq), lambda bi, p, qi, ki, kind: (bi, 0, 0, qi[p])),
            pl.BlockSpec((1, N_HEADS, HEAD_QK, tk), lambda bi, p, qi, ki, kind: (bi, 0, 0, ki[p])),
            pl.BlockSpec((1, N_HEADS, V_DIM, tk), lambda bi, p, qi, ki, kind: (bi, 0, 0, ki[p])),
            pl.BlockSpec(gout_a.shape, lambda bi, p, qi, ki, kind: (0, 0, 0)),
        ],
        out_specs=pl.BlockSpec((1, tq, N_HEADS * V_DIM), lambda bi, p, qi, ki, kind: (bi, qi[p], 0)),
        scratch_shapes=[pltpu.VMEM((N_HEADS, 1, tq), F32), pltpu.VMEM((N_HEADS, 1, tq), F32),
                        pltpu.VMEM((N_HEADS, V_DIM, tq), F32)],
    )
    return pl.pallas_call(
        _attn_prompt_kernel,
        grid_spec=grid_spec,
        out_shape=jax.ShapeDtypeStruct((b, s, N_HEADS * V_DIM), BF16),
        compiler_params=_params(("arbitrary", "arbitrary")),
        name="attn_prompt",
    )(qi_tab, ki_tab, kind_tab, qt, kt, vt, gout_a)


def _proj_sample_kernel(x_ref, tpos_ref, st1_ref, st2_ref, gmix_ref, win_c_ref, win_at_ref, convw_ref,
                        convb_ref, gout_c_ref, gqlat_ref, wuqt_ref, gkv_ref, gqn_ref, gqr_ref, gkn_ref,
                        gkr_ref, wuk_ref, cos_ref, sin_ref,
                        yconv_ref, u_ref, qa_ref, qr_ref, ckv_ref, krope_ref, ext_ref):
    tm = x_ref.shape[0]
    c = convw_ref.shape[1]
    xf = x_ref[...]
    h = (xf * _rms_lanes(xf) * gmix_ref[...]).astype(BF16)

    zc = jnp.dot(h, win_c_ref[...], preferred_element_type=F32)
    u = zc[:, 2 * c:] * zc[:, :c]
    ext_ref[0:SUBLANES, :] = jnp.zeros((SUBLANES, c), F32)
    ext_ref[SUBLANES:, :] = u
    tpos = tpos_ref[...]
    u_m2 = jnp.where(tpos >= 2, ext_ref[pl.ds(SUBLANES - 2, tm), :], st2_ref[...])
    u_m1 = jnp.where(tpos >= 1, ext_ref[pl.ds(SUBLANES - 1, tm), :], st1_ref[...])
    v = convb_ref[...] + convw_ref[0:1, :] * u_m2 + convw_ref[1:2, :] * u_m1 + convw_ref[2:3, :] * u
    yconv = zc[:, c:2 * c] * v
    yconv_ref[...] = _group_norm_lanes(yconv, gout_c_ref[...]).astype(BF16)
    u_ref[...] = u

    q_nope, q_rope, ckvt, (k1, k2) = _attention_side(
        h, win_at_ref, gqlat_ref, wuqt_ref, gkv_ref, gqn_ref, gqr_ref, gkr_ref, cos_ref[...], sin_ref[...])
    for hd in range(N_HEADS):
        qg = (q_nope[hd] * gkn_ref[...]).astype(BF16)
        qa_t = jnp.dot(wuk_ref[:, hd * NOPE_DIM:(hd + 1) * NOPE_DIM], qg, preferred_element_type=F32)
        qa_ref[hd] = qa_t.T
        qr_ref[hd] = _to_token_major(jnp.concatenate(q_rope[hd], axis=0), ROPE_DIM)
    ckv_ref[...] = ckvt.T
    krope_ref[...] = _to_token_major(jnp.concatenate([k1, k2], axis=0), ROPE_DIM)


def _proj_sample(x, tpos, st1, st2, w, cos_t, sin_t):
    tm, d = x.shape
    c = w["conv_w"].shape[1]
    kv_rank = w["g_kv"].shape[0]
    args = [x, tpos, st1, st2, w["g_mix"], w["win_c"], w["win_at"], w["conv_w"], w["conv_b"], w["g_out_c"],
            w["g_qlat"], w["wuq_t"], w["g_kv"], w["g_qn"], w["g_qr"], w["g_kn"], w["g_kr"], w["wuk"],
            cos_t, sin_t]
    out_shape = [
        jax.ShapeDtypeStruct((tm, c), BF16),
        jax.ShapeDtypeStruct((tm, c), F32),
        jax.ShapeDtypeStruct((N_HEADS, tm, kv_rank), F32),
        jax.ShapeDtypeStruct((N_HEADS, tm, ROPE_DIM), F32),
        jax.ShapeDtypeStruct((tm, kv_rank), F32),
        jax.ShapeDtypeStruct((tm, ROPE_DIM), F32),
    ]
    return pl.pallas_call(
        _proj_sample_kernel,
        out_shape=out_shape,
        scratch_shapes=[pltpu.VMEM((tm + SUBLANES, c), F32)],
        compiler_params=pltpu.CompilerParams(vmem_limit_bytes=VMEM_LIMIT_BYTES),
        name="proj_sample",
    )(*args)


def _attn_decode_kernel(pt_ref, ckv_hbm, kr_hbm, qa_ref, qr_ref, cnew_ref, krnew_ref, wukt_ref, wuv_ref,
                        gout_ref, y_ref, cbuf, kbuf, sem, *, layer, n_pages, page, t_dec):
    b = pl.program_id(0)
    n_seq = pl.num_programs(0)
    chunk_pages = cbuf.shape[1] // page
    n_chunks = n_pages // chunk_pages
    rows = N_HEADS * t_dec

    def copies(first_page, slot):
        out = []
        for pg in range(chunk_pages):
            pid = 0 if first_page is None else pt_ref[first_page + pg]
            out.append(pltpu.make_async_copy(ckv_hbm.at[layer, pid], cbuf.at[slot, pl.ds(pg * page, page)], sem.at[0, slot]))
            out.append(pltpu.make_async_copy(kr_hbm.at[layer, pid], kbuf.at[slot, :, pl.ds(pg * page, page)],
                                             sem.at[1, slot]))
        return out

    qa = qa_ref[...].reshape(rows, qa_ref.shape[2]).astype(BF16)
    qr = qr_ref[...].reshape(rows, ROPE_DIM).astype(BF16)

    n_up = wukt_ref.shape[0]
    wq = jnp.concatenate([wukt_ref[...], qa], axis=0)

    def nope_scores(c_b):
        both = lax.dot_general(wq, c_b, _NT, preferred_element_type=F32)
        rs = []
        for hd in range(N_HEADS):
            blk = both[hd * NOPE_DIM:(hd + 1) * NOPE_DIM]
            rs.append(jnp.broadcast_to(_rms_rows(blk), (t_dec, blk.shape[1])))
        return both[n_up:] * jnp.concatenate(rs, axis=0)

    def update(carry, s, c_b):
        m_prev, l_prev, acc = carry
        m_new = jnp.maximum(m_prev, jnp.max(s, axis=-1, keepdims=True))
        alpha = jnp.exp2(m_prev - m_new)
        pr = jnp.exp2(s - m_new)
        l_new = alpha * l_prev + jnp.sum(pr, axis=-1, keepdims=True)
        acc = alpha * acc + jnp.dot(pr.astype(BF16), c_b, preferred_element_type=F32)
        return m_new, l_new, acc

    last = n_seq * n_chunks - 1
    g0 = b * n_chunks

    def fetch(g):
        for cp in copies(jnp.minimum(g, last) * chunk_pages, lax.rem(g, RING)):
            cp.start()

    def arrive(g):
        for cp in copies(None, lax.rem(g, RING)):
            cp.wait()

    def chunk_scores(g):
        slot = lax.rem(g, RING)
        s_rope = jnp.dot(qr, kbuf[slot].astype(BF16), preferred_element_type=F32)
        return nope_scores(cbuf[slot].astype(BF16)) + s_rope

    def chunk_update(carry, s, g):
        return update(carry, s, cbuf[lax.rem(g, RING)].astype(BF16))

    ahead = RING - 1

    @pl.when(b == 0)
    def _():
        for k in range(ahead):
            fetch(k)

    arrive(g0)
    s_first = chunk_scores(g0)

    def body(j, state):
        carry, s = state
        g = g0 + j
        fetch(g + ahead)
        arrive(g + 1)
        s_next = chunk_scores(g + 1)
        return chunk_update(carry, s, g), s_next

    init = (jnp.full((rows, 1), -jnp.inf, F32), jnp.zeros((rows, 1), F32), jnp.zeros((rows, cbuf.shape[2]), F32))
    carry, s_last = lax.fori_loop(0, n_chunks - 1, body, (init, s_first))
    fetch(g0 + n_chunks - 1 + ahead)
    carry = chunk_update(carry, s_last, g0 + n_chunks - 1)

    @pl.when(b == n_seq - 1)
    def _():
        for k in range(1, ahead + 1):
            arrive(last + k)

    pad = LANES - t_dec
    c_new = jnp.concatenate([cnew_ref[...], jnp.zeros((pad, cnew_ref.shape[1]), F32)], axis=0).astype(BF16)
    kr_new = jnp.concatenate([krnew_ref[...], jnp.zeros((pad, ROPE_DIM), F32)], axis=0).astype(BF16)
    s_new = nope_scores(c_new) + lax.dot_general(qr, kr_new, _NT, preferred_element_type=F32)
    q_t = lax.rem(lax.broadcasted_iota(jnp.int32, (rows, LANES), 0), t_dec)
    key = lax.broadcasted_iota(jnp.int32, (rows, LANES), 1)
    s_new = jnp.where(key <= q_t, s_new, MASK_VALUE)
    _, l_fin, acc = update(carry, s_new, c_new)

    o_lat = (acc / l_fin).astype(BF16)
    ov = jnp.dot(o_lat, wuv_ref[...], preferred_element_type=F32)
    outs = []
    for hd in range(N_HEADS):
        o = ov[hd * t_dec:(hd + 1) * t_dec, hd * V_DIM:(hd + 1) * V_DIM]
        outs.append(o * _rms_lanes(o) * gout_ref[hd])
    y_ref[...] = jnp.concatenate(outs, axis=1)


def _attn_decode(layer, page_table, cache_ckv, cache_kr, qa, qr, c_new, kr_new, w, t_dec):
    n_seq, n_pages = page_table.shape
    _, _, page, kv_rank = cache_ckv.shape
    chunk_pages = min(DECODE_CHUNK_PAGES, n_pages)
    assert n_pages % chunk_pages == 0 and t_dec % SUBLANES == 0 and t_dec <= LANES
    chunk = chunk_pages * page
    grid_spec = pltpu.PrefetchScalarGridSpec(
        num_scalar_prefetch=1,
        grid=(n_seq,),
        in_specs=[
            pl.BlockSpec(memory_space=pl.ANY),
            pl.BlockSpec(memory_space=pl.ANY),
            pl.BlockSpec((N_HEADS, t_dec, kv_rank), lambda b, pt: (0, b, 0)),
            pl.BlockSpec((N_HEADS, t_dec, ROPE_DIM), lambda b, pt: (0, b, 0)),
            pl.BlockSpec((t_dec, kv_rank), lambda b, pt: (b, 0)),
            pl.BlockSpec((t_dec, ROPE_DIM), lambda b, pt: (b, 0)),
            pl.BlockSpec(w["wuk_t"].shape, lambda b, pt: (0, 0)),
            pl.BlockSpec(w["wuv"].shape, lambda b, pt: (0, 0)),
            pl.BlockSpec(w["g_out_a_row"].shape, lambda b, pt: (0, 0, 0)),
        ],
        out_specs=pl.BlockSpec((t_dec, N_HEADS * V_DIM), lambda b, pt: (b, 0)),
        scratch_shapes=[pltpu.VMEM((RING, chunk, kv_rank), F32), pltpu.VMEM((RING, ROPE_DIM, chunk), F32),
                        pltpu.SemaphoreType.DMA((2, RING))],
    )
    return pl.pallas_call(
        functools.partial(_attn_decode_kernel, layer=layer, n_pages=n_pages, page=page, t_dec=t_dec),
        grid_spec=grid_spec,
        out_shape=jax.ShapeDtypeStruct((n_seq * t_dec, N_HEADS * V_DIM), F32),
        compiler_params=_params(("arbitrary",)),
        name="attn_decode",
    )(page_table.reshape(-1), cache_ckv, cache_kr, qa, qr, c_new, kr_new, w["wuk_t"], w["wuv"], w["g_out_a_row"])


ROUTE_E0, ROUTE_E1, ROUTE_R0, ROUTE_R1, ROUTE_W0, ROUTE_W1 = range(6)


def _lane_pick(x, lane, idx):
    return jnp.sum(jnp.where(lane == idx, x, 0.0), axis=-1, keepdims=True)


def _store_row_tiles(ref, x):
    rows, width = x.shape
    n = width // LANES
    for j in range(n):
        ref[pl.ds(j, rows, stride=n), :] = x[:, j * LANES:(j + 1) * LANES]


def _load_row_tiles(ref, rows, n):
    return jnp.concatenate([ref[pl.ds(j, rows, stride=n), :] for j in range(n)], axis=1)


def _merge_route_kernel(ycp_ref, yap_ref, xp_ref, ycs_ref, yas_ref, xs_ref, wout_ref, gffn_ref, wr_ref, br_ref,
                        xmid_ref, h2_ref, route_ref, counts_ref, carry_ref, *, n_prompt_tiles):
    i = pl.program_id(0)
    tm = xp_ref.shape[0]
    is_p = i < n_prompt_tiles

    @pl.when(i == 0)
    def _():
        carry_ref[...] = jnp.zeros(carry_ref.shape, F32)

    yc = jnp.where(is_p, ycp_ref[...], ycs_ref[...])
    ya = jnp.where(is_p, yap_ref[...], yas_ref[...].astype(BF16))
    x = jnp.where(is_p, xp_ref[...], xs_ref[...])
    y = jnp.concatenate([yc, ya], axis=1)
    xm = x + jnp.dot(y, wout_ref[...], preferred_element_type=F32)
    xmid_ref[...] = xm
    h2f = xm * _rms_lanes(xm) * gffn_ref[...]
    _store_row_tiles(h2_ref, h2f)
    h2 = h2f.astype(BF16)
    logits = jnp.dot(h2, wr_ref[...], preferred_element_type=F32) + br_ref[...]

    lt = logits.T
    epg = EXPERTS_PER_GROUP
    row = lax.broadcasted_iota(jnp.int32, (epg, tm), 0).astype(F32)
    neg = -jnp.inf
    far = float(epg)

    def first_max(x):
        v = jnp.max(x, axis=0, keepdims=True)
        return v, jnp.min(jnp.where(x == v, row, far), axis=0, keepdims=True)

    gl = jnp.where(row < N_GROUPS, lt[N_EXPERTS:N_EXPERTS + epg], neg)
    ge = jnp.exp(gl - jnp.max(gl, axis=0, keepdims=True))
    pg = ge / jnp.sum(ge, axis=0, keepdims=True)
    p_sel, g_sel = first_max(pg)
    v1 = i1 = v2 = i2 = None
    for g in range(N_GROUPS):
        eg = lt[g * epg:(g + 1) * epg]
        a1, j1 = first_max(eg)
        a2, j2 = first_max(jnp.where(row == j1, neg, eg))
        pick = g_sel == g
        v1, i1, v2, i2 = (c if g == 0 else jnp.where(pick, c, p) for c, p in ((a1, v1), (j1, i1), (a2, v2), (j2, i2)))
    e2 = jnp.exp(v2 - v1)
    w0 = 1.0 / (1.0 + e2) * p_sel
    w1 = e2 / (1.0 + e2) * p_sel
    e0 = g_sel * epg + i1
    e1 = g_sel * epg + i2

    erow = lax.broadcasted_iota(jnp.int32, (N_EXPERTS, tm), 0).astype(F32)
    oh0 = erow == e0
    oh1 = erow == e1
    onehot = (oh0 | oh1).astype(BF16)
    earlier = (lax.broadcasted_iota(jnp.int32, (tm, tm), 0) < lax.broadcasted_iota(jnp.int32, (tm, tm), 1)).astype(BF16)
    before = jnp.dot(onehot, earlier, preferred_element_type=F32) + carry_ref[...]
    r0 = jnp.sum(jnp.where(oh0, before, 0.0), axis=0, keepdims=True)
    r1 = jnp.sum(jnp.where(oh1, before, 0.0), axis=0, keepdims=True)
    carry_ref[...] += jnp.sum(onehot.astype(F32), axis=1, keepdims=True)

    zero = jnp.zeros((1, tm), F32)
    route_ref[...] = jnp.concatenate([e0, e1, r0, r1, w0, w1, zero, zero], axis=0)

    @pl.when(i == pl.num_programs(0) - 1)
    def _():
        counts_ref[...] = carry_ref[...]


def _merge_route(ycp, yap, xp, ycs, yas, xs, w):
    tp, d = xp.shape
    ts = xs.shape[0]
    tm = TOKEN_TILE
    assert tp % tm == 0 and ts % tm == 0
    npt, nst = tp // tm, ts // tm
    half = ycp.shape[1]
    pmap = lambda i: (jnp.minimum(i, npt - 1), 0)
    smap = lambda i: (jnp.maximum(i - npt, 0), 0)
    cmap = lambda i: (0, 0)
    return pl.pallas_call(
        functools.partial(_merge_route_kernel, n_prompt_tiles=npt),
        grid=(npt + nst,),
        in_specs=[
            pl.BlockSpec((tm, half), pmap), pl.BlockSpec((tm, half), pmap), pl.BlockSpec((tm, d), pmap),
            pl.BlockSpec((tm, half), smap), pl.BlockSpec((tm, half), smap), pl.BlockSpec((tm, d), smap),
            pl.BlockSpec(w["w_out"].shape, cmap), pl.BlockSpec(w["g_ffn"].shape, cmap),
            pl.BlockSpec(w["w_r"].shape, cmap), pl.BlockSpec(w["b_r"].shape, cmap),
        ],
        out_specs=[pl.BlockSpec((tm, d), lambda i: (i, 0)),
                   pl.BlockSpec((tm * (d // LANES), LANES), lambda i: (i, 0)),
                   pl.BlockSpec((SUBLANES, tm), lambda i: (0, i)),
                   pl.BlockSpec((N_EXPERTS, 1), cmap)],
        out_shape=[jax.ShapeDtypeStruct((tp + ts, d), F32),
                   jax.ShapeDtypeStruct(((tp + ts) * (d // LANES), LANES), F32),
                   jax.ShapeDtypeStruct((SUBLANES, tp + ts), F32),
                   jax.ShapeDtypeStruct((N_EXPERTS, 1), F32)],
        scratch_shapes=[pltpu.VMEM((N_EXPERTS, 1), F32)],
        compiler_params=_params(("arbitrary",)),
        name="merge_route",
    )(ycp, yap, xp, ycs, yas, xs, w["w_out"], w["g_ffn"], w["w_r"], w["b_r"])


def _tile_of(ref, row, n):
    start = row * n
    return ref.at[pl.ds(start if isinstance(row, int) else pl.multiple_of(start, n), n)]


def _pad_fill_copies(base_ref, cnt_ref, tiles_ref, nu_ref, zeros_ref, xs_hbm, sem, tme, n, n_tiles):
    out = []
    for e in range(N_EXPERTS):
        pad = tiles_ref[e] * tme - cnt_ref[e]
        pos = base_ref[e] + cnt_ref[e]
        bit = tme // 2
        while bit >= 1:
            take = pad & bit
            out.append((take != 0, pltpu.make_async_copy(
                zeros_ref.at[pl.ds(0, bit * n)], xs_hbm.at[pl.ds(pl.multiple_of(pos * n, n), bit * n)], sem)))
            pos = pos + take
            bit //= 2
    for k in range(N_EXPERTS + 1):
        tile = nu_ref[0] + k
        out.append((tile < n_tiles, pltpu.make_async_copy(
            zeros_ref, xs_hbm.at[pl.ds(pl.multiple_of(jnp.minimum(tile, n_tiles - 1) * (tme * n), tme * n), tme * n)],
            sem)))
    return out


def _dispatch_kernel(base_ref, cnt_ref, tiles_ref, nu_ref, s0_ref, s1_ref, h2_ref, xs_hbm, zeros, sem,
                     *, tm, tme, n, n_tiles):
    i = pl.program_id(0)
    for t0 in range(0, tm, SLOT_BATCH):
        slots = [(s0_ref[0, 0, t], s1_ref[0, 0, t]) for t in range(t0, t0 + SLOT_BATCH)]
        for t, pair in zip(range(t0, t0 + SLOT_BATCH), slots):
            for k, s in enumerate(pair):
                pltpu.make_async_copy(_tile_of(h2_ref, t, n), _tile_of(xs_hbm, s, n), sem.at[0]).start(priority=k)

    def fills():
        return _pad_fill_copies(base_ref, cnt_ref, tiles_ref, nu_ref, zeros, xs_hbm, sem.at[1], tme, n, n_tiles)

    @pl.when(i == 0)
    def _():
        zeros[...] = jnp.zeros(zeros.shape, F32)
        for pred, cp in fills():
            pl.when(pred)(cp.start)

    for t in range(2 * tm):
        pltpu.make_async_copy(_tile_of(h2_ref, 0, n), _tile_of(xs_hbm, 0, n), sem.at[0]).wait()

    @pl.when(i == 0)
    def _():
        for pred, cp in fills():
            pl.when(pred)(cp.wait)


def _dispatch(plan, h2_tiles, n_slots):
    base, cnt, tiles, n_used, slot0, slot1 = plan
    n_tok_tiles, _, tm = slot0.shape
    n = h2_tiles.shape[0] // (n_tok_tiles * tm)
    tme = EXPERT_TILE
    n_tiles = n_slots // tme
    smem_blk = pl.BlockSpec((1, 1, tm), lambda i, *_: (i, 0, 0), memory_space=pltpu.SMEM)
    grid_spec = pltpu.PrefetchScalarGridSpec(
        num_scalar_prefetch=4,
        grid=(n_tok_tiles,),
        in_specs=[smem_blk] * 2 + [pl.BlockSpec((tm * n, LANES), lambda i, *_: (i, 0))],
        out_specs=pl.BlockSpec(memory_space=pl.ANY),
        scratch_shapes=[pltpu.VMEM((tme * n, LANES), F32), pltpu.SemaphoreType.DMA((2,))],
    )
    return pl.pallas_call(
        functools.partial(_dispatch_kernel, tm=tm, tme=tme, n=n, n_tiles=n_tiles),
        grid_spec=grid_spec,
        out_shape=jax.ShapeDtypeStruct((n_slots * n, LANES), F32),
        compiler_params=_params(("arbitrary",)),
        name="dispatch",
    )(base, cnt, tiles, n_used, slot0, slot1, h2_tiles)


def _experts_kernel(te_ref, nu_ref, xs_ref, wg_ref, wu_ref, wd_ref, ys_ref, wgu_bf, wd_bf):
    i = pl.program_id(0)
    n_used = nu_ref[0]
    d_exp = wd_ref.shape[2]
    d = wd_ref.shape[3]
    n = d // LANES
    tme = xs_ref.shape[0] // n
    new_expert = (i == 0) | (te_ref[i] != te_ref[jnp.maximum(i - 1, 0)])

    @pl.when(new_expert & (i < n_used))
    def _():
        wgu_bf[:, :d_exp] = wg_ref[0, 0].astype(BF16)
        wgu_bf[:, d_exp:] = wu_ref[0, 0].astype(BF16)
        wd_bf[...] = wd_ref[0, 0].astype(BF16)

    @pl.when(i < n_used)
    def _():
        h2 = _load_row_tiles(xs_ref, tme, n).astype(BF16)
        gu = jnp.dot(h2, wgu_bf[...], preferred_element_type=F32)
        g = gu[:, :d_exp]
        a = (g / (1.0 + jnp.exp(-g))) * gu[:, d_exp:]
        _store_row_tiles(ys_ref, jnp.dot(a.astype(BF16), wd_bf[...], preferred_element_type=F32))

    @pl.when(i >= n_used)
    def _():
        ys_ref[...] = jnp.zeros(ys_ref.shape, F32)


def _experts(layer, tile_expert, n_used, xs_tiles, w_gate, w_up, w_down):
    n_tiles = tile_expert.shape[0]
    _, _, d, d_exp = w_gate.shape
    blk = xs_tiles.shape[0] // n_tiles
    grid_spec = pltpu.PrefetchScalarGridSpec(
        num_scalar_prefetch=2,
        grid=(n_tiles,),
        in_specs=[
            pl.BlockSpec((blk, LANES), lambda i, te, nu: (jnp.minimum(i, nu[0] - 1), 0)),
            pl.BlockSpec((1, 1, d, d_exp), lambda i, te, nu: (layer, te[i], 0, 0)),
            pl.BlockSpec((1, 1, d, d_exp), lambda i, te, nu: (layer, te[i], 0, 0)),
            pl.BlockSpec((1, 1, d_exp, d), lambda i, te, nu: (layer, te[i], 0, 0)),
        ],
        out_specs=pl.BlockSpec((blk, LANES), lambda i, te, nu: (i, 0)),
        scratch_shapes=[pltpu.VMEM((d, 2 * d_exp), BF16), pltpu.VMEM((d_exp, d), BF16)],
    )
    return pl.pallas_call(
        _experts_kernel,
        grid_spec=grid_spec,
        out_shape=jax.ShapeDtypeStruct(xs_tiles.shape, F32),
        compiler_params=_params(("arbitrary",)),
        name="experts",
    )(tile_expert, n_used, xs_tiles, w_gate, w_up, w_down)


def _combine_kernel(s0_ref, s1_ref, s0n_ref, s1n_ref, ys_hbm, xmid_ref, route_ref, yp_ref, ysmp_ref, buf, sem,
                    *, n_prompt_tiles):
    i = pl.program_id(0)
    last = pl.num_programs(0) - 1
    tm, d = xmid_ref.shape
    n = d // LANES
    slot = lax.rem(i, 2)

    def gather(refs, dst_slot):
        s0, s1 = refs
        for t0 in range(0, tm, SLOT_BATCH):
            slots = [(s0[0, 0, t], s1[0, 0, t]) for t in range(t0, t0 + SLOT_BATCH)]
            for t, pair in zip(range(t0, t0 + SLOT_BATCH), slots):
                for k, s in enumerate(pair):
                    pltpu.make_async_copy(_tile_of(ys_hbm, s, n), _tile_of(buf.at[dst_slot, k], t, n),
                                          sem.at[dst_slot]).start(priority=k)

    def gather_wait(dst_slot):
        for t in range(2 * tm):
            pltpu.make_async_copy(_tile_of(ys_hbm, 0, n), _tile_of(buf.at[dst_slot, 0], 0, n), sem.at[dst_slot]).wait()

    @pl.when(i == 0)
    def _():
        gather((s0_ref, s1_ref), 0)

    gather((s0n_ref, s1n_ref), 1 - slot)
    gather_wait(slot)

    rt = route_ref[...]
    cols = jnp.concatenate([rt, jnp.zeros((LANES - rt.shape[0], tm), F32)], axis=0).T
    lane = lax.broadcasted_iota(jnp.int32, cols.shape, 1)
    w0 = _lane_pick(cols, lane, ROUTE_W0)
    w1 = _lane_pick(cols, lane, ROUTE_W1)
    y0 = _load_row_tiles(buf.at[slot, 0], tm, n)
    y1 = _load_row_tiles(buf.at[slot, 1], tm, n)
    out = xmid_ref[...] + (w0 * y0 + w1 * y1)

    @pl.when(i < n_prompt_tiles)
    def _():
        yp_ref[...] = out

    @pl.when(i >= n_prompt_tiles)
    def _():
        ysmp_ref[...] = out

    @pl.when(i == last)
    def _():
        gather_wait(1 - slot)


def _combine(plan, ys_tiles, xmid, route_t, tp):
    slot0, slot1 = plan[4:]
    ttot, d = xmid.shape
    n_tok_tiles, _, tm = slot0.shape
    npt = tp // tm
    n = d // LANES
    cur = pl.BlockSpec((1, 1, tm), lambda i: (i, 0, 0), memory_space=pltpu.SMEM)
    nxt = pl.BlockSpec((1, 1, tm), lambda i: (jnp.minimum(i + 1, n_tok_tiles - 1), 0, 0), memory_space=pltpu.SMEM)
    return pl.pallas_call(
        functools.partial(_combine_kernel, n_prompt_tiles=npt),
        grid=(n_tok_tiles,),
        in_specs=[cur, cur, nxt, nxt,
                  pl.BlockSpec(memory_space=pl.ANY),
                  pl.BlockSpec((tm, d), lambda i: (i, 0)),
                  pl.BlockSpec((SUBLANES, tm), lambda i: (0, i))],
        out_specs=[pl.BlockSpec((tm, d), lambda i: (jnp.minimum(i, npt - 1), 0)),
                   pl.BlockSpec((tm, d), lambda i: (jnp.maximum(i - npt, 0), 0))],
        out_shape=[jax.ShapeDtypeStruct((tp, d), F32), jax.ShapeDtypeStruct((ttot - tp, d), F32)],
        scratch_shapes=[pltpu.VMEM((2, 2, tm * n, LANES), F32), pltpu.SemaphoreType.DMA((2,))],
        compiler_params=_params(("arbitrary",)),
        name="combine",
    )(slot0, slot1, slot0, slot1, ys_tiles, xmid, route_t)


def _rope_tables(pos):
    inv_freq = ROPE_BASE ** (-jnp.arange(HALF_ROPE, dtype=F32) / HALF_ROPE)
    ang = pos.astype(F32)[:, None] * inv_freq[None, :]
    return jnp.cos(ang).T, jnp.sin(ang).T


def _layer_weights(l, g_mix, w_in, conv_w, conv_b, g_q_lat, w_uq, g_kv_lat, w_uk, w_uv, g_q_nope, g_q_rope,
                   g_k_nope, g_k_rope, g_out, w_out, g_ffn, w_router_group, b_router_group, w_router_expert,
                   b_router_expert, w_gate, w_up, w_down):
    c = conv_w.shape[2]
    col = lambda g: g[l].reshape(-1, 1)
    w_r = jnp.concatenate([w_router_expert[l], w_router_group[l]], axis=1)
    b_r = jnp.concatenate([b_router_expert[l], b_router_group[l]])
    return {
        "g_mix": g_mix[l].reshape(1, -1),
        "win_c": w_in[l][:, :3 * c].astype(BF16),
        "win_at": w_in[l][:, 3 * c:].T.astype(BF16),
        "conv_w": conv_w[l],
        "conv_b": conv_b[l].reshape(1, -1),
        "g_out_c": g_out[l][:c].reshape(1, -1),
        "g_out_a": g_out[l][c:].reshape(N_HEADS, V_DIM, 1),
        "g_out_a_row": g_out[l][c:].reshape(N_HEADS, 1, V_DIM),
        "g_qlat": col(g_q_lat), "g_kv": col(g_kv_lat), "g_qn": col(g_q_nope), "g_qr": col(g_q_rope),
        "g_kn": col(g_k_nope), "g_kr": col(g_k_rope),
        "wuq_t": w_uq[l].T.astype(BF16),
        "wuk_t": w_uk[l].T.astype(BF16),
        "wuk": w_uk[l].astype(BF16),
        "wuv_t": w_uv[l].T.astype(BF16),
        "wuv": w_uv[l].astype(BF16),
        "w_out": w_out[l].astype(BF16),
        "g_ffn": g_ffn[l].reshape(1, -1),
        "w_r": jnp.pad(w_r, ((0, 0), (0, LANES - w_r.shape[1]))).astype(BF16),
        "b_r": jnp.pad(b_r, (0, LANES - b_r.shape[0])).reshape(1, -1),
    }


def _moe_plan(route_t, counts):
    ttot = route_t.shape[1]
    tme = EXPERT_TILE
    tm = TOKEN_TILE
    n_tiles = (TOP_K * ttot) // tme + N_EXPERTS + 1
    cnt = counts[:, 0].astype(jnp.int32)
    tiles = (cnt + tme - 1) // tme
    tile_end = jnp.cumsum(tiles)
    base = (tile_end - tiles) * tme
    n_used = tile_end[-1:]
    tile_id = jnp.minimum(jnp.arange(n_tiles, dtype=jnp.int32), n_used[0] - 1)
    tile_expert = jnp.sum((tile_end[None, :] <= tile_id[:, None]).astype(jnp.int32), axis=1)
    ids = route_t[:ROUTE_R1 + 1].astype(jnp.int32)
    expert_ids = jnp.arange(N_EXPERTS, dtype=jnp.int32)[:, None]

    def slots(e, r):
        s = r + jnp.sum(jnp.where(e[None, :] == expert_ids, base[:, None], 0), axis=0)
        return s.reshape(ttot // tm, 1, tm)

    plan = (base, cnt, tiles, n_used, slots(ids[ROUTE_E0], ids[ROUTE_R0]), slots(ids[ROUTE_E1], ids[ROUTE_R1]))
    return plan, tile_expert, n_tiles * tme


def kernel(x_prompt, x_sample, state_conv, cache_ckv, cache_krope, page_table, g_mix, w_in, conv_w, conv_b, g_q_lat, w_uq, g_kv_lat, w_uk, w_uv, g_q_nope, g_q_rope, g_k_nope, g_k_rope, g_out, w_out, g_ffn, w_router_group, b_router_group, w_router_expert, b_router_expert, w_gate, w_up, w_down):
    b_p, s_p, d = x_prompt.shape
    b_s, t_s, _ = x_sample.shape
    depth = g_mix.shape[0]
    c = conv_w.shape[2]
    page = cache_ckv.shape[2]
    past_len = page_table.shape[1] * page
    kv_rank = cache_ckv.shape[3]

    cos_p, sin_p = _rope_tables(jnp.arange(s_p, dtype=jnp.int32))
    cos_s, sin_s = _rope_tables(jnp.tile(past_len + jnp.arange(t_s, dtype=jnp.int32), b_s))
    tpos = jnp.tile(jnp.arange(t_s, dtype=jnp.int32), b_s).reshape(-1, 1)

    krope_pages = jnp.swapaxes(cache_krope, 2, 3)

    xp, xs = x_prompt, x_sample.reshape(b_s * t_s, d)
    outs = [[] for _ in range(6)]
    for l in range(depth):
        w = _layer_weights(l, g_mix, w_in, conv_w, conv_b, g_q_lat, w_uq, g_kv_lat, w_uk, w_uv, g_q_nope,
                           g_q_rope, g_k_nope, g_k_rope, g_out, w_out, g_ffn, w_router_group, b_router_group,
                           w_router_expert, b_router_expert, w_gate, w_up, w_down)
        yconv_p, qt, kt, vt, ckv_p, kr_p, conv_p = _proj_prompt(xp, w, cos_p, sin_p)
        yattn_p = _attn_prompt(qt, kt, vt, w["g_out_a"])
        st = state_conv[l]
        zeros = lambda n: jnp.zeros((b_s, n, c), F32)
        st1 = jnp.concatenate([st[:, CONV_K - 2:], zeros(t_s - 1)], axis=1).reshape(b_s * t_s, c)
        st2 = jnp.concatenate([st, zeros(t_s - (CONV_K - 1))], axis=1).reshape(b_s * t_s, c)
        yconv_s, u_s, qa, qr, ckv_s, kr_s = _proj_sample(xs, tpos, st1, st2, w, cos_s, sin_s)
        yattn_s = _attn_decode(l, page_table, cache_ckv, krope_pages, qa, qr, ckv_s, kr_s, w, t_s)
        xmid, h2_tiles, route_t, counts = _merge_route(
            yconv_p.reshape(b_p * s_p, c), yattn_p.reshape(b_p * s_p, -1), xp.reshape(b_p * s_p, d),
            yconv_s, yattn_s, xs, w)
        plan, tile_expert, n_slots = _moe_plan(route_t, counts)
        xs_tiles = _dispatch(plan, h2_tiles, n_slots)
        ys_tiles = _experts(l, tile_expert, plan[3], xs_tiles, w_gate, w_up, w_down)
        yp, ysmp = _combine(plan, ys_tiles, xmid, route_t, b_p * s_p)
        xp, xs = yp.reshape(b_p, s_p, d), ysmp
        for lst, val in zip(outs, (ckv_p, kr_p, conv_p, ckv_s.reshape(b_s, t_s, kv_rank),
                                   kr_s.reshape(b_s, t_s, ROPE_DIM),
                                   u_s.reshape(b_s, t_s, c)[:, t_s - (CONV_K - 1):])):
            lst.append(val)
    return (xp, xs.reshape(b_s, t_s, d)) + tuple(jnp.stack(o) for o in outs)
```

```python
import functools

import jax
import jax.numpy as jnp
from jax import lax
from jax.experimental import pallas as pl
from jax.experimental.pallas import tpu as pltpu

N_HEADS = 8
NOPE_DIM = 64
ROPE_DIM = 32
V_DIM = 64
HEAD_QK = NOPE_DIM + ROPE_DIM
HALF_ROPE = ROPE_DIM // 2
ROPE_BASE = 10000.0
CONV_K = 3
OUT_GROUP_DIM = 64
N_GROUPS = 4
EXPERTS_PER_GROUP = 8
N_EXPERTS = N_GROUPS * EXPERTS_PER_GROUP
TOP_K = 2
EPS = 1e-6
MASK_VALUE = -1e30
LOG2_E = 1.4426950408889634

LANES = 128
SUBLANES = 8
VMEM_LIMIT_BYTES = 48 * 1024 * 1024

PROJ_TILE = 1024
ATTN_Q_TILE = 512
ATTN_K_TILE = 512
TOKEN_TILE = 256
EXPERT_TILE = 512
DECODE_CHUNK_PAGES = 16
RING = 4
COMBINE_AHEAD = 2
SLOT_BATCH = 8

F32 = jnp.float32
BF16 = jnp.bfloat16

_NT = (((1,), (1,)), ((), ()))
_TN = (((0,), (0,)), ((), ()))


def _params(sem):
    return pltpu.CompilerParams(dimension_semantics=sem, vmem_limit_bytes=VMEM_LIMIT_BYTES)


def _rms_rows(x):
    return lax.rsqrt(jnp.mean(x * x, axis=0, keepdims=True) + EPS)


def _rms_lanes(x):
    return lax.rsqrt(jnp.mean(x * x, axis=-1, keepdims=True) + EPS)


def _group_norm_lanes(y, gain):
    lane = lax.broadcasted_iota(jnp.int32, (1, LANES), 1)
    low = lane < OUT_GROUP_DIM
    outs = []
    for j in range(y.shape[1] // LANES):
        t = y[:, j * LANES:(j + 1) * LANES]
        sq = t * t
        ss_lo = jnp.sum(jnp.where(low, sq, 0.0), axis=-1, keepdims=True)
        ss_hi = jnp.sum(jnp.where(low, 0.0, sq), axis=-1, keepdims=True)
        r = jnp.where(low, lax.rsqrt(ss_lo / OUT_GROUP_DIM + EPS), lax.rsqrt(ss_hi / OUT_GROUP_DIM + EPS))
        outs.append(t * r)
    return jnp.concatenate(outs, axis=1) * gain


def _rope_rows(x, cos, sin):
    x1, x2 = x[:HALF_ROPE], x[HALF_ROPE:]
    return x1 * cos - x2 * sin, x1 * sin + x2 * cos


def _attention_side(h_bf16, win_at_ref, gqlat_ref, wuqt_ref, gkv_ref, gqn_ref, gqr_ref, gkr_ref, cos, sin):
    q_rank = gqlat_ref.shape[0]
    kv_rank = gkv_ref.shape[0]
    scale = HEAD_QK ** -0.5 * LOG2_E
    zat = lax.dot_general(win_at_ref[...], h_bf16, _NT, preferred_element_type=F32)
    qlt = zat[:q_rank]
    kvt = zat[q_rank:q_rank + kv_rank]
    krt = zat[q_rank + kv_rank:]
    qln = (qlt * _rms_rows(qlt) * gqlat_ref[...]).astype(BF16)
    qt = jnp.dot(wuqt_ref[...], qln, preferred_element_type=F32)
    q_nope, q_rope = [], []
    for h in range(N_HEADS):
        nope = qt[h * HEAD_QK:h * HEAD_QK + NOPE_DIM]
        rope = qt[h * HEAD_QK + NOPE_DIM:(h + 1) * HEAD_QK]
        q_nope.append(nope * _rms_rows(nope) * gqn_ref[...] * scale)
        r1, r2 = _rope_rows(rope * _rms_rows(rope) * gqr_ref[...], cos, sin)
        q_rope.append((r1 * scale, r2 * scale))
    ckvt = kvt * _rms_rows(kvt) * gkv_ref[...]
    k1, k2 = _rope_rows(krt * _rms_rows(krt) * gkr_ref[...], cos, sin)
    return q_nope, q_rope, ckvt, (k1, k2)


def _to_token_major(xt, width):
    rows, toks = xt.shape
    if rows < LANES:
        xt = jnp.concatenate([xt, jnp.zeros((LANES - rows, toks), F32)], axis=0)
    return xt.T[:, :width]


def _proj_prompt_kernel(x_ref, gmix_ref, win_c_ref, win_at_ref, convw_ref, convb_ref, gout_c_ref,
                        gqlat_ref, wuqt_ref, gkv_ref, gqn_ref, gqr_ref, gkn_ref, gkr_ref,
                        wukt_ref, wuvt_ref, cos_ref, sin_ref,
                        yconv_ref, qt_ref, kt_ref, vt_ref, ckv_ref, krope_ref, convst_ref,
                        ext_ref):
    si = pl.program_id(1)
    tm = x_ref.shape[1]
    c = convw_ref.shape[1]

    @pl.when(si == 0)
    def _():
        ext_ref[0:SUBLANES, :] = jnp.zeros((SUBLANES, c), F32)

    xf = x_ref[0]
    h = (xf * _rms_lanes(xf) * gmix_ref[...]).astype(BF16)

    zc = jnp.dot(h, win_c_ref[...], preferred_element_type=F32)
    u = zc[:, 2 * c:] * zc[:, :c]
    ext_ref[SUBLANES:, :] = u
    v = (convb_ref[...]
         + convw_ref[0:1, :] * ext_ref[pl.ds(SUBLANES - 2, tm), :]
         + convw_ref[1:2, :] * ext_ref[pl.ds(SUBLANES - 1, tm), :]
         + convw_ref[2:3, :] * u)
    yconv = zc[:, c:2 * c] * v
    yconv_ref[0] = _group_norm_lanes(yconv, gout_c_ref[...]).astype(BF16)
    ext_ref[0:SUBLANES, :] = ext_ref[pl.ds(tm, SUBLANES), :]
    convst_ref[0] = ext_ref[pl.ds(SUBLANES - (CONV_K - 1), CONV_K - 1), :]

    q_nope, q_rope, ckvt, (k1, k2) = _attention_side(
        h, win_at_ref, gqlat_ref, wuqt_ref, gkv_ref, gqn_ref, gqr_ref, gkr_ref, cos_ref[...], sin_ref[...])
    for hd in range(N_HEADS):
        qt_ref[0, hd, 0:NOPE_DIM, :] = q_nope[hd].astype(BF16)
        qt_ref[0, hd, NOPE_DIM:NOPE_DIM + HALF_ROPE, :] = q_rope[hd][0].astype(BF16)
        qt_ref[0, hd, NOPE_DIM + HALF_ROPE:HEAD_QK, :] = q_rope[hd][1].astype(BF16)
    ckv_ref[0] = ckvt.T
    krt = jnp.concatenate([k1, k2], axis=0)
    krope_ref[0] = _to_token_major(krt, ROPE_DIM)
    ckv_b = ckvt.astype(BF16)
    ktn = jnp.dot(wukt_ref[...], ckv_b, preferred_element_type=F32)
    vt = jnp.dot(wuvt_ref[...], ckv_b, preferred_element_type=F32)
    krt_b = krt.astype(BF16)
    for hd in range(N_HEADS):
        blk = ktn[hd * NOPE_DIM:(hd + 1) * NOPE_DIM]
        kt_ref[0, hd, 0:NOPE_DIM, :] = (blk * _rms_rows(blk) * gkn_ref[...]).astype(BF16)
        kt_ref[0, hd, NOPE_DIM:HEAD_QK, :] = krt_b
        vt_ref[0, hd] = vt[hd * V_DIM:(hd + 1) * V_DIM].astype(BF16)


def _proj_prompt(x, w, cos_t, sin_t):
    b, s, d = x.shape
    tm = min(PROJ_TILE, s)
    assert s % tm == 0
    c = w["conv_w"].shape[1]
    kv_rank = w["g_kv"].shape[0]
    full = lambda a: pl.BlockSpec(a.shape, lambda bi, si: (0,) * a.ndim)
    weights = [w["g_mix"], w["win_c"], w["win_at"], w["conv_w"], w["conv_b"], w["g_out_c"],
               w["g_qlat"], w["wuq_t"], w["g_kv"], w["g_qn"], w["g_qr"], w["g_kn"], w["g_kr"],
               w["wuk_t"], w["wuv_t"]]
    in_specs = ([pl.BlockSpec((1, tm, d), lambda bi, si: (bi, si, 0))] + [full(a) for a in weights]
                + [pl.BlockSpec((HALF_ROPE, tm), lambda bi, si: (0, si))] * 2)
    out_shape = [
        jax.ShapeDtypeStruct((b, s, c), BF16),
        jax.ShapeDtypeStruct((b, N_HEADS, HEAD_QK, s), BF16),
        jax.ShapeDtypeStruct((b, N_HEADS, HEAD_QK, s), BF16),
        jax.ShapeDtypeStruct((b, N_HEADS, V_DIM, s), BF16),
        jax.ShapeDtypeStruct((b, s, kv_rank), F32),
        jax.ShapeDtypeStruct((b, s, ROPE_DIM), F32),
        jax.ShapeDtypeStruct((b, CONV_K - 1, c), F32),
    ]
    out_specs = [
        pl.BlockSpec((1, tm, c), lambda bi, si: (bi, si, 0)),
        pl.BlockSpec((1, N_HEADS, HEAD_QK, tm), lambda bi, si: (bi, 0, 0, si)),
        pl.BlockSpec((1, N_HEADS, HEAD_QK, tm), lambda bi, si: (bi, 0, 0, si)),
        pl.BlockSpec((1, N_HEADS, V_DIM, tm), lambda bi, si: (bi, 0, 0, si)),
        pl.BlockSpec((1, tm, kv_rank), lambda bi, si: (bi, si, 0)),
        pl.BlockSpec((1, tm, ROPE_DIM), lambda bi, si: (bi, si, 0)),
        pl.BlockSpec((1, CONV_K - 1, c), lambda bi, si: (bi, 0, 0)),
    ]
    return pl.pallas_call(
        _proj_prompt_kernel,
        grid=(b, s // tm),
        in_specs=in_specs,
        out_specs=out_specs,
        out_shape=out_shape,
        scratch_shapes=[pltpu.VMEM((tm + SUBLANES, c), F32)],
        compiler_params=_params(("arbitrary", "arbitrary")),
        name="proj_prompt",
    )(x, *weights, cos_t, sin_t)


ATTN_FULL, ATTN_MASKED, ATTN_LAST = 0, 1, 2


def _attn_prompt_kernel(qi_ref, ki_ref, kind_ref, qt_ref, kt_ref, vt_ref, gout_ref, y_ref, m_ref, l_ref, acc_ref):
    p = pl.program_id(1)
    qi = qi_ref[p]
    ki = ki_ref[p]
    kind = kind_ref[p]
    tq = qt_ref.shape[3]
    tk = kt_ref.shape[3]

    @pl.when(ki == 0)
    def _():
        m_ref[...] = jnp.full(m_ref.shape, -jnp.inf, F32)
        l_ref[...] = jnp.zeros(l_ref.shape, F32)
        acc_ref[...] = jnp.zeros(acc_ref.shape, F32)

    def block(masked):
        if masked:
            offset = 0 if tq == tk else qi * tq - ki * tk
            visible = (lax.broadcasted_iota(jnp.int32, (tk, tq), 0) - lax.broadcasted_iota(jnp.int32, (tk, tq), 1)
                       <= offset)
        def logits(hd):
            return lax.dot_general(kt_ref[0, hd], qt_ref[0, hd], _TN, preferred_element_type=F32)

        s_next = logits(0)
        for hd in range(N_HEADS):
            s = s_next
            if hd + 1 < N_HEADS:
                s_next = logits(hd + 1)
            if masked:
                s = jnp.where(visible, s, MASK_VALUE)
            m_prev = m_ref[hd]
            m_new = jnp.maximum(m_prev, jnp.max(s, axis=0, keepdims=True))
            alpha = jnp.exp2(m_prev - m_new)
            pr = jnp.exp2(s - m_new)
            l_ref[hd] = alpha * l_ref[hd] + jnp.sum(pr, axis=0, keepdims=True)
            pv = jnp.dot(vt_ref[0, hd], pr.astype(BF16), preferred_element_type=F32)
            acc_ref[hd] = alpha * acc_ref[hd] + pv
            m_ref[hd] = m_new

    @pl.when(kind == ATTN_FULL)
    def _():
        block(False)

    @pl.when(kind != ATTN_FULL)
    def _():
        block(True)

    @pl.when(kind == ATTN_LAST)
    def _():
        outs = []
        for hd in range(N_HEADS):
            o = acc_ref[hd] / l_ref[hd]
            outs.append(o * _rms_rows(o) * gout_ref[hd])
        y_ref[0] = jnp.concatenate(outs, axis=0).T.astype(BF16)


def _attn_prompt(qt, kt, vt, gout_a):
    b, _, _, s = qt.shape
    tq, tk = min(ATTN_Q_TILE, s), min(ATTN_K_TILE, s)
    assert s % tq == 0 and s % tk == 0
    steps = []
    for i in range(s // tq):
        j_last = ((i + 1) * tq - 1) // tk
        for j in range(j_last + 1):
            crosses_diagonal = (j + 1) * tk - 1 > i * tq
            steps.append((i, j, ATTN_LAST if j == j_last else ATTN_MASKED if crosses_diagonal else ATTN_FULL))
    qi_tab, ki_tab, kind_tab = (jnp.asarray([st[k] for st in steps], jnp.int32) for k in range(3))
    grid_spec = pltpu.PrefetchScalarGridSpec(
        num_scalar_prefetch=3,
        grid=(b, len(steps)),
        in_specs=[
            pl.BlockSpec((1, N_HEADS, HEAD_QK, tq), lambda bi, p, qi, ki, kind: (bi, 0, 0, qi[p])),
            pl.BlockSpec((1, N_HEADS, HEAD_QK, tk), lambda bi, p, qi, ki, kind: (bi, 0, 0, ki[p])),
            pl.BlockSpec((1, N_HEADS, V_DIM, tk), lambda bi, p, qi, ki, kind: (bi, 0, 0, ki[p])),
            pl.BlockSpec(gout_a.shape, lambda bi, p, qi, ki, kind: (0, 0, 0)),
        ],
        out_specs=pl.BlockSpec((1, tq, N_HEADS * V_DIM), lambda bi, p, qi, ki, kind: (bi, qi[p], 0)),
        scratch_shapes=[pltpu.VMEM((N_HEADS, 1, tq), F32), pltpu.VMEM((N_HEADS, 1, tq), F32),
                        pltpu.VMEM((N_HEADS, V_DIM, tq), F32)],
    )
    return pl.pallas_call(
        _attn_prompt_kernel,
        grid_spec=grid_spec,
        out_shape=jax.ShapeDtypeStruct((b, s, N_HEADS * V_DIM), BF16),
        compiler_params=_params(("arbitrary", "arbitrary")),
        name="attn_prompt",
    )(qi_tab, ki_tab, kind_tab, qt, kt, vt, gout_a)


def _proj_sample_kernel(x_ref, tpos_ref, st1_ref, st2_ref, gmix_ref, win_c_ref, win_at_ref, convw_ref,
                        convb_ref, gout_c_ref, gqlat_ref, wuqt_ref, gkv_ref, gqn_ref, gqr_ref, gkn_ref,
                        gkr_ref, wuk_ref, cos_ref, sin_ref,
                        yconv_ref, u_ref, qa_ref, qr_ref, ckv_ref, krope_ref, ext_ref):
    tm = x_ref.shape[0]
    c = convw_ref.shape[1]
    xf = x_ref[...]
    h = (xf * _rms_lanes(xf) * gmix_ref[...]).astype(BF16)

    zc = jnp.dot(h, win_c_ref[...], preferred_element_type=F32)
    u = zc[:, 2 * c:] * zc[:, :c]
    ext_ref[0:SUBLANES, :] = jnp.zeros((SUBLANES, c), F32)
    ext_ref[SUBLANES:, :] = u
    tpos = tpos_ref[...]
    u_m2 = jnp.where(tpos >= 2, ext_ref[pl.ds(SUBLANES - 2, tm), :], st2_ref[...])
    u_m1 = jnp.where(tpos >= 1, ext_ref[pl.ds(SUBLANES - 1, tm), :], st1_ref[...])
    v = convb_ref[...] + convw_ref[0:1, :] * u_m2 + convw_ref[1:2, :] * u_m1 + convw_ref[2:3, :] * u
    yconv = zc[:, c:2 * c] * v
    yconv_ref[...] = _group_norm_lanes(yconv, gout_c_ref[...]).astype(BF16)
    u_ref[...] = u

    q_nope, q_rope, ckvt, (k1, k2) = _attention_side(
        h, win_at_ref, gqlat_ref, wuqt_ref, gkv_ref, gqn_ref, gqr_ref, gkr_ref, cos_ref[...], sin_ref[...])
    for hd in range(N_HEADS):
        qg = (q_nope[hd] * gkn_ref[...]).astype(BF16)
        qa_t = jnp.dot(wuk_ref[:, hd * NOPE_DIM:(hd + 1) * NOPE_DIM], qg, preferred_element_type=F32)
        qa_ref[hd] = qa_t.T
        qr_ref[hd] = _to_token_major(jnp.concatenate(q_rope[hd], axis=0), ROPE_DIM)
    ckv_ref[...] = ckvt.T
    krope_ref[...] = _to_token_major(jnp.concatenate([k1, k2], axis=0), ROPE_DIM)


def _proj_sample(x, tpos, st1, st2, w, cos_t, sin_t):
    tm, d = x.shape
    c = w["conv_w"].shape[1]
    kv_rank = w["g_kv"].shape[0]
    args = [x, tpos, st1, st2, w["g_mix"], w["win_c"], w["win_at"], w["conv_w"], w["conv_b"], w["g_out_c"],
            w["g_qlat"], w["wuq_t"], w["g_kv"], w["g_qn"], w["g_qr"], w["g_kn"], w["g_kr"], w["wuk"],
            cos_t, sin_t]
    out_shape = [
        jax.ShapeDtypeStruct((tm, c), BF16),
        jax.ShapeDtypeStruct((tm, c), F32),
        jax.ShapeDtypeStruct((N_HEADS, tm, kv_rank), F32),
        jax.ShapeDtypeStruct((N_HEADS, tm, ROPE_DIM), F32),
        jax.ShapeDtypeStruct((tm, kv_rank), F32),
        jax.ShapeDtypeStruct((tm, ROPE_DIM), F32),
    ]
    return pl.pallas_call(
        _proj_sample_kernel,
        out_shape=out_shape,
        scratch_shapes=[pltpu.VMEM((tm + SUBLANES, c), F32)],
        compiler_params=pltpu.CompilerParams(vmem_limit_bytes=VMEM_LIMIT_BYTES),
        name="proj_sample",
    )(*args)


def _attn_decode_kernel(pt_ref, ckv_hbm, kr_hbm, qa_ref, qr_ref, cnew_ref, krnew_ref, wukt_ref, wuv_ref,
                        gout_ref, y_ref, cbuf, kbuf, sem, *, layer, n_pages, page, t_dec):
    b = pl.program_id(0)
    n_seq = pl.num_programs(0)
    chunk_pages = cbuf.shape[1] // page
    n_chunks = n_pages // chunk_pages
    rows = N_HEADS * t_dec

    def copies(first_page, slot):
        out = []
        for pg in range(chunk_pages):
            pid = 0 if first_page is None else pt_ref[first_page + pg]
            out.append(pltpu.make_async_copy(ckv_hbm.at[layer, pid], cbuf.at[slot, pl.ds(pg * page, page)], sem.at[0, slot]))
            out.append(pltpu.make_async_copy(kr_hbm.at[layer, pid], kbuf.at[slot, :, pl.ds(pg * page, page)],
                                             sem.at[1, slot]))
        return out

    qa = qa_ref[...].reshape(rows, qa_ref.shape[2]).astype(BF16)
    qr = qr_ref[...].reshape(rows, ROPE_DIM).astype(BF16)

    n_up = wukt_ref.shape[0]
    wq = jnp.concatenate([wukt_ref[...], qa], axis=0)

    def nope_scores(c_b):
        both = lax.dot_general(wq, c_b, _NT, preferred_element_type=F32)
        rs = []
        for hd in range(N_HEADS):
            blk = both[hd * NOPE_DIM:(hd + 1) * NOPE_DIM]
            rs.append(jnp.broadcast_to(_rms_rows(blk), (t_dec, blk.shape[1])))
        return both[n_up:] * jnp.concatenate(rs, axis=0)

    def update(carry, s, c_b):
        m_prev, l_prev, acc = carry
        m_new = jnp.maximum(m_prev, jnp.max(s, axis=-1, keepdims=True))
        alpha = jnp.exp2(m_prev - m_new)
        pr = jnp.exp2(s - m_new)
        l_new = alpha * l_prev + jnp.sum(pr, axis=-1, keepdims=True)
        acc = alpha * acc + jnp.dot(pr.astype(BF16), c_b, preferred_element_type=F32)
        return m_new, l_new, acc

    last = n_seq * n_chunks - 1
    g0 = b * n_chunks

    def fetch(g):
        for cp in copies(jnp.minimum(g, last) * chunk_pages, lax.rem(g, RING)):
            cp.start()

    def arrive(g):
        for cp in copies(None, lax.rem(g, RING)):
            cp.wait()

    def chunk_scores(g):
        slot = lax.rem(g, RING)
        s_rope = jnp.dot(qr, kbuf[slot].astype(BF16), preferred_element_type=F32)
        return nope_scores(cbuf[slot].astype(BF16)) + s_rope

    def chunk_update(carry, s, g):
        return update(carry, s, cbuf[lax.rem(g, RING)].astype(BF16))

    ahead = RING - 1

    @pl.when(b == 0)
    def _():
        for k in range(ahead):
            fetch(k)

    arrive(g0)
    s_first = chunk_scores(g0)

    def body(j, state):
        carry, s = state
        g = g0 + j
        fetch(g + ahead)
        arrive(g + 1)
        s_next = chunk_scores(g + 1)
        return chunk_update(carry, s, g), s_next

    init = (jnp.full((rows, 1), -jnp.inf, F32), jnp.zeros((rows, 1), F32), jnp.zeros((rows, cbuf.shape[2]), F32))
    carry, s_last = lax.fori_loop(0, n_chunks - 1, body, (init, s_first))
    fetch(g0 + n_chunks - 1 + ahead)
    carry = chunk_update(carry, s_last, g0 + n_chunks - 1)

    @pl.when(b == n_seq - 1)
    def _():
        for k in range(1, ahead + 1):
            arrive(last + k)

    pad = LANES - t_dec
    c_new = jnp.concatenate([cnew_ref[...], jnp.zeros((pad, cnew_ref.shape[1]), F32)], axis=0).astype(BF16)
    kr_new = jnp.concatenate([krnew_ref[...], jnp.zeros((pad, ROPE_DIM), F32)], axis=0).astype(BF16)
    s_new = nope_scores(c_new) + lax.dot_general(qr, kr_new, _NT, preferred_element_type=F32)
    q_t = lax.rem(lax.broadcasted_iota(jnp.int32, (rows, LANES), 0), t_dec)
    key = lax.broadcasted_iota(jnp.int32, (rows, LANES), 1)
    s_new = jnp.where(key <= q_t, s_new, MASK_VALUE)
    _, l_fin, acc = update(carry, s_new, c_new)

    o_lat = (acc / l_fin).astype(BF16)
    ov = jnp.dot(o_lat, wuv_ref[...], preferred_element_type=F32)
    outs = []
    for hd in range(N_HEADS):
        o = ov[hd * t_dec:(hd + 1) * t_dec, hd * V_DIM:(hd + 1) * V_DIM]
        outs.append(o * _rms_lanes(o) * gout_ref[hd])
    y_ref[...] = jnp.concatenate(outs, axis=1)


def _attn_decode(layer, page_table, cache_ckv, cache_kr, qa, qr, c_new, kr_new, w, t_dec):
    n_seq, n_pages = page_table.shape
    _, _, page, kv_rank = cache_ckv.shape
    chunk_pages = min(DECODE_CHUNK_PAGES, n_pages)
    assert n_pages % chunk_pages == 0 and t_dec % SUBLANES == 0 and t_dec <= LANES
    chunk = chunk_pages * page
    grid_spec = pltpu.PrefetchScalarGridSpec(
        num_scalar_prefetch=1,
        grid=(n_seq,),
        in_specs=[
            pl.BlockSpec(memory_space=pl.ANY),
            pl.BlockSpec(memory_space=pl.ANY),
            pl.BlockSpec((N_HEADS, t_dec, kv_rank), lambda b, pt: (0, b, 0)),
            pl.BlockSpec((N_HEADS, t_dec, ROPE_DIM), lambda b, pt: (0, b, 0)),
            pl.BlockSpec((t_dec, kv_rank), lambda b, pt: (b, 0)),
            pl.BlockSpec((t_dec, ROPE_DIM), lambda b, pt: (b, 0)),
            pl.BlockSpec(w["wuk_t"].shape, lambda b, pt: (0, 0)),
            pl.BlockSpec(w["wuv"].shape, lambda b, pt: (0, 0)),
            pl.BlockSpec(w["g_out_a_row"].shape, lambda b, pt: (0, 0, 0)),
        ],
        out_specs=pl.BlockSpec((t_dec, N_HEADS * V_DIM), lambda b, pt: (b, 0)),
        scratch_shapes=[pltpu.VMEM((RING, chunk, kv_rank), F32), pltpu.VMEM((RING, ROPE_DIM, chunk), F32),
                        pltpu.SemaphoreType.DMA((2, RING))],
    )
    return pl.pallas_call(
        functools.partial(_attn_decode_kernel, layer=layer, n_pages=n_pages, page=page, t_dec=t_dec),
        grid_spec=grid_spec,
        out_shape=jax.ShapeDtypeStruct((n_seq * t_dec, N_HEADS * V_DIM), F32),
        compiler_params=_params(("arbitrary",)),
        name="attn_decode",
    )(page_table.reshape(-1), cache_ckv, cache_kr, qa, qr, c_new, kr_new, w["wuk_t"], w["wuv"], w["g_out_a_row"])


ROUTE_E0, ROUTE_E1, ROUTE_R0, ROUTE_R1, ROUTE_W0, ROUTE_W1 = range(6)


def _lane_pick(x, lane, idx):
    return jnp.sum(jnp.where(lane == idx, x, 0.0), axis=-1, keepdims=True)


def _store_row_tiles(ref, x):
    rows, width = x.shape
    n = width // LANES
    for j in range(n):
        ref[pl.ds(j, rows, stride=n), :] = x[:, j * LANES:(j + 1) * LANES]


def _load_row_tiles(ref, rows, n):
    return jnp.concatenate([ref[pl.ds(j, rows, stride=n), :] for j in range(n)], axis=1)


def _merge_route_kernel(ycp_ref, yap_ref, xp_ref, ycs_ref, yas_ref, xs_ref, wout_ref, gffn_ref, wr_ref, br_ref,
                        xmid_ref, h2_ref, route_ref, counts_ref, carry_ref, *, n_prompt_tiles):
    i = pl.program_id(0)
    tm = xp_ref.shape[0]
    is_p = i < n_prompt_tiles

    @pl.when(i == 0)
    def _():
        carry_ref[...] = jnp.zeros(carry_ref.shape, F32)

    yc = jnp.where(is_p, ycp_ref[...], ycs_ref[...])
    ya = jnp.where(is_p, yap_ref[...], yas_ref[...].astype(BF16))
    x = jnp.where(is_p, xp_ref[...], xs_ref[...])
    y = jnp.concatenate([yc, ya], axis=1)
    xm = x + jnp.dot(y, wout_ref[...], preferred_element_type=F32)
    xmid_ref[...] = xm
    h2f = xm * _rms_lanes(xm) * gffn_ref[...]
    _store_row_tiles(h2_ref, h2f)
    h2 = h2f.astype(BF16)
    logits = jnp.dot(h2, wr_ref[...], preferred_element_type=F32) + br_ref[...]

    lt = logits.T
    epg = EXPERTS_PER_GROUP
    row = lax.broadcasted_iota(jnp.int32, (epg, tm), 0).astype(F32)
    neg = -jnp.inf
    far = float(epg)

    def first_max(x):
        v = jnp.max(x, axis=0, keepdims=True)
        return v, jnp.min(jnp.where(x == v, row, far), axis=0, keepdims=True)

    gl = jnp.where(row < N_GROUPS, lt[N_EXPERTS:N_EXPERTS + epg], neg)
    ge = jnp.exp(gl - jnp.max(gl, axis=0, keepdims=True))
    pg = ge / jnp.sum(ge, axis=0, keepdims=True)
    p_sel, g_sel = first_max(pg)
    v1 = i1 = v2 = i2 = None
    for g in range(N_GROUPS):
        eg = lt[g * epg:(g + 1) * epg]
        a1, j1 = first_max(eg)
        a2, j2 = first_max(jnp.where(row == j1, neg, eg))
        pick = g_sel == g
        v1, i1, v2, i2 = (c if g == 0 else jnp.where(pick, c, p) for c, p in ((a1, v1), (j1, i1), (a2, v2), (j2, i2)))
    e2 = jnp.exp(v2 - v1)
    w0 = 1.0 / (1.0 + e2) * p_sel
    w1 = e2 / (1.0 + e2) * p_sel
    e0 = g_sel * epg + i1
    e1 = g_sel * epg + i2

    erow = lax.broadcasted_iota(jnp.int32, (N_EXPERTS, tm), 0).astype(F32)
    oh0 = erow == e0
    oh1 = erow == e1
    onehot = (oh0 | oh1).astype(BF16)
    earlier = (lax.broadcasted_iota(jnp.int32, (tm, tm), 0) < lax.broadcasted_iota(jnp.int32, (tm, tm), 1)).astype(BF16)
    before = jnp.dot(onehot, earlier, preferred_element_type=F32) + carry_ref[...]
    r0 = jnp.sum(jnp.where(oh0, before, 0.0), axis=0, keepdims=True)
    r1 = jnp.sum(jnp.where(oh1, before, 0.0), axis=0, keepdims=True)
    carry_ref[...] += jnp.sum(onehot.astype(F32), axis=1, keepdims=True)

    zero = jnp.zeros((1, tm), F32)
    route_ref[...] = jnp.concatenate([e0, e1, r0, r1, w0, w1, zero, zero], axis=0)

    @pl.when(i == pl.num_programs(0) - 1)
    def _():
        counts_ref[...] = carry_ref[...]


def _merge_route(ycp, yap, xp, ycs, yas, xs, w):
    tp, d = xp.shape
    ts = xs.shape[0]
    tm = TOKEN_TILE
    assert tp % tm == 0 and ts % tm == 0
    npt, nst = tp // tm, ts // tm
    half = ycp.shape[1]
    pmap = lambda i: (jnp.minimum(i, npt - 1), 0)
    smap = lambda i: (jnp.maximum(i - npt, 0), 0)
    cmap = lambda i: (0, 0)
    return pl.pallas_call(
        functools.partial(_merge_route_kernel, n_prompt_tiles=npt),
        grid=(npt + nst,),
        in_specs=[
            pl.BlockSpec((tm, half), pmap), pl.BlockSpec((tm, half), pmap), pl.BlockSpec((tm, d), pmap),
            pl.BlockSpec((tm, half), smap), pl.BlockSpec((tm, half), smap), pl.BlockSpec((tm, d), smap),
            pl.BlockSpec(w["w_out"].shape, cmap), pl.BlockSpec(w["g_ffn"].shape, cmap),
            pl.BlockSpec(w["w_r"].shape, cmap), pl.BlockSpec(w["b_r"].shape, cmap),
        ],
        out_specs=[pl.BlockSpec((tm, d), lambda i: (i, 0)),
                   pl.BlockSpec((tm * (d // LANES), LANES), lambda i: (i, 0)),
                   pl.BlockSpec((SUBLANES, tm), lambda i: (0, i)),
                   pl.BlockSpec((N_EXPERTS, 1), cmap)],
        out_shape=[jax.ShapeDtypeStruct((tp + ts, d), F32),
                   jax.ShapeDtypeStruct(((tp + ts) * (d // LANES), LANES), F32),
                   jax.ShapeDtypeStruct((SUBLANES, tp + ts), F32),
                   jax.ShapeDtypeStruct((N_EXPERTS, 1), F32)],
        scratch_shapes=[pltpu.VMEM((N_EXPERTS, 1), F32)],
        compiler_params=_params(("arbitrary",)),
        name="merge_route",
    )(ycp, yap, xp, ycs, yas, xs, w["w_out"], w["g_ffn"], w["w_r"], w["b_r"])


def _tile_of(ref, row, n):
    start = row * n
    return ref.at[pl.ds(start if isinstance(row, int) else pl.multiple_of(start, n), n)]


def _pad_fill_copies(base_ref, cnt_ref, tiles_ref, nu_ref, zeros_ref, xs_hbm, sem, tme, n, n_tiles):
    out = []
    for e in range(N_EXPERTS):
        pad = tiles_ref[e] * tme - cnt_ref[e]
        pos = base_ref[e] + cnt_ref[e]
        bit = tme // 2
        while bit >= 1:
            take = pad & bit
            out.append((take != 0, pltpu.make_async_copy(
                zeros_ref.at[pl.ds(0, bit * n)], xs_hbm.at[pl.ds(pl.multiple_of(pos * n, n), bit * n)], sem)))
            pos = pos + take
            bit //= 2
    for k in range(N_EXPERTS + 1):
        tile = nu_ref[0] + k
        out.append((tile < n_tiles, pltpu.make_async_copy(
            zeros_ref, xs_hbm.at[pl.ds(pl.multiple_of(jnp.minimum(tile, n_tiles - 1) * (tme * n), tme * n), tme * n)],
            sem)))
    return out


def _dispatch_kernel(base_ref, cnt_ref, tiles_ref, nu_ref, s0_ref, s1_ref, h2_ref, xs_hbm, stage, zeros, sem,
                     fill_sem, *, tm, tme, n, n_tiles):
    i = pl.program_id(0)
    last = pl.num_programs(0) - 1
    slot = lax.rem(i, 2)

    def drain(sl):
        for t in range(2 * tm):
            pltpu.make_async_copy(_tile_of(stage.at[sl], 0, n), _tile_of(xs_hbm, 0, n), sem.at[sl]).wait()

    @pl.when(i >= 2)
    def _():
        drain(slot)

    stage[slot] = h2_ref[...]
    for t0 in range(0, tm, SLOT_BATCH):
        slots = [(s0_ref[0, 0, t], s1_ref[0, 0, t]) for t in range(t0, t0 + SLOT_BATCH)]
        for t, pair in zip(range(t0, t0 + SLOT_BATCH), slots):
            for k, s in enumerate(pair):
                pltpu.make_async_copy(_tile_of(stage.at[slot], t, n), _tile_of(xs_hbm, s, n),
                                      sem.at[slot]).start(priority=k)

    def fills():
        return _pad_fill_copies(base_ref, cnt_ref, tiles_ref, nu_ref, zeros, xs_hbm, fill_sem, tme, n, n_tiles)

    @pl.when(i == 0)
    def _():
        zeros[...] = jnp.zeros(zeros.shape, F32)
        for pred, cp in fills():
            pl.when(pred)(cp.start)
        for pred, cp in fills():
            pl.when(pred)(cp.wait)

    @pl.when((i == last) & (i >= 1))
    def _():
        drain(1 - slot)

    @pl.when(i == last)
    def _():
        drain(slot)


def _dispatch(plan, h2_tiles, n_slots):
    base, cnt, tiles, n_used, slot0, slot1 = plan
    n_tok_tiles, _, tm = slot0.shape
    n = h2_tiles.shape[0] // (n_tok_tiles * tm)
    tme = EXPERT_TILE
    n_tiles = n_slots // tme
    smem_blk = pl.BlockSpec((1, 1, tm), lambda i, *_: (i, 0, 0), memory_space=pltpu.SMEM)
    grid_spec = pltpu.PrefetchScalarGridSpec(
        num_scalar_prefetch=4,
        grid=(n_tok_tiles,),
        in_specs=[smem_blk] * 2 + [pl.BlockSpec((tm * n, LANES), lambda i, *_: (i, 0))],
        out_specs=pl.BlockSpec(memory_space=pl.ANY),
        scratch_shapes=[pltpu.VMEM((2, tm * n, LANES), F32), pltpu.VMEM((tme * n, LANES), F32),
                        pltpu.SemaphoreType.DMA((2,)), pltpu.SemaphoreType.DMA],
    )
    return pl.pallas_call(
        functools.partial(_dispatch_kernel, tm=tm, tme=tme, n=n, n_tiles=n_tiles),
        grid_spec=grid_spec,
        out_shape=jax.ShapeDtypeStruct((n_slots * n, LANES), F32),
        compiler_params=_params(("arbitrary",)),
        name="dispatch",
    )(base, cnt, tiles, n_used, slot0, slot1, h2_tiles)


def _experts_kernel(te_ref, nu_ref, xs_ref, wg_ref, wu_ref, wd_ref, ys_ref, wgu_bf, wd_bf):
    i = pl.program_id(0)
    n_used = nu_ref[0]
    d_exp = wd_ref.shape[2]
    d = wd_ref.shape[3]
    n = d // LANES
    tme = xs_ref.shape[0] // n
    new_expert = (i == 0) | (te_ref[i] != te_ref[jnp.maximum(i - 1, 0)])

    @pl.when(new_expert & (i < n_used))
    def _():
        wgu_bf[:, :d_exp] = wg_ref[0, 0].astype(BF16)
        wgu_bf[:, d_exp:] = wu_ref[0, 0].astype(BF16)
        wd_bf[...] = wd_ref[0, 0].astype(BF16)

    @pl.when(i < n_used)
    def _():
        h2 = _load_row_tiles(xs_ref, tme, n).astype(BF16)
        gu = jnp.dot(h2, wgu_bf[...], preferred_element_type=F32)
        g = gu[:, :d_exp]
        a = (g / (1.0 + jnp.exp(-g))) * gu[:, d_exp:]
        _store_row_tiles(ys_ref, jnp.dot(a.astype(BF16), wd_bf[...], preferred_element_type=F32))

    @pl.when(i >= n_used)
    def _():
        ys_ref[...] = jnp.zeros(ys_ref.shape, F32)


def _experts(layer, tile_expert, n_used, xs_tiles, w_gate, w_up, w_down):
    n_tiles = tile_expert.shape[0]
    _, _, d, d_exp = w_gate.shape
    blk = xs_tiles.shape[0] // n_tiles
    grid_spec = pltpu.PrefetchScalarGridSpec(
        num_scalar_prefetch=2,
        grid=(n_tiles,),
        in_specs=[
            pl.BlockSpec((blk, LANES), lambda i, te, nu: (jnp.minimum(i, nu[0] - 1), 0)),
            pl.BlockSpec((1, 1, d, d_exp), lambda i, te, nu: (layer, te[i], 0, 0)),
            pl.BlockSpec((1, 1, d, d_exp), lambda i, te, nu: (layer, te[i], 0, 0)),
            pl.BlockSpec((1, 1, d_exp, d), lambda i, te, nu: (layer, te[i], 0, 0)),
        ],
        out_specs=pl.BlockSpec((blk, LANES), lambda i, te, nu: (i, 0)),
        scratch_shapes=[pltpu.VMEM((d, 2 * d_exp), BF16), pltpu.VMEM((d_exp, d), BF16)],
    )
    return pl.pallas_call(
        _experts_kernel,
        grid_spec=grid_spec,
        out_shape=jax.ShapeDtypeStruct(xs_tiles.shape, F32),
        compiler_params=_params(("arbitrary",)),
        name="experts",
    )(tile_expert, n_used, xs_tiles, w_gate, w_up, w_down)


def _combine_kernel(*refs, n_prompt_tiles):
    tables = [refs[2 * a:2 * a + 2] for a in range(COMBINE_AHEAD + 1)]
    ys_hbm, xmid_ref, route_ref, yp_ref, ysmp_ref, buf, sem = refs[2 * (COMBINE_AHEAD + 1):]
    ring = COMBINE_AHEAD + 1
    i = pl.program_id(0)
    last = pl.num_programs(0) - 1
    tm, d = xmid_ref.shape
    n = d // LANES
    slot = lax.rem(i, ring)

    def gather(refs, dst_slot):
        s0, s1 = refs
        for t0 in range(0, tm, SLOT_BATCH):
            slots = [(s0[0, 0, t], s1[0, 0, t]) for t in range(t0, t0 + SLOT_BATCH)]
            for t, pair in zip(range(t0, t0 + SLOT_BATCH), slots):
                for k, s in enumerate(pair):
                    pltpu.make_async_copy(_tile_of(ys_hbm, s, n), _tile_of(buf.at[dst_slot, k], t, n),
                                          sem.at[dst_slot]).start(priority=k)

    def gather_wait(dst_slot):
        for t in range(2 * tm):
            pltpu.make_async_copy(_tile_of(ys_hbm, 0, n), _tile_of(buf.at[dst_slot, 0], 0, n), sem.at[dst_slot]).wait()

    @pl.when(i == 0)
    def _():
        for a in range(COMBINE_AHEAD):
            gather(tables[a], a)

    gather(tables[COMBINE_AHEAD], lax.rem(i + COMBINE_AHEAD, ring))
    gather_wait(slot)

    rt = route_ref[...]
    cols = jnp.concatenate([rt, jnp.zeros((LANES - rt.shape[0], tm), F32)], axis=0).T
    lane = lax.broadcasted_iota(jnp.int32, cols.shape, 1)
    w0 = _lane_pick(cols, lane, ROUTE_W0)
    w1 = _lane_pick(cols, lane, ROUTE_W1)
    y0 = _load_row_tiles(buf.at[slot, 0], tm, n)
    y1 = _load_row_tiles(buf.at[slot, 1], tm, n)
    out = xmid_ref[...] + (w0 * y0 + w1 * y1)

    @pl.when(i < n_prompt_tiles)
    def _():
        yp_ref[...] = out

    @pl.when(i >= n_prompt_tiles)
    def _():
        ysmp_ref[...] = out

    @pl.when(i == last)
    def _():
        for a in range(1, COMBINE_AHEAD + 1):
            gather_wait(lax.rem(i + a, ring))


def _combine(plan, ys_tiles, xmid, route_t, tp):
    slot0, slot1 = plan[4:]
    ttot, d = xmid.shape
    n_tok_tiles, _, tm = slot0.shape
    assert n_tok_tiles >= COMBINE_AHEAD
    npt = tp // tm
    n = d // LANES

    def table(a):
        return pl.BlockSpec((1, 1, tm), lambda i: (jnp.minimum(i + a, n_tok_tiles - 1), 0, 0), memory_space=pltpu.SMEM)

    return pl.pallas_call(
        functools.partial(_combine_kernel, n_prompt_tiles=npt),
        grid=(n_tok_tiles,),
        in_specs=[table(a) for a in range(COMBINE_AHEAD + 1) for _ in range(2)] + [
                  pl.BlockSpec(memory_space=pl.ANY),
                  pl.BlockSpec((tm, d), lambda i: (i, 0)),
                  pl.BlockSpec((SUBLANES, tm), lambda i: (0, i))],
        out_specs=[pl.BlockSpec((tm, d), lambda i: (jnp.minimum(i, npt - 1), 0)),
                   pl.BlockSpec((tm, d), lambda i: (jnp.maximum(i - npt, 0), 0))],
        out_shape=[jax.ShapeDtypeStruct((tp, d), F32), jax.ShapeDtypeStruct((ttot - tp, d), F32)],
        scratch_shapes=[pltpu.VMEM((COMBINE_AHEAD + 1, 2, tm * n, LANES), F32),
                        pltpu.SemaphoreType.DMA((COMBINE_AHEAD + 1,))],
        compiler_params=_params(("arbitrary",)),
        name="combine",
    )(*([slot0, slot1] * (COMBINE_AHEAD + 1)), ys_tiles, xmid, route_t)


def _rope_tables(pos):
    inv_freq = ROPE_BASE ** (-jnp.arange(HALF_ROPE, dtype=F32) / HALF_ROPE)
    ang = pos.astype(F32)[:, None] * inv_freq[None, :]
    return jnp.cos(ang).T, jnp.sin(ang).T


def _layer_weights(l, g_mix, w_in, conv_w, conv_b, g_q_lat, w_uq, g_kv_lat, w_uk, w_uv, g_q_nope, g_q_rope,
                   g_k_nope, g_k_rope, g_out, w_out, g_ffn, w_router_group, b_router_group, w_router_expert,
                   b_router_expert, w_gate, w_up, w_down):
    c = conv_w.shape[2]
    col = lambda g: g[l].reshape(-1, 1)
    w_r = jnp.concatenate([w_router_expert[l], w_router_group[l]], axis=1)
    b_r = jnp.concatenate([b_router_expert[l], b_router_group[l]])
    return {
        "g_mix": g_mix[l].reshape(1, -1),
        "win_c": w_in[l][:, :3 * c].astype(BF16),
        "win_at": w_in[l][:, 3 * c:].T.astype(BF16),
        "conv_w": conv_w[l],
        "conv_b": conv_b[l].reshape(1, -1),
        "g_out_c": g_out[l][:c].reshape(1, -1),
        "g_out_a": g_out[l][c:].reshape(N_HEADS, V_DIM, 1),
        "g_out_a_row": g_out[l][c:].reshape(N_HEADS, 1, V_DIM),
        "g_qlat": col(g_q_lat), "g_kv": col(g_kv_lat), "g_qn": col(g_q_nope), "g_qr": col(g_q_rope),
        "g_kn": col(g_k_nope), "g_kr": col(g_k_rope),
        "wuq_t": w_uq[l].T.astype(BF16),
        "wuk_t": w_uk[l].T.astype(BF16),
        "wuk": w_uk[l].astype(BF16),
        "wuv_t": w_uv[l].T.astype(BF16),
        "wuv": w_uv[l].astype(BF16),
        "w_out": w_out[l].astype(BF16),
        "g_ffn": g_ffn[l].reshape(1, -1),
        "w_r": jnp.pad(w_r, ((0, 0), (0, LANES - w_r.shape[1]))).astype(BF16),
        "b_r": jnp.pad(b_r, (0, LANES - b_r.shape[0])).reshape(1, -1),
    }


def _moe_plan(route_t, counts):
    ttot = route_t.shape[1]
    tme = EXPERT_TILE
    tm = TOKEN_TILE
    n_tiles = (TOP_K * ttot) // tme + N_EXPERTS + 1
    cnt = counts[:, 0].astype(jnp.int32)
    tiles = (cnt + tme - 1) // tme
    tile_end = jnp.cumsum(tiles)
    base = (tile_end - tiles) * tme
    n_used = tile_end[-1:]
    tile_id = jnp.minimum(jnp.arange(n_tiles, dtype=jnp.int32), n_used[0] - 1)
    tile_expert = jnp.sum((tile_end[None, :] <= tile_id[:, None]).astype(jnp.int32), axis=1)
    ids = route_t[:ROUTE_R1 + 1].astype(jnp.int32)
    expert_ids = jnp.arange(N_EXPERTS, dtype=jnp.int32)[:, None]

    def slots(e, r):
        s = r + jnp.sum(jnp.where(e[None, :] == expert_ids, base[:, None], 0), axis=0)
        return s.reshape(ttot // tm, 1, tm)

    plan = (base, cnt, tiles, n_used, slots(ids[ROUTE_E0], ids[ROUTE_R0]), slots(ids[ROUTE_E1], ids[ROUTE_R1]))
    return plan, tile_expert, n_tiles * tme


def kernel(x_prompt, x_sample, state_conv, cache_ckv, cache_krope, page_table, g_mix, w_in, conv_w, conv_b, g_q_lat, w_uq, g_kv_lat, w_uk, w_uv, g_q_nope, g_q_rope, g_k_nope, g_k_rope, g_out, w_out, g_ffn, w_router_group, b_router_group, w_router_expert, b_router_expert, w_gate, w_up, w_down):
    b_p, s_p, d = x_prompt.shape
    b_s, t_s, _ = x_sample.shape
    depth = g_mix.shape[0]
    c = conv_w.shape[2]
    page = cache_ckv.shape[2]
    past_len = page_table.shape[1] * page
    kv_rank = cache_ckv.shape[3]

    cos_p, sin_p = _rope_tables(jnp.arange(s_p, dtype=jnp.int32))
    cos_s, sin_s = _rope_tables(jnp.tile(past_len + jnp.arange(t_s, dtype=jnp.int32), b_s))
    tpos = jnp.tile(jnp.arange(t_s, dtype=jnp.int32), b_s).reshape(-1, 1)

    krope_pages = jnp.swapaxes(cache_krope, 2, 3)

    xp, xs = x_prompt, x_sample.reshape(b_s * t_s, d)
    outs = [[] for _ in range(6)]
    for l in range(depth):
        w = _layer_weights(l, g_mix, w_in, conv_w, conv_b, g_q_lat, w_uq, g_kv_lat, w_uk, w_uv, g_q_nope,
                           g_q_rope, g_k_nope, g_k_rope, g_out, w_out, g_ffn, w_router_group, b_router_group,
                           w_router_expert, b_router_expert, w_gate, w_up, w_down)
        yconv_p, qt, kt, vt, ckv_p, kr_p, conv_p = _proj_prompt(xp, w, cos_p, sin_p)
        yattn_p = _attn_prompt(qt, kt, vt, w["g_out_a"])
        st = state_conv[l]
        zeros = lambda n: jnp.zeros((b_s, n, c), F32)
        st1 = jnp.concatenate([st[:, CONV_K - 2:], zeros(t_s - 1)], axis=1).reshape(b_s * t_s, c)
        st2 = jnp.concatenate([st, zeros(t_s - (CONV_K - 1))], axis=1).reshape(b_s * t_s, c)
        yconv_s, u_s, qa, qr, ckv_s, kr_s = _proj_sample(xs, tpos, st1, st2, w, cos_s, sin_s)
        yattn_s = _attn_decode(l, page_table, cache_ckv, krope_pages, qa, qr, ckv_s, kr_s, w, t_s)
        xmid, h2_tiles, route_t, counts = _merge_route(
            yconv_p.reshape(b_p * s_p, c), yattn_p.reshape(b_p * s_p, -1), xp.reshape(b_p * s_p, d),
            yconv_s, yattn_s, xs, w)
        plan, tile_expert, n_slots = _moe_plan(route_t, counts)
        xs_tiles = _dispatch(plan, h2_tiles, n_slots)
        ys_tiles = _experts(l, tile_expert, plan[3], xs_tiles, w_gate, w_up, w_down)
        yp, ysmp = _combine(plan, ys_tiles, xmid, route_t, b_p * s_p)
        xp, xs = yp.reshape(b_p, s_p, d), ysmp
        for lst, val in zip(outs, (ckv_p, kr_p, conv_p, ckv_s.reshape(b_s, t_s, kv_rank),
                                   kr_s.reshape(b_s, t_s, ROPE_DIM),
                                   u_s.reshape(b_s, t_s, c)[:, t_s - (CONV_K - 1):])):
            lst.append(val)
    return (xp, xs.reshape(b_s, t_s, d)) + tuple(jnp.stack(o) for o in outs)
```

```python
import functools

import jax
import jax.numpy as jnp
from jax import lax
from jax.experimental import pallas as pl
from jax.experimental.pallas import tpu as pltpu

N_HEADS = 8
NOPE_DIM = 64
ROPE_DIM = 32
V_DIM = 64
HEAD_QK = NOPE_DIM + ROPE_DIM
HALF_ROPE = ROPE_DIM // 2
ROPE_BASE = 10000.0
CONV_K = 3
OUT_GROUP_DIM = 64
N_GROUPS = 4
EXPERTS_PER_GROUP = 8
N_EXPERTS = N_GROUPS * EXPERTS_PER_GROUP
TOP_K = 2
EPS = 1e-6
MASK_VALUE = -1e30
LOG2_E = 1.4426950408889634

LANES = 128
SUBLANES = 8
VMEM_LIMIT_BYTES = 48 * 1024 * 1024

PROJ_TILE = 1024
ATTN_Q_TILE = 512
ATTN_K_TILE = 512
TOKEN_TILE = 256
EXPERT_TILE = 512
DECODE_CHUNK_PAGES = 16
RING = 4
COMBINE_AHEAD = 2
SLOT_BATCH = 8

F32 = jnp.float32
BF16 = jnp.bfloat16

_NT = (((1,), (1,)), ((), ()))
_TN = (((0,), (0,)), ((), ()))


def _params(sem):
    return pltpu.CompilerParams(dimension_semantics=sem, vmem_limit_bytes=VMEM_LIMIT_BYTES)


def _rms_rows(x):
    return lax.rsqrt(jnp.mean(x * x, axis=0, keepdims=True) + EPS)


def _rms_lanes(x):
    return lax.rsqrt(jnp.mean(x * x, axis=-1, keepdims=True) + EPS)


def _group_norm_lanes(y, gain):
    lane = lax.broadcasted_iota(jnp.int32, (1, LANES), 1)
    low = lane < OUT_GROUP_DIM
    outs = []
    for j in range(y.shape[1] // LANES):
        t = y[:, j * LANES:(j + 1) * LANES]
        sq = t * t
        ss_lo = jnp.sum(jnp.where(low, sq, 0.0), axis=-1, keepdims=True)
        ss_hi = jnp.sum(jnp.where(low, 0.0, sq), axis=-1, keepdims=True)
        r = jnp.where(low, lax.rsqrt(ss_lo / OUT_GROUP_DIM + EPS), lax.rsqrt(ss_hi / OUT_GROUP_DIM + EPS))
        outs.append(t * r)
    return jnp.concatenate(outs, axis=1) * gain


def _rope_rows(x, cos, sin):
    x1, x2 = x[:HALF_ROPE], x[HALF_ROPE:]
    return x1 * cos - x2 * sin, x1 * sin + x2 * cos


def _attention_side(h_bf16, win_at_ref, gqlat_ref, wuqt_ref, gkv_ref, gqn_ref, gqr_ref, gkr_ref, cos, sin):
    q_rank = gqlat_ref.shape[0]
    kv_rank = gkv_ref.shape[0]
    scale = HEAD_QK ** -0.5 * LOG2_E
    zat = lax.dot_general(win_at_ref[...], h_bf16, _NT, preferred_element_type=F32)
    qlt = zat[:q_rank]
    kvt = zat[q_rank:q_rank + kv_rank]
    krt = zat[q_rank + kv_rank:]
    qln = (qlt * _rms_rows(qlt) * gqlat_ref[...]).astype(BF16)
    qt = jnp.dot(wuqt_ref[...], qln, preferred_element_type=F32)
    q_nope, q_rope = [], []
    for h in range(N_HEADS):
        nope = qt[h * HEAD_QK:h * HEAD_QK + NOPE_DIM]
        rope = qt[h * HEAD_QK + NOPE_DIM:(h + 1) * HEAD_QK]
        q_nope.append(nope * _rms_rows(nope) * gqn_ref[...] * scale)
        r1, r2 = _rope_rows(rope * _rms_rows(rope) * gqr_ref[...], cos, sin)
        q_rope.append((r1 * scale, r2 * scale))
    ckvt = kvt * _rms_rows(kvt) * gkv_ref[...]
    k1, k2 = _rope_rows(krt * _rms_rows(krt) * gkr_ref[...], cos, sin)
    return q_nope, q_rope, ckvt, (k1, k2)


def _to_token_major(xt, width):
    rows, toks = xt.shape
    if rows < LANES:
        xt = jnp.concatenate([xt, jnp.zeros((LANES - rows, toks), F32)], axis=0)
    return xt.T[:, :width]


def _proj_prompt_kernel(x_ref, gmix_ref, win_c_ref, win_at_ref, convw_ref, convb_ref, gout_c_ref,
                        gqlat_ref, wuqt_ref, gkv_ref, gqn_ref, gqr_ref, gkn_ref, gkr_ref,
                        wukt_ref, wuvt_ref, cos_ref, sin_ref,
                        yconv_ref, qt_ref, kt_ref, vt_ref, ckv_ref, krope_ref, convst_ref,
                        ext_ref):
    si = pl.program_id(1)
    tm = x_ref.shape[1]
    c = convw_ref.shape[1]

    @pl.when(si == 0)
    def _():
        ext_ref[0:SUBLANES, :] = jnp.zeros((SUBLANES, c), F32)

    xf = x_ref[0]
    h = (xf * _rms_lanes(xf) * gmix_ref[...]).astype(BF16)

    zc = jnp.dot(h, win_c_ref[...], preferred_element_type=F32)
    u = zc[:, 2 * c:] * zc[:, :c]
    ext_ref[SUBLANES:, :] = u
    v = (convb_ref[...]
         + convw_ref[0:1, :] * ext_ref[pl.ds(SUBLANES - 2, tm), :]
         + convw_ref[1:2, :] * ext_ref[pl.ds(SUBLANES - 1, tm), :]
         + convw_ref[2:3, :] * u)
    yconv = zc[:, c:2 * c] * v
    yconv_ref[0] = _group_norm_lanes(yconv, gout_c_ref[...]).astype(BF16)
    ext_ref[0:SUBLANES, :] = ext_ref[pl.ds(tm, SUBLANES), :]
    convst_ref[0] = ext_ref[pl.ds(SUBLANES - (CONV_K - 1), CONV_K - 1), :]

    q_nope, q_rope, ckvt, (k1, k2) = _attention_side(
        h, win_at_ref, gqlat_ref, wuqt_ref, gkv_ref, gqn_ref, gqr_ref, gkr_ref, cos_ref[...], sin_ref[...])
    for hd in range(N_HEADS):
        qt_ref[0, hd, 0:NOPE_DIM, :] = q_nope[hd].astype(BF16)
        qt_ref[0, hd, NOPE_DIM:NOPE_DIM + HALF_ROPE, :] = q_rope[hd][0].astype(BF16)
        qt_ref[0, hd, NOPE_DIM + HALF_ROPE:HEAD_QK, :] = q_rope[hd][1].astype(BF16)
    ckv_ref[0] = ckvt.T
    krt = jnp.concatenate([k1, k2], axis=0)
    krope_ref[0] = _to_token_major(krt, ROPE_DIM)
    ckv_b = ckvt.astype(BF16)
    ktn = jnp.dot(wukt_ref[...], ckv_b, preferred_element_type=F32)
    vt = jnp.dot(wuvt_ref[...], ckv_b, preferred_element_type=F32)
    krt_b = krt.astype(BF16)
    for hd in range(N_HEADS):
        blk = ktn[hd * NOPE_DIM:(hd + 1) * NOPE_DIM]
        kt_ref[0, hd, 0:NOPE_DIM, :] = (blk * _rms_rows(blk) * gkn_ref[...]).astype(BF16)
        kt_ref[0, hd, NOPE_DIM:HEAD_QK, :] = krt_b
        vt_ref[0, hd] = vt[hd * V_DIM:(hd + 1) * V_DIM].astype(BF16)


def _proj_prompt(x, w, cos_t, sin_t):
    b, s, d = x.shape
    tm = min(PROJ_TILE, s)
    assert s % tm == 0
    c = w["conv_w"].shape[1]
    kv_rank = w["g_kv"].shape[0]
    full = lambda a: pl.BlockSpec(a.shape, lambda bi, si: (0,) * a.ndim)
    weights = [w["g_mix"], w["win_c"], w["win_at"], w["conv_w"], w["conv_b"], w["g_out_c"],
               w["g_qlat"], w["wuq_t"], w["g_kv"], w["g_qn"], w["g_qr"], w["g_kn"], w["g_kr"],
               w["wuk_t"], w["wuv_t"]]
    in_specs = ([pl.BlockSpec((1, tm, d), lambda bi, si: (bi, si, 0))] + [full(a) for a in weights]
                + [pl.BlockSpec((HALF_ROPE, tm), lambda bi, si: (0, si))] * 2)
    out_shape = [
        jax.ShapeDtypeStruct((b, s, c), BF16),
        jax.ShapeDtypeStruct((b, N_HEADS, HEAD_QK, s), BF16),
        jax.ShapeDtypeStruct((b, N_HEADS, HEAD_QK, s), BF16),
        jax.ShapeDtypeStruct((b, N_HEADS, V_DIM, s), BF16),
        jax.ShapeDtypeStruct((b, s, kv_rank), F32),
        jax.ShapeDtypeStruct((b, s, ROPE_DIM), F32),
        jax.ShapeDtypeStruct((b, CONV_K - 1, c), F32),
    ]
    out_specs = [
        pl.BlockSpec((1, tm, c), lambda bi, si: (bi, si, 0)),
        pl.BlockSpec((1, N_HEADS, HEAD_QK, tm), lambda bi, si: (bi, 0, 0, si)),
        pl.BlockSpec((1, N_HEADS, HEAD_QK, tm), lambda bi, si: (bi, 0, 0, si)),
        pl.BlockSpec((1, N_HEADS, V_DIM, tm), lambda bi, si: (bi, 0, 0, si)),
        pl.BlockSpec((1, tm, kv_rank), lambda bi, si: (bi, si, 0)),
        pl.BlockSpec((1, tm, ROPE_DIM), lambda bi, si: (bi, si, 0)),
        pl.BlockSpec((1, CONV_K - 1, c), lambda bi, si: (bi, 0, 0)),
    ]
    return pl.pallas_call(
        _proj_prompt_kernel,
        grid=(b, s // tm),
        in_specs=in_specs,
        out_specs=out_specs,
        out_shape=out_shape,
        scratch_shapes=[pltpu.VMEM((tm + SUBLANES, c), F32)],
        compiler_params=_params(("arbitrary", "arbitrary")),
        name="proj_prompt",
    )(x, *weights, cos_t, sin_t)


ATTN_FULL, ATTN_MASKED, ATTN_LAST = 0, 1, 2


def _attn_prompt_kernel(qi_ref, ki_ref, kind_ref, qt_ref, kt_ref, vt_ref, gout_ref, y_ref, m_ref, l_ref, acc_ref):
    p = pl.program_id(1)
    qi = qi_ref[p]
    ki = ki_ref[p]
    kind = kind_ref[p]
    tq = qt_ref.shape[3]
    tk = kt_ref.shape[3]

    @pl.when(ki == 0)
    def _():
        m_ref[...] = jnp.full(m_ref.shape, -jnp.inf, F32)
        l_ref[...] = jnp.zeros(l_ref.shape, F32)
        acc_ref[...] = jnp.zeros(acc_ref.shape, F32)

    def block(masked):
        if masked:
            offset = 0 if tq == tk else qi * tq - ki * tk
            visible = (lax.broadcasted_iota(jnp.int32, (tk, tq), 0) - lax.broadcasted_iota(jnp.int32, (tk, tq), 1)
                       <= offset)
        def logits(hd):
            return lax.dot_general(kt_ref[0, hd], qt_ref[0, hd], _TN, preferred_element_type=F32)

        s_next = logits(0)
        for hd in range(N_HEADS):
            s = s_next
            if hd + 1 < N_HEADS:
                s_next = logits(hd + 1)
            if masked:
                s = jnp.where(visible, s, MASK_VALUE)
            m_prev = m_ref[hd]
            m_new = jnp.maximum(m_prev, jnp.max(s, axis=0, keepdims=True))
            alpha = jnp.exp2(m_prev - m_new)
            pr = jnp.exp2(s - m_new)
            l_ref[hd] = alpha * l_ref[hd] + jnp.sum(pr, axis=0, keepdims=True)
            pv = jnp.dot(vt_ref[0, hd], pr.astype(BF16), preferred_element_type=F32)
            acc_ref[hd] = alpha * acc_ref[hd] + pv
            m_ref[hd] = m_new

    @pl.when(kind == ATTN_FULL)
    def _():
        block(False)

    @pl.when(kind != ATTN_FULL)
    def _():
        block(True)

    @pl.when(kind == ATTN_LAST)
    def _():
        outs = []
        for hd in range(N_HEADS):
            o = acc_ref[hd] / l_ref[hd]
            outs.append(o * _rms_rows(o) * gout_ref[hd])
        y_ref[0] = jnp.concatenate(outs, axis=0).T.astype(BF16)


def _attn_prompt(qt, kt, vt, gout_a):
    b, _, _, s = qt.shape
    tq, tk = min(ATTN_Q_TILE, s), min(ATTN_K_TILE, s)
    assert s % tq == 0 and s % tk == 0
    steps = []
    for i in range(s // tq):
        j_last = ((i + 1) * tq - 1) // tk
        for j in range(j_last + 1):
            crosses_diagonal = (j + 1) * tk - 1 > i * tq
            steps.append((i, j, ATTN_LAST if j == j_last else ATTN_MASKED if crosses_diagonal else ATTN_FULL))
    qi_tab, ki_tab, kind_tab = (jnp.asarray([st[k] for st in steps], jnp.int32) for k in range(3))
    grid_spec = pltpu.PrefetchScalarGridSpec(
        num_scalar_prefetch=3,
        grid=(b, len(steps)),
        in_specs=[
            pl.BlockSpec((1, N_HEADS, HEAD_QK, tq), lambda bi, p, qi, ki, kind: (bi, 0, 0, qi[p])),
            pl.BlockSpec((1, N_HEADS, HEAD_QK, tk), lambda bi, p, qi, ki, kind: (bi, 0, 0, ki[p])),
            pl.BlockSpec((1, N_HEADS, V_DIM, tk), lambda bi, p, qi, ki, kind: (bi, 0, 0, ki[p])),
            pl.BlockSpec(gout_a.shape, lambda bi, p, qi, ki, kind: (0, 0, 0)),
        ],
        out_specs=pl.BlockSpec((1, tq, N_HEADS * V_DIM), lambda bi, p, qi, ki, kind: (bi, qi[p], 0)),
        scratch_shapes=[pltpu.VMEM((N_HEADS, 1, tq), F32), pltpu.VMEM((N_HEADS, 1, tq), F32),
                        pltpu.VMEM((N_HEADS, V_DIM, tq), F32)],
    )
    return pl.pallas_call(
        _attn_prompt_kernel,
        grid_spec=grid_spec,
        out_shape=jax.ShapeDtypeStruct((b, s, N_HEADS * V_DIM), BF16),
        compiler_params=_params(("arbitrary", "arbitrary")),
        name="attn_prompt",
    )(qi_tab, ki_tab, kind_tab, qt, kt, vt, gout_a)


def _proj_sample_kernel(x_ref, tpos_ref, st1_ref, st2_ref, gmix_ref, win_c_ref, win_at_ref, convw_ref,
                        convb_ref, gout_c_ref, gqlat_ref, wuqt_ref, gkv_ref, gqn_ref, gqr_ref, gkn_ref,
                        gkr_ref, wuk_ref, cos_ref, sin_ref,
                        yconv_ref, u_ref, qa_ref, qr_ref, ckv_ref, krope_ref, ext_ref):
    tm = x_ref.shape[0]
    c = convw_ref.shape[1]
    xf = x_ref[...]
    h = (xf * _rms_lanes(xf) * gmix_ref[...]).astype(BF16)

    zc = jnp.dot(h, win_c_ref[...], preferred_element_type=F32)
    u = zc[:, 2 * c:] * zc[:, :c]
    ext_ref[0:SUBLANES, :] = jnp.zeros((SUBLANES, c), F32)
    ext_ref[SUBLANES:, :] = u
    tpos = tpos_ref[...]
    u_m2 = jnp.where(tpos >= 2, ext_ref[pl.ds(SUBLANES - 2, tm), :], st2_ref[...])
    u_m1 = jnp.where(tpos >= 1, ext_ref[pl.ds(SUBLANES - 1, tm), :], st1_ref[...])
    v = convb_ref[...] + convw_ref[0:1, :] * u_m2 + convw_ref[1:2, :] * u_m1 + convw_ref[2:3, :] * u
    yconv = zc[:, c:2 * c] * v
    yconv_ref[...] = _group_norm_lanes(yconv, gout_c_ref[...]).astype(BF16)
    u_ref[...] = u

    q_nope, q_rope, ckvt, (k1, k2) = _attention_side(
        h, win_at_ref, gqlat_ref, wuqt_ref, gkv_ref, gqn_ref, gqr_ref, gkr_ref, cos_ref[...], sin_ref[...])
    for hd in range(N_HEADS):
        qg = (q_nope[hd] * gkn_ref[...]).astype(BF16)
        qa_t = jnp.dot(wuk_ref[:, hd * NOPE_DIM:(hd + 1) * NOPE_DIM], qg, preferred_element_type=F32)
        qa_ref[hd] = qa_t.T
        qr_ref[hd] = _to_token_major(jnp.concatenate(q_rope[hd], axis=0), ROPE_DIM)
    ckv_ref[...] = ckvt.T
    krope_ref[...] = _to_token_major(jnp.concatenate([k1, k2], axis=0), ROPE_DIM)


def _proj_sample(x, tpos, st1, st2, w, cos_t, sin_t):
    tm, d = x.shape
    c = w["conv_w"].shape[1]
    kv_rank = w["g_kv"].shape[0]
    args = [x, tpos, st1, st2, w["g_mix"], w["win_c"], w["win_at"], w["conv_w"], w["conv_b"], w["g_out_c"],
            w["g_qlat"], w["wuq_t"], w["g_kv"], w["g_qn"], w["g_qr"], w["g_kn"], w["g_kr"], w["wuk"],
            cos_t, sin_t]
    out_shape = [
        jax.ShapeDtypeStruct((tm, c), BF16),
        jax.ShapeDtypeStruct((tm, c), F32),
        jax.ShapeDtypeStruct((N_HEADS, tm, kv_rank), F32),
        jax.ShapeDtypeStruct((N_HEADS, tm, ROPE_DIM), F32),
        jax.ShapeDtypeStruct((tm, kv_rank), F32),
        jax.ShapeDtypeStruct((tm, ROPE_DIM), F32),
    ]
    return pl.pallas_call(
        _proj_sample_kernel,
        out_shape=out_shape,
        scratch_shapes=[pltpu.VMEM((tm + SUBLANES, c), F32)],
        compiler_params=pltpu.CompilerParams(vmem_limit_bytes=VMEM_LIMIT_BYTES),
        name="proj_sample",
    )(*args)


def _attn_decode_kernel(pt_ref, ckv_hbm, kr_hbm, qa_ref, qr_ref, cnew_ref, krnew_ref, wukt_ref, wuv_ref,
                        gout_ref, y_ref, cbuf, kbuf, sem, *, layer, n_pages, page, t_dec):
    b = pl.program_id(0)
    n_seq = pl.num_programs(0)
    chunk_pages = cbuf.shape[1] // page
    n_chunks = n_pages // chunk_pages
    rows = N_HEADS * t_dec

    def copies(first_page, slot):
        out = []
        for pg in range(chunk_pages):
            pid = 0 if first_page is None else pt_ref[first_page + pg]
            out.append(pltpu.make_async_copy(ckv_hbm.at[layer, pid], cbuf.at[slot, pl.ds(pg * page, page)], sem.at[0, slot]))
            out.append(pltpu.make_async_copy(kr_hbm.at[layer, pid], kbuf.at[slot, :, pl.ds(pg * page, page)],
                                             sem.at[1, slot]))
        return out

    qa = qa_ref[...].reshape(rows, qa_ref.shape[2]).astype(BF16)
    qr = qr_ref[...].reshape(rows, ROPE_DIM).astype(BF16)

    n_up = wukt_ref.shape[0]
    wq = jnp.concatenate([wukt_ref[...], qa], axis=0)

    def nope_scores(c_b):
        both = lax.dot_general(wq, c_b, _NT, preferred_element_type=F32)
        rs = []
        for hd in range(N_HEADS):
            blk = both[hd * NOPE_DIM:(hd + 1) * NOPE_DIM]
            rs.append(jnp.broadcast_to(_rms_rows(blk), (t_dec, blk.shape[1])))
        return both[n_up:] * jnp.concatenate(rs, axis=0)

    def update(carry, s, c_b):
        m_prev, l_prev, acc = carry
        m_new = jnp.maximum(m_prev, jnp.max(s, axis=-1, keepdims=True))
        alpha = jnp.exp2(m_prev - m_new)
        pr = jnp.exp2(s - m_new)
        l_new = alpha * l_prev + jnp.sum(pr, axis=-1, keepdims=True)
        acc = alpha * acc + jnp.dot(pr.astype(BF16), c_b, preferred_element_type=F32)
        return m_new, l_new, acc

    last = n_seq * n_chunks - 1
    g0 = b * n_chunks

    def fetch(g):
        for cp in copies(jnp.minimum(g, last) * chunk_pages, lax.rem(g, RING)):
            cp.start()

    def arrive(g):
        for cp in copies(None, lax.rem(g, RING)):
            cp.wait()

    def chunk_scores(g):
        slot = lax.rem(g, RING)
        s_rope = jnp.dot(qr, kbuf[slot].astype(BF16), preferred_element_type=F32)
        return nope_scores(cbuf[slot].astype(BF16)) + s_rope

    def chunk_update(carry, s, g):
        return update(carry, s, cbuf[lax.rem(g, RING)].astype(BF16))

    ahead = RING - 1

    @pl.when(b == 0)
    def _():
        for k in range(ahead):
            fetch(k)

    arrive(g0)
    s_first = chunk_scores(g0)

    def body(j, state):
        carry, s = state
        g = g0 + j
        fetch(g + ahead)
        arrive(g + 1)
        s_next = chunk_scores(g + 1)
        return chunk_update(carry, s, g), s_next

    init = (jnp.full((rows, 1), -jnp.inf, F32), jnp.zeros((rows, 1), F32), jnp.zeros((rows, cbuf.shape[2]), F32))
    carry, s_last = lax.fori_loop(0, n_chunks - 1, body, (init, s_first))
    fetch(g0 + n_chunks - 1 + ahead)
    carry = chunk_update(carry, s_last, g0 + n_chunks - 1)

    @pl.when(b == n_seq - 1)
    def _():
        for k in range(1, ahead + 1):
            arrive(last + k)

    pad = LANES - t_dec
    c_new = jnp.concatenate([cnew_ref[...], jnp.zeros((pad, cnew_ref.shape[1]), F32)], axis=0).astype(BF16)
    kr_new = jnp.concatenate([krnew_ref[...], jnp.zeros((pad, ROPE_DIM), F32)], axis=0).astype(BF16)
    s_new = nope_scores(c_new) + lax.dot_general(qr, kr_new, _NT, preferred_element_type=F32)
    q_t = lax.rem(lax.broadcasted_iota(jnp.int32, (rows, LANES), 0), t_dec)
    key = lax.broadcasted_iota(jnp.int32, (rows, LANES), 1)
    s_new = jnp.where(key <= q_t, s_new, MASK_VALUE)
    _, l_fin, acc = update(carry, s_new, c_new)

    o_lat = (acc / l_fin).astype(BF16)
    ov = jnp.dot(o_lat, wuv_ref[...], preferred_element_type=F32)
    outs = []
    for hd in range(N_HEADS):
        o = ov[hd * t_dec:(hd + 1) * t_dec, hd * V_DIM:(hd + 1) * V_DIM]
        outs.append(o * _rms_lanes(o) * gout_ref[hd])
    y_ref[...] = jnp.concatenate(outs, axis=1)


def _attn_decode(layer, page_table, cache_ckv, cache_kr, qa, qr, c_new, kr_new, w, t_dec):
    n_seq, n_pages = page_table.shape
    _, _, page, kv_rank = cache_ckv.shape
    chunk_pages = min(DECODE_CHUNK_PAGES, n_pages)
    assert n_pages % chunk_pages == 0 and t_dec % SUBLANES == 0 and t_dec <= LANES
    chunk = chunk_pages * page
    grid_spec = pltpu.PrefetchScalarGridSpec(
        num_scalar_prefetch=1,
        grid=(n_seq,),
        in_specs=[
            pl.BlockSpec(memory_space=pl.ANY),
            pl.BlockSpec(memory_space=pl.ANY),
            pl.BlockSpec((N_HEADS, t_dec, kv_rank), lambda b, pt: (0, b, 0)),
            pl.BlockSpec((N_HEADS, t_dec, ROPE_DIM), lambda b, pt: (0, b, 0)),
            pl.BlockSpec((t_dec, kv_rank), lambda b, pt: (b, 0)),
            pl.BlockSpec((t_dec, ROPE_DIM), lambda b, pt: (b, 0)),
            pl.BlockSpec(w["wuk_t"].shape, lambda b, pt: (0, 0)),
            pl.BlockSpec(w["wuv"].shape, lambda b, pt: (0, 0)),
            pl.BlockSpec(w["g_out_a_row"].shape, lambda b, pt: (0, 0, 0)),
        ],
        out_specs=pl.BlockSpec((t_dec, N_HEADS * V_DIM), lambda b, pt: (b, 0)),
        scratch_shapes=[pltpu.VMEM((RING, chunk, kv_rank), F32), pltpu.VMEM((RING, ROPE_DIM, chunk), F32),
                        pltpu.SemaphoreType.DMA((2, RING))],
    )
    return pl.pallas_call(
        functools.partial(_attn_decode_kernel, layer=layer, n_pages=n_pages, page=page, t_dec=t_dec),
        grid_spec=grid_spec,
        out_shape=jax.ShapeDtypeStruct((n_seq * t_dec, N_HEADS * V_DIM), F32),
        compiler_params=_params(("arbitrary",)),
        name="attn_decode",
    )(page_table.reshape(-1), cache_ckv, cache_kr, qa, qr, c_new, kr_new, w["wuk_t"], w["wuv"], w["g_out_a_row"])


ROUTE_E0, ROUTE_E1, ROUTE_R0, ROUTE_R1, ROUTE_W0, ROUTE_W1 = range(6)


def _lane_pick(x, lane, idx):
    return jnp.sum(jnp.where(lane == idx, x, 0.0), axis=-1, keepdims=True)


def _store_row_tiles(ref, x):
    rows, width = x.shape
    n = width // LANES
    for j in range(n):
        ref[pl.ds(j, rows, stride=n), :] = x[:, j * LANES:(j + 1) * LANES]


def _load_row_tiles(ref, rows, n):
    return jnp.concatenate([ref[pl.ds(j, rows, stride=n), :] for j in range(n)], axis=1)


def _merge_route_kernel(ycp_ref, yap_ref, xp_ref, ycs_ref, yas_ref, xs_ref, wout_ref, gffn_ref, wr_ref, br_ref,
                        xmid_ref, h2_ref, route_ref, counts_ref, carry_ref, *, n_prompt_tiles):
    i = pl.program_id(0)
    tm = xp_ref.shape[0]
    is_p = i < n_prompt_tiles

    @pl.when(i == 0)
    def _():
        carry_ref[...] = jnp.zeros(carry_ref.shape, F32)

    yc = jnp.where(is_p, ycp_ref[...], ycs_ref[...])
    ya = jnp.where(is_p, yap_ref[...], yas_ref[...].astype(BF16))
    x = jnp.where(is_p, xp_ref[...], xs_ref[...])
    y = jnp.concatenate([yc, ya], axis=1)
    xm = x + jnp.dot(y, wout_ref[...], preferred_element_type=F32)
    xmid_ref[...] = xm
    h2f = xm * _rms_lanes(xm) * gffn_ref[...]
    h2 = h2f.astype(BF16)
    h2_ref[...] = h2f.reshape(h2_ref.shape).astype(BF16)
    logits = jnp.dot(h2, wr_ref[...], preferred_element_type=F32) + br_ref[...]

    lt = logits.T
    epg = EXPERTS_PER_GROUP
    row = lax.broadcasted_iota(jnp.int32, (epg, tm), 0).astype(F32)
    neg = -jnp.inf
    far = float(epg)

    def first_max(x):
        v = jnp.max(x, axis=0, keepdims=True)
        return v, jnp.min(jnp.where(x == v, row, far), axis=0, keepdims=True)

    gl = jnp.where(row < N_GROUPS, lt[N_EXPERTS:N_EXPERTS + epg], neg)
    ge = jnp.exp(gl - jnp.max(gl, axis=0, keepdims=True))
    pg = ge / jnp.sum(ge, axis=0, keepdims=True)
    p_sel, g_sel = first_max(pg)
    v1 = i1 = v2 = i2 = None
    for g in range(N_GROUPS):
        eg = lt[g * epg:(g + 1) * epg]
        a1, j1 = first_max(eg)
        a2, j2 = first_max(jnp.where(row == j1, neg, eg))
        pick = g_sel == g
        v1, i1, v2, i2 = (c if g == 0 else jnp.where(pick, c, p) for c, p in ((a1, v1), (j1, i1), (a2, v2), (j2, i2)))
    e2 = jnp.exp(v2 - v1)
    w0 = 1.0 / (1.0 + e2) * p_sel
    w1 = e2 / (1.0 + e2) * p_sel
    e0 = g_sel * epg + i1
    e1 = g_sel * epg + i2

    erow = lax.broadcasted_iota(jnp.int32, (N_EXPERTS, tm), 0).astype(F32)
    oh0 = erow == e0
    oh1 = erow == e1
    onehot = (oh0 | oh1).astype(BF16)
    earlier = (lax.broadcasted_iota(jnp.int32, (tm, tm), 0) < lax.broadcasted_iota(jnp.int32, (tm, tm), 1)).astype(BF16)
    before = jnp.dot(onehot, earlier, preferred_element_type=F32) + carry_ref[...]
    r0 = jnp.sum(jnp.where(oh0, before, 0.0), axis=0, keepdims=True)
    r1 = jnp.sum(jnp.where(oh1, before, 0.0), axis=0, keepdims=True)
    carry_ref[...] += jnp.sum(onehot.astype(F32), axis=1, keepdims=True)

    zero = jnp.zeros((1, tm), F32)
    route_ref[...] = jnp.concatenate([e0, e1, r0, r1, w0, w1, zero, zero], axis=0)

    @pl.when(i == pl.num_programs(0) - 1)
    def _():
        counts_ref[...] = carry_ref[...]


def _merge_route(ycp, yap, xp, ycs, yas, xs, w):
    tp, d = xp.shape
    ts = xs.shape[0]
    tm = TOKEN_TILE
    assert tp % tm == 0 and ts % tm == 0
    npt, nst = tp // tm, ts // tm
    half = ycp.shape[1]
    pmap = lambda i: (jnp.minimum(i, npt - 1), 0)
    smap = lambda i: (jnp.maximum(i - npt, 0), 0)
    cmap = lambda i: (0, 0)
    return pl.pallas_call(
        functools.partial(_merge_route_kernel, n_prompt_tiles=npt),
        grid=(npt + nst,),
        in_specs=[
            pl.BlockSpec((tm, half), pmap), pl.BlockSpec((tm, half), pmap), pl.BlockSpec((tm, d), pmap),
            pl.BlockSpec((tm, half), smap), pl.BlockSpec((tm, half), smap), pl.BlockSpec((tm, d), smap),
            pl.BlockSpec(w["w_out"].shape, cmap), pl.BlockSpec(w["g_ffn"].shape, cmap),
            pl.BlockSpec(w["w_r"].shape, cmap), pl.BlockSpec(w["b_r"].shape, cmap),
        ],
        out_specs=[pl.BlockSpec((tm, d), lambda i: (i, 0)),
                   pl.BlockSpec((tm * (d // LANES), LANES), lambda i: (i, 0)),
                   pl.BlockSpec((SUBLANES, tm), lambda i: (0, i)),
                   pl.BlockSpec((N_EXPERTS, 1), cmap)],
        out_shape=[jax.ShapeDtypeStruct((tp + ts, d), F32),
                   jax.ShapeDtypeStruct(((tp + ts) * (d // LANES), LANES), BF16),
                   jax.ShapeDtypeStruct((SUBLANES, tp + ts), F32),
                   jax.ShapeDtypeStruct((N_EXPERTS, 1), F32)],
        scratch_shapes=[pltpu.VMEM((N_EXPERTS, 1), F32)],
        compiler_params=_params(("arbitrary",)),
        name="merge_route",
    )(ycp, yap, xp, ycs, yas, xs, w["w_out"], w["g_ffn"], w["w_r"], w["b_r"])


def _tile_of(ref, row, n):
    start = row * n
    return ref.at[pl.ds(start if isinstance(row, int) else pl.multiple_of(start, n), n)]


def _pad_fill_copies(base_ref, cnt_ref, tiles_ref, nu_ref, zeros_ref, xs_hbm, sem, tme, n, n_tiles):
    out = []
    for e in range(N_EXPERTS):
        pad = tiles_ref[e] * tme - cnt_ref[e]
        pos = base_ref[e] + cnt_ref[e]
        bit = tme // 2
        while bit >= 1:
            take = pad & bit
            out.append((take != 0, pltpu.make_async_copy(
                zeros_ref.at[pl.ds(0, bit * n)], xs_hbm.at[pl.ds(pl.multiple_of(pos * n, n), bit * n)], sem)))
            pos = pos + take
            bit //= 2
    for k in range(N_EXPERTS + 1):
        tile = nu_ref[0] + k
        out.append((tile < n_tiles, pltpu.make_async_copy(
            zeros_ref, xs_hbm.at[pl.ds(pl.multiple_of(jnp.minimum(tile, n_tiles - 1) * (tme * n), tme * n), tme * n)],
            sem)))
    return out


def _dispatch_kernel(base_ref, cnt_ref, tiles_ref, nu_ref, s0_ref, s1_ref, h2_ref, xs_hbm, stage, zeros, sem,
                     fill_sem, *, tm, tme, n, n_tiles):
    i = pl.program_id(0)
    last = pl.num_programs(0) - 1
    slot = lax.rem(i, 2)

    def drain(sl):
        for t in range(2 * tm):
            pltpu.make_async_copy(_tile_of(stage.at[sl], 0, n), _tile_of(xs_hbm, 0, n), sem.at[sl]).wait()

    @pl.when(i >= 2)
    def _():
        drain(slot)

    stage[slot] = h2_ref[...]
    for t0 in range(0, tm, SLOT_BATCH):
        slots = [(s0_ref[0, 0, t], s1_ref[0, 0, t]) for t in range(t0, t0 + SLOT_BATCH)]
        for t, pair in zip(range(t0, t0 + SLOT_BATCH), slots):
            for k, s in enumerate(pair):
                pltpu.make_async_copy(_tile_of(stage.at[slot], t, n), _tile_of(xs_hbm, s, n),
                                      sem.at[slot]).start(priority=k)

    def fills():
        return _pad_fill_copies(base_ref, cnt_ref, tiles_ref, nu_ref, zeros, xs_hbm, fill_sem, tme, n, n_tiles)

    @pl.when(i == 0)
    def _():
        zeros[...] = jnp.zeros(zeros.shape, zeros.dtype)
        for pred, cp in fills():
            pl.when(pred)(cp.start)
        for pred, cp in fills():
            pl.when(pred)(cp.wait)

    @pl.when((i == last) & (i >= 1))
    def _():
        drain(1 - slot)

    @pl.when(i == last)
    def _():
        drain(slot)


def _dispatch(plan, h2_tiles, n_slots):
    base, cnt, tiles, n_used, slot0, slot1 = plan
    n_tok_tiles, _, tm = slot0.shape
    n = h2_tiles.shape[0] // (n_tok_tiles * tm)
    tme = EXPERT_TILE
    n_tiles = n_slots // tme
    smem_blk = pl.BlockSpec((1, 1, tm), lambda i, *_: (i, 0, 0), memory_space=pltpu.SMEM)
    grid_spec = pltpu.PrefetchScalarGridSpec(
        num_scalar_prefetch=4,
        grid=(n_tok_tiles,),
        in_specs=[smem_blk] * 2 + [pl.BlockSpec((tm * n, LANES), lambda i, *_: (i, 0))],
        out_specs=pl.BlockSpec(memory_space=pl.ANY),
        scratch_shapes=[pltpu.VMEM((2, tm * n, LANES), h2_tiles.dtype), pltpu.VMEM((tme * n, LANES), h2_tiles.dtype),
                        pltpu.SemaphoreType.DMA((2,)), pltpu.SemaphoreType.DMA],
    )
    return pl.pallas_call(
        functools.partial(_dispatch_kernel, tm=tm, tme=tme, n=n, n_tiles=n_tiles),
        grid_spec=grid_spec,
        out_shape=jax.ShapeDtypeStruct((n_slots * n, LANES), h2_tiles.dtype),
        compiler_params=_params(("arbitrary",)),
        name="dispatch",
    )(base, cnt, tiles, n_used, slot0, slot1, h2_tiles)


def _experts_kernel(te_ref, nu_ref, xs_ref, wg_ref, wu_ref, wd_ref, ys_ref, wgu_bf, wd_bf):
    i = pl.program_id(0)
    n_used = nu_ref[0]
    d_exp = wd_ref.shape[2]
    d = wd_ref.shape[3]
    n = d // LANES
    tme = ys_ref.shape[0] // n
    new_expert = (i == 0) | (te_ref[i] != te_ref[jnp.maximum(i - 1, 0)])

    @pl.when(new_expert & (i < n_used))
    def _():
        wgu_bf[:, :d_exp] = wg_ref[0, 0].astype(BF16)
        wgu_bf[:, d_exp:] = wu_ref[0, 0].astype(BF16)
        wd_bf[...] = wd_ref[0, 0].astype(BF16)

    @pl.when(i < n_used)
    def _():
        h2 = xs_ref[...].astype(F32).reshape(tme, d).astype(BF16)
        gu = jnp.dot(h2, wgu_bf[...], preferred_element_type=F32)
        g = gu[:, :d_exp]
        a = (g / (1.0 + jnp.exp(-g))) * gu[:, d_exp:]
        _store_row_tiles(ys_ref, jnp.dot(a.astype(BF16), wd_bf[...], preferred_element_type=F32))

    @pl.when(i >= n_used)
    def _():
        ys_ref[...] = jnp.zeros(ys_ref.shape, F32)


def _experts(layer, tile_expert, n_used, xs_tiles, w_gate, w_up, w_down):
    n_tiles = tile_expert.shape[0]
    _, _, d, d_exp = w_gate.shape
    blk_in = blk = xs_tiles.shape[0] // n_tiles
    grid_spec = pltpu.PrefetchScalarGridSpec(
        num_scalar_prefetch=2,
        grid=(n_tiles,),
        in_specs=[
            pl.BlockSpec((blk_in, LANES), lambda i, te, nu: (jnp.minimum(i, nu[0] - 1), 0)),
            pl.BlockSpec((1, 1, d, d_exp), lambda i, te, nu: (layer, te[i], 0, 0)),
            pl.BlockSpec((1, 1, d, d_exp), lambda i, te, nu: (layer, te[i], 0, 0)),
            pl.BlockSpec((1, 1, d_exp, d), lambda i, te, nu: (layer, te[i], 0, 0)),
        ],
        out_specs=pl.BlockSpec((blk, LANES), lambda i, te, nu: (i, 0)),
        scratch_shapes=[pltpu.VMEM((d, 2 * d_exp), BF16), pltpu.VMEM((d_exp, d), BF16)],
    )
    return pl.pallas_call(
        _experts_kernel,
        grid_spec=grid_spec,
        out_shape=jax.ShapeDtypeStruct((n_tiles * blk, LANES), F32),
        compiler_params=_params(("arbitrary",)),
        name="experts",
    )(tile_expert, n_used, xs_tiles, w_gate, w_up, w_down)


def _combine_kernel(*refs, n_prompt_tiles):
    tables = [refs[2 * a:2 * a + 2] for a in range(COMBINE_AHEAD + 1)]
    ys_hbm, xmid_ref, route_ref, yp_ref, ysmp_ref, buf, sem = refs[2 * (COMBINE_AHEAD + 1):]
    ring = COMBINE_AHEAD + 1
    i = pl.program_id(0)
    last = pl.num_programs(0) - 1
    tm, d = xmid_ref.shape
    n = d // LANES
    slot = lax.rem(i, ring)

    def gather(refs, dst_slot):
        s0, s1 = refs
        for t0 in range(0, tm, SLOT_BATCH):
            slots = [(s0[0, 0, t], s1[0, 0, t]) for t in range(t0, t0 + SLOT_BATCH)]
            for t, pair in zip(range(t0, t0 + SLOT_BATCH), slots):
                for k, s in enumerate(pair):
                    pltpu.make_async_copy(_tile_of(ys_hbm, s, n), _tile_of(buf.at[dst_slot, k], t, n),
                                          sem.at[dst_slot]).start(priority=k)

    def gather_wait(dst_slot):
        for t in range(2 * tm):
            pltpu.make_async_copy(_tile_of(ys_hbm, 0, n), _tile_of(buf.at[dst_slot, 0], 0, n), sem.at[dst_slot]).wait()

    @pl.when(i == 0)
    def _():
        for a in range(COMBINE_AHEAD):
            gather(tables[a], a)

    gather(tables[COMBINE_AHEAD], lax.rem(i + COMBINE_AHEAD, ring))
    gather_wait(slot)

    rt = route_ref[...]
    cols = jnp.concatenate([rt, jnp.zeros((LANES - rt.shape[0], tm), F32)], axis=0).T
    lane = lax.broadcasted_iota(jnp.int32, cols.shape, 1)
    w0 = _lane_pick(cols, lane, ROUTE_W0)
    w1 = _lane_pick(cols, lane, ROUTE_W1)
    y0 = _load_row_tiles(buf.at[slot, 0], tm, n)
    y1 = _load_row_tiles(buf.at[slot, 1], tm, n)
    out = xmid_ref[...] + (w0 * y0 + w1 * y1)

    @pl.when(i < n_prompt_tiles)
    def _():
        yp_ref[...] = out

    @pl.when(i >= n_prompt_tiles)
    def _():
        ysmp_ref[...] = out

    @pl.when(i == last)
    def _():
        for a in range(1, COMBINE_AHEAD + 1):
            gather_wait(lax.rem(i + a, ring))


def _combine(plan, ys_tiles, xmid, route_t, tp):
    slot0, slot1 = plan[4:]
    ttot, d = xmid.shape
    n_tok_tiles, _, tm = slot0.shape
    assert n_tok_tiles >= COMBINE_AHEAD
    npt = tp // tm
    n = d // LANES

    def table(a):
        return pl.BlockSpec((1, 1, tm), lambda i: (jnp.minimum(i + a, n_tok_tiles - 1), 0, 0), memory_space=pltpu.SMEM)

    return pl.pallas_call(
        functools.partial(_combine_kernel, n_prompt_tiles=npt),
        grid=(n_tok_tiles,),
        in_specs=[table(a) for a in range(COMBINE_AHEAD + 1) for _ in range(2)] + [
                  pl.BlockSpec(memory_space=pl.ANY),
                  pl.BlockSpec((tm, d), lambda i: (i, 0)),
                  pl.BlockSpec((SUBLANES, tm), lambda i: (0, i))],
        out_specs=[pl.BlockSpec((tm, d), lambda i: (jnp.minimum(i, npt - 1), 0)),
                   pl.BlockSpec((tm, d), lambda i: (jnp.maximum(i - npt, 0), 0))],
        out_shape=[jax.ShapeDtypeStruct((tp, d), F32), jax.ShapeDtypeStruct((ttot - tp, d), F32)],
        scratch_shapes=[pltpu.VMEM((COMBINE_AHEAD + 1, 2, tm * n, LANES), F32),
                        pltpu.SemaphoreType.DMA((COMBINE_AHEAD + 1,))],
        compiler_params=_params(("arbitrary",)),
        name="combine",
    )(*([slot0, slot1] * (COMBINE_AHEAD + 1)), ys_tiles, xmid, route_t)


def _rope_tables(pos):
    inv_freq = ROPE_BASE ** (-jnp.arange(HALF_ROPE, dtype=F32) / HALF_ROPE)
    ang = pos.astype(F32)[:, None] * inv_freq[None, :]
    return jnp.cos(ang).T, jnp.sin(ang).T


def _layer_weights(l, g_mix, w_in, conv_w, conv_b, g_q_lat, w_uq, g_kv_lat, w_uk, w_uv, g_q_nope, g_q_rope,
                   g_k_nope, g_k_rope, g_out, w_out, g_ffn, w_router_group, b_router_group, w_router_expert,
                   b_router_expert, w_gate, w_up, w_down):
    c = conv_w.shape[2]
    col = lambda g: g[l].reshape(-1, 1)
    w_r = jnp.concatenate([w_router_expert[l], w_router_group[l]], axis=1)
    b_r = jnp.concatenate([b_router_expert[l], b_router_group[l]])
    return {
        "g_mix": g_mix[l].reshape(1, -1),
        "win_c": w_in[l][:, :3 * c].astype(BF16),
        "win_at": w_in[l][:, 3 * c:].T.astype(BF16),
        "conv_w": conv_w[l],
        "conv_b": conv_b[l].reshape(1, -1),
        "g_out_c": g_out[l][:c].reshape(1, -1),
        "g_out_a": g_out[l][c:].reshape(N_HEADS, V_DIM, 1),
        "g_out_a_row": g_out[l][c:].reshape(N_HEADS, 1, V_DIM),
        "g_qlat": col(g_q_lat), "g_kv": col(g_kv_lat), "g_qn": col(g_q_nope), "g_qr": col(g_q_rope),
        "g_kn": col(g_k_nope), "g_kr": col(g_k_rope),
        "wuq_t": w_uq[l].T.astype(BF16),
        "wuk_t": w_uk[l].T.astype(BF16),
        "wuk": w_uk[l].astype(BF16),
        "wuv_t": w_uv[l].T.astype(BF16),
        "wuv": w_uv[l].astype(BF16),
        "w_out": w_out[l].astype(BF16),
        "g_ffn": g_ffn[l].reshape(1, -1),
        "w_r": jnp.pad(w_r, ((0, 0), (0, LANES - w_r.shape[1]))).astype(BF16),
        "b_r": jnp.pad(b_r, (0, LANES - b_r.shape[0])).reshape(1, -1),
    }


def _moe_plan(route_t, counts):
    ttot = route_t.shape[1]
    tme = EXPERT_TILE
    tm = TOKEN_TILE
    n_tiles = (TOP_K * ttot) // tme + N_EXPERTS + 1
    cnt = counts[:, 0].astype(jnp.int32)
    tiles = (cnt + tme - 1) // tme
    tile_end = jnp.cumsum(tiles)
    base = (tile_end - tiles) * tme
    n_used = tile_end[-1:]
    tile_id = jnp.minimum(jnp.arange(n_tiles, dtype=jnp.int32), n_used[0] - 1)
    tile_expert = jnp.sum((tile_end[None, :] <= tile_id[:, None]).astype(jnp.int32), axis=1)
    ids = route_t[:ROUTE_R1 + 1].astype(jnp.int32)
    expert_ids = jnp.arange(N_EXPERTS, dtype=jnp.int32)[:, None]

    def slots(e, r):
        s = r + jnp.sum(jnp.where(e[None, :] == expert_ids, base[:, None], 0), axis=0)
        return s.reshape(ttot // tm, 1, tm)

    plan = (base, cnt, tiles, n_used, slots(ids[ROUTE_E0], ids[ROUTE_R0]), slots(ids[ROUTE_E1], ids[ROUTE_R1]))
    return plan, tile_expert, n_tiles * tme


def kernel(x_prompt, x_sample, state_conv, cache_ckv, cache_krope, page_table, g_mix, w_in, conv_w, conv_b, g_q_lat, w_uq, g_kv_lat, w_uk, w_uv, g_q_nope, g_q_rope, g_k_nope, g_k_rope, g_out, w_out, g_ffn, w_router_group, b_router_group, w_router_expert, b_router_expert, w_gate, w_up, w_down):
    b_p, s_p, d = x_prompt.shape
    b_s, t_s, _ = x_sample.shape
    depth = g_mix.shape[0]
    c = conv_w.shape[2]
    page = cache_ckv.shape[2]
    past_len = page_table.shape[1] * page
    kv_rank = cache_ckv.shape[3]

    cos_p, sin_p = _rope_tables(jnp.arange(s_p, dtype=jnp.int32))
    cos_s, sin_s = _rope_tables(jnp.tile(past_len + jnp.arange(t_s, dtype=jnp.int32), b_s))
    tpos = jnp.tile(jnp.arange(t_s, dtype=jnp.int32), b_s).reshape(-1, 1)

    krope_pages = jnp.swapaxes(cache_krope, 2, 3)

    xp, xs = x_prompt, x_sample.reshape(b_s * t_s, d)
    outs = [[] for _ in range(6)]
    for l in range(depth):
        w = _layer_weights(l, g_mix, w_in, conv_w, conv_b, g_q_lat, w_uq, g_kv_lat, w_uk, w_uv, g_q_nope,
                           g_q_rope, g_k_nope, g_k_rope, g_out, w_out, g_ffn, w_router_group, b_router_group,
                           w_router_expert, b_router_expert, w_gate, w_up, w_down)
        yconv_p, qt, kt, vt, ckv_p, kr_p, conv_p = _proj_prompt(xp, w, cos_p, sin_p)
        yattn_p = _attn_prompt(qt, kt, vt, w["g_out_a"])
        st = state_conv[l]
        zeros = lambda n: jnp.zeros((b_s, n, c), F32)
        st1 = jnp.concatenate([st[:, CONV_K - 2:], zeros(t_s - 1)], axis=1).reshape(b_s * t_s, c)
        st2 = jnp.concatenate([st, zeros(t_s - (CONV_K - 1))], axis=1).reshape(b_s * t_s, c)
        yconv_s, u_s, qa, qr, ckv_s, kr_s = _proj_sample(xs, tpos, st1, st2, w, cos_s, sin_s)
        yattn_s = _attn_decode(l, page_table, cache_ckv, krope_pages, qa, qr, ckv_s, kr_s, w, t_s)
        xmid, h2_tiles, route_t, counts = _merge_route(
            yconv_p.reshape(b_p * s_p, c), yattn_p.reshape(b_p * s_p, -1), xp.reshape(b_p * s_p, d),
            yconv_s, yattn_s, xs, w)
        plan, tile_expert, n_slots = _moe_plan(route_t, counts)
        xs_tiles = _dispatch(plan, h2_tiles, n_slots)
        ys_tiles = _experts(l, tile_expert, plan[3], xs_tiles, w_gate, w_up, w_down)
        yp, ysmp = _combine(plan, ys_tiles, xmid, route_t, b_p * s_p)
        xp, xs = yp.reshape(b_p, s_p, d), ysmp
        for lst, val in zip(outs, (ckv_p, kr_p, conv_p, ckv_s.reshape(b_s, t_s, kv_rank),
                                   kr_s.reshape(b_s, t_s, ROPE_DIM),
                                   u_s.reshape(b_s, t_s, c)[:, t_s - (CONV_K - 1):])):
            lst.append(val)
    return (xp, xs.reshape(b_s, t_s, d)) + tuple(jnp.stack(o) for o in outs)
```

```python
import functools

import jax
import jax.numpy as jnp
from jax import lax
from jax.experimental import pallas as pl
from jax.experimental.pallas import tpu as pltpu

N_HEADS = 8
NOPE_DIM = 64
ROPE_DIM = 32
V_DIM = 64
HEAD_QK = NOPE_DIM + ROPE_DIM
HALF_ROPE = ROPE_DIM // 2
ROPE_BASE = 10000.0
CONV_K = 3
OUT_GROUP_DIM = 64
N_GROUPS = 4
EXPERTS_PER_GROUP = 8
N_EXPERTS = N_GROUPS * EXPERTS_PER_GROUP
TOP_K = 2
EPS = 1e-6
MASK_VALUE = -1e30
LOG2_E = 1.4426950408889634

LANES = 128
SUBLANES = 8
VMEM_LIMIT_BYTES = 48 * 1024 * 1024

PROJ_TILE = 1024
ATTN_Q_TILE = 512
ATTN_K_TILE = 512
TOKEN_TILE = 256
EXPERT_TILE = 512
DECODE_CHUNK_PAGES = 16
RING = 4
COMBINE_AHEAD = 2
SLOT_BATCH = 8

F32 = jnp.float32
BF16 = jnp.bfloat16

_NT = (((1,), (1,)), ((), ()))
_TN = (((0,), (0,)), ((), ()))


def _params(sem):
    return pltpu.CompilerParams(dimension_semantics=sem, vmem_limit_bytes=VMEM_LIMIT_BYTES)


def _rms_rows(x):
    return lax.rsqrt(jnp.mean(x * x, axis=0, keepdims=True) + EPS)


def _rms_lanes(x):
    return lax.rsqrt(jnp.mean(x * x, axis=-1, keepdims=True) + EPS)


def _group_norm_lanes(y, gain):
    lane = lax.broadcasted_iota(jnp.int32, (1, LANES), 1)
    low = lane < OUT_GROUP_DIM
    outs = []
    for j in range(y.shape[1] // LANES):
        t = y[:, j * LANES:(j + 1) * LANES]
        sq = t * t
        ss_lo = jnp.sum(jnp.where(low, sq, 0.0), axis=-1, keepdims=True)
        ss_hi = jnp.sum(jnp.where(low, 0.0, sq), axis=-1, keepdims=True)
        r = jnp.where(low, lax.rsqrt(ss_lo / OUT_GROUP_DIM + EPS), lax.rsqrt(ss_hi / OUT_GROUP_DIM + EPS))
        outs.append(t * r)
    return jnp.concatenate(outs, axis=1) * gain


def _rope_rows(x, cos, sin):
    x1, x2 = x[:HALF_ROPE], x[HALF_ROPE:]
    return x1 * cos - x2 * sin, x1 * sin + x2 * cos


def _attention_side(h_bf16, win_at_ref, gqlat_ref, wuqt_ref, gkv_ref, gqn_ref, gqr_ref, gkr_ref, cos, sin):
    q_rank = gqlat_ref.shape[0]
    kv_rank = gkv_ref.shape[0]
    scale = HEAD_QK ** -0.5 * LOG2_E
    zat = lax.dot_general(win_at_ref[...], h_bf16, _NT, preferred_element_type=F32)
    qlt = zat[:q_rank]
    kvt = zat[q_rank:q_rank + kv_rank]
    krt = zat[q_rank + kv_rank:]
    qln = (qlt * _rms_rows(qlt) * gqlat_ref[...]).astype(BF16)
    qt = jnp.dot(wuqt_ref[...], qln, preferred_element_type=F32)
    q_nope, q_rope = [], []
    for h in range(N_HEADS):
        nope = qt[h * HEAD_QK:h * HEAD_QK + NOPE_DIM]
        rope = qt[h * HEAD_QK + NOPE_DIM:(h + 1) * HEAD_QK]
        q_nope.append(nope * _rms_rows(nope) * gqn_ref[...] * scale)
        r1, r2 = _rope_rows(rope * _rms_rows(rope) * gqr_ref[...], cos, sin)
        q_rope.append((r1 * scale, r2 * scale))
    ckvt = kvt * _rms_rows(kvt) * gkv_ref[...]
    k1, k2 = _rope_rows(krt * _rms_rows(krt) * gkr_ref[...], cos, sin)
    return q_nope, q_rope, ckvt, (k1, k2)


def _to_token_major(xt, width):
    rows, toks = xt.shape
    if rows < LANES:
        xt = jnp.concatenate([xt, jnp.zeros((LANES - rows, toks), F32)], axis=0)
    return xt.T[:, :width]


def _proj_prompt_kernel(x_ref, gmix_ref, win_c_ref, win_at_ref, convw_ref, convb_ref, gout_c_ref,
                        gqlat_ref, wuqt_ref, gkv_ref, gqn_ref, gqr_ref, gkn_ref, gkr_ref,
                        wukt_ref, wuvt_ref, cos_ref, sin_ref,
                        yconv_ref, qt_ref, kt_ref, vt_ref, ckv_ref, krope_ref, convst_ref,
                        ext_ref):
    si = pl.program_id(1)
    tm = x_ref.shape[1]
    c = convw_ref.shape[1]

    @pl.when(si == 0)
    def _():
        ext_ref[0:SUBLANES, :] = jnp.zeros((SUBLANES, c), F32)

    xf = x_ref[0]
    h = (xf * _rms_lanes(xf) * gmix_ref[...]).astype(BF16)

    zc = jnp.dot(h, win_c_ref[...], preferred_element_type=F32)
    u = zc[:, 2 * c:] * zc[:, :c]
    ext_ref[SUBLANES:, :] = u
    v = (convb_ref[...]
         + convw_ref[0:1, :] * ext_ref[pl.ds(SUBLANES - 2, tm), :]
         + convw_ref[1:2, :] * ext_ref[pl.ds(SUBLANES - 1, tm), :]
         + convw_ref[2:3, :] * u)
    yconv = zc[:, c:2 * c] * v
    yconv_ref[0] = _group_norm_lanes(yconv, gout_c_ref[...]).astype(BF16)
    ext_ref[0:SUBLANES, :] = ext_ref[pl.ds(tm, SUBLANES), :]
    convst_ref[0] = ext_ref[pl.ds(SUBLANES - (CONV_K - 1), CONV_K - 1), :]

    q_nope, q_rope, ckvt, (k1, k2) = _attention_side(
        h, win_at_ref, gqlat_ref, wuqt_ref, gkv_ref, gqn_ref, gqr_ref, gkr_ref, cos_ref[...], sin_ref[...])
    for hd in range(N_HEADS):
        qt_ref[0, hd, 0:NOPE_DIM, :] = q_nope[hd].astype(BF16)
        qt_ref[0, hd, NOPE_DIM:NOPE_DIM + HALF_ROPE, :] = q_rope[hd][0].astype(BF16)
        qt_ref[0, hd, NOPE_DIM + HALF_ROPE:HEAD_QK, :] = q_rope[hd][1].astype(BF16)
    ckv_ref[0] = ckvt.T
    krt = jnp.concatenate([k1, k2], axis=0)
    krope_ref[0] = _to_token_major(krt, ROPE_DIM)
    ckv_b = ckvt.astype(BF16)
    ktn = jnp.dot(wukt_ref[...], ckv_b, preferred_element_type=F32)
    vt = jnp.dot(wuvt_ref[...], ckv_b, preferred_element_type=F32)
    krt_b = krt.astype(BF16)
    for hd in range(N_HEADS):
        blk = ktn[hd * NOPE_DIM:(hd + 1) * NOPE_DIM]
        kt_ref[0, hd, 0:NOPE_DIM, :] = (blk * _rms_rows(blk) * gkn_ref[...]).astype(BF16)
        kt_ref[0, hd, NOPE_DIM:HEAD_QK, :] = krt_b
        vt_ref[0, hd] = vt[hd * V_DIM:(hd + 1) * V_DIM].astype(BF16)


def _proj_prompt(x, w, cos_t, sin_t):
    b, s, d = x.shape
    tm = min(PROJ_TILE, s)
    assert s % tm == 0
    c = w["conv_w"].shape[1]
    kv_rank = w["g_kv"].shape[0]
    full = lambda a: pl.BlockSpec(a.shape, lambda bi, si: (0,) * a.ndim)
    weights = [w["g_mix"], w["win_c"], w["win_at"], w["conv_w"], w["conv_b"], w["g_out_c"],
               w["g_qlat"], w["wuq_t"], w["g_kv"], w["g_qn"], w["g_qr"], w["g_kn"], w["g_kr"],
               w["wuk_t"], w["wuv_t"]]
    in_specs = ([pl.BlockSpec((1, tm, d), lambda bi, si: (bi, si, 0))] + [full(a) for a in weights]
                + [pl.BlockSpec((HALF_ROPE, tm), lambda bi, si: (0, si))] * 2)
    out_shape = [
        jax.ShapeDtypeStruct((b, s, c), BF16),
        jax.ShapeDtypeStruct((b, N_HEADS, HEAD_QK, s), BF16),
        jax.ShapeDtypeStruct((b, N_HEADS, HEAD_QK, s), BF16),
        jax.ShapeDtypeStruct((b, N_HEADS, V_DIM, s), BF16),
        jax.ShapeDtypeStruct((b, s, kv_rank), F32),
        jax.ShapeDtypeStruct((b, s, ROPE_DIM), F32),
        jax.ShapeDtypeStruct((b, CONV_K - 1, c), F32),
    ]
    out_specs = [
        pl.BlockSpec((1, tm, c), lambda bi, si: (bi, si, 0)),
        pl.BlockSpec((1, N_HEADS, HEAD_QK, tm), lambda bi, si: (bi, 0, 0, si)),
        pl.BlockSpec((1, N_HEADS, HEAD_QK, tm), lambda bi, si: (bi, 0, 0, si)),
        pl.BlockSpec((1, N_HEADS, V_DIM, tm), lambda bi, si: (bi, 0, 0, si)),
        pl.BlockSpec((1, tm, kv_rank), lambda bi, si: (bi, si, 0)),
        pl.BlockSpec((1, tm, ROPE_DIM), lambda bi, si: (bi, si, 0)),
        pl.BlockSpec((1, CONV_K - 1, c), lambda bi, si: (bi, 0, 0)),
    ]
    return pl.pallas_call(
        _proj_prompt_kernel,
        grid=(b, s // tm),
        in_specs=in_specs,
        out_specs=out_specs,
        out_shape=out_shape,
        scratch_shapes=[pltpu.VMEM((tm + SUBLANES, c), F32)],
        compiler_params=_params(("arbitrary", "arbitrary")),
        name="proj_prompt",
    )(x, *weights, cos_t, sin_t)


ATTN_FULL, ATTN_MASKED, ATTN_LAST = 0, 1, 2


def _attn_prompt_kernel(qi_ref, ki_ref, kind_ref, qt_ref, kt_ref, vt_ref, gout_ref, y_ref, m_ref, l_ref, acc_ref):
    p = pl.program_id(1)
    qi = qi_ref[p]
    ki = ki_ref[p]
    kind = kind_ref[p]
    tq = qt_ref.shape[3]
    tk = kt_ref.shape[3]

    @pl.when(ki == 0)
    def _():
        m_ref[...] = jnp.full(m_ref.shape, -jnp.inf, F32)
        l_ref[...] = jnp.zeros(l_ref.shape, F32)
        acc_ref[...] = jnp.zeros(acc_ref.shape, F32)

    def block(masked):
        if masked:
            offset = 0 if tq == tk else qi * tq - ki * tk
            visible = (lax.broadcasted_iota(jnp.int32, (tk, tq), 0) - lax.broadcasted_iota(jnp.int32, (tk, tq), 1)
                       <= offset)
        def logits(hd):
            return lax.dot_general(kt_ref[0, hd], qt_ref[0, hd], _TN, preferred_element_type=F32)

        s_next = logits(0)
        for hd in range(N_HEADS):
            s = s_next
            if hd + 1 < N_HEADS:
                s_next = logits(hd + 1)
            if masked:
                s = jnp.where(visible, s, MASK_VALUE)
            m_prev = m_ref[hd]
            m_new = jnp.maximum(m_prev, jnp.max(s, axis=0, keepdims=True))
            alpha = jnp.exp2(m_prev - m_new)
            pr = jnp.exp2(s - m_new)
            l_ref[hd] = alpha * l_ref[hd] + jnp.sum(pr, axis=0, keepdims=True)
            pv = jnp.dot(vt_ref[0, hd], pr.astype(BF16), preferred_element_type=F32)
            acc_ref[hd] = alpha * acc_ref[hd] + pv
            m_ref[hd] = m_new

    @pl.when(kind == ATTN_FULL)
    def _():
        block(False)

    @pl.when(kind != ATTN_FULL)
    def _():
        block(True)

    @pl.when(kind == ATTN_LAST)
    def _():
        outs = []
        for hd in range(N_HEADS):
            o = acc_ref[hd] / l_ref[hd]
            outs.append(o * _rms_rows(o) * gout_ref[hd])
        y_ref[0] = jnp.concatenate(outs, axis=0).T.astype(BF16)


def _attn_prompt(qt, kt, vt, gout_a):
    b, _, _, s = qt.shape
    tq, tk = min(ATTN_Q_TILE, s), min(ATTN_K_TILE, s)
    assert s % tq == 0 and s % tk == 0
    steps = []
    for i in range(s // tq):
        j_last = ((i + 1) * tq - 1) // tk
        for j in range(j_last + 1):
            crosses_diagonal = (j + 1) * tk - 1 > i * tq
            steps.append((i, j, ATTN_LAST if j == j_last else ATTN_MASKED if crosses_diagonal else ATTN_FULL))
    qi_tab, ki_tab, kind_tab = (jnp.asarray([st[k] for st in steps], jnp.int32) for k in range(3))
    grid_spec = pltpu.PrefetchScalarGridSpec(
        num_scalar_prefetch=3,
        grid=(b, len(steps)),
        in_specs=[
            pl.BlockSpec((1, N_HEADS, HEAD_QK, tq), lambda bi, p, qi, ki, kind: (bi, 0, 0, qi[p])),
            pl.BlockSpec((1, N_HEADS, HEAD_QK, tk), lambda bi, p, qi, ki, kind: (bi, 0, 0, ki[p])),
            pl.BlockSpec((1, N_HEADS, V_DIM, tk), lambda bi, p, qi, ki, kind: (bi, 0, 0, ki[p])),
            pl.BlockSpec(gout_a.shape, lambda bi, p, qi, ki, kind: (0, 0, 0)),
        ],
        out_specs=pl.BlockSpec((1, tq, N_HEADS * V_DIM), lambda bi, p, qi, ki, kind: (bi, qi[p], 0)),
        scratch_shapes=[pltpu.VMEM((N_HEADS, 1, tq), F32), pltpu.VMEM((N_HEADS, 1, tq), F32),
                        pltpu.VMEM((N_HEADS, V_DIM, tq), F32)],
    )
    return pl.pallas_call(
        _attn_prompt_kernel,
        grid_spec=grid_spec,
        out_shape=jax.ShapeDtypeStruct((b, s, N_HEADS * V_DIM), BF16),
        compiler_params=_params(("arbitrary", "arbitrary")),
        name="attn_prompt",
    )(qi_tab, ki_tab, kind_tab, qt, kt, vt, gout_a)


def _proj_sample_kernel(x_ref, tpos_ref, st1_ref, st2_ref, gmix_ref, win_c_ref, win_at_ref, convw_ref,
                        convb_ref, gout_c_ref, gqlat_ref, wuqt_ref, gkv_ref, gqn_ref, gqr_ref, gkn_ref,
                        gkr_ref, wuk_ref, cos_ref, sin_ref,
                        yconv_ref, u_ref, qa_ref, qr_ref, ckv_ref, krope_ref, ext_ref):
    tm = x_ref.shape[0]
    c = convw_ref.shape[1]
    xf = x_ref[...]
    h = (xf * _rms_lanes(xf) * gmix_ref[...]).astype(BF16)

    zc = jnp.dot(h, win_c_ref[...], preferred_element_type=F32)
    u = zc[:, 2 * c:] * zc[:, :c]
    ext_ref[0:SUBLANES, :] = jnp.zeros((SUBLANES, c), F32)
    ext_ref[SUBLANES:, :] = u
    tpos = tpos_ref[...]
    u_m2 = jnp.where(tpos >= 2, ext_ref[pl.ds(SUBLANES - 2, tm), :], st2_ref[...])
    u_m1 = jnp.where(tpos >= 1, ext_ref[pl.ds(SUBLANES - 1, tm), :], st1_ref[...])
    v = convb_ref[...] + convw_ref[0:1, :] * u_m2 + convw_ref[1:2, :] * u_m1 + convw_ref[2:3, :] * u
    yconv = zc[:, c:2 * c] * v
    yconv_ref[...] = _group_norm_lanes(yconv, gout_c_ref[...]).astype(BF16)
    u_ref[...] = u

    q_nope, q_rope, ckvt, (k1, k2) = _attention_side(
        h, win_at_ref, gqlat_ref, wuqt_ref, gkv_ref, gqn_ref, gqr_ref, gkr_ref, cos_ref[...], sin_ref[...])
    for hd in range(N_HEADS):
        qg = (q_nope[hd] * gkn_ref[...]).astype(BF16)
        qa_t = jnp.dot(wuk_ref[:, hd * NOPE_DIM:(hd + 1) * NOPE_DIM], qg, preferred_element_type=F32)
        qa_ref[hd] = qa_t.T
        qr_ref[hd] = _to_token_major(jnp.concatenate(q_rope[hd], axis=0), ROPE_DIM)
    ckv_ref[...] = ckvt.T
    krope_ref[...] = _to_token_major(jnp.concatenate([k1, k2], axis=0), ROPE_DIM)


def _proj_sample(x, tpos, st1, st2, w, cos_t, sin_t):
    tm, d = x.shape
    c = w["conv_w"].shape[1]
    kv_rank = w["g_kv"].shape[0]
    args = [x, tpos, st1, st2, w["g_mix"], w["win_c"], w["win_at"], w["conv_w"], w["conv_b"], w["g_out_c"],
            w["g_qlat"], w["wuq_t"], w["g_kv"], w["g_qn"], w["g_qr"], w["g_kn"], w["g_kr"], w["wuk"],
            cos_t, sin_t]
    out_shape = [
        jax.ShapeDtypeStruct((tm, c), BF16),
        jax.ShapeDtypeStruct((tm, c), F32),
        jax.ShapeDtypeStruct((N_HEADS, tm, kv_rank), F32),
        jax.ShapeDtypeStruct((N_HEADS, tm, ROPE_DIM), F32),
        jax.ShapeDtypeStruct((tm, kv_rank), F32),
        jax.ShapeDtypeStruct((tm, ROPE_DIM), F32),
    ]
    return pl.pallas_call(
        _proj_sample_kernel,
        out_shape=out_shape,
        scratch_shapes=[pltpu.VMEM((tm + SUBLANES, c), F32)],
        compiler_params=pltpu.CompilerParams(vmem_limit_bytes=VMEM_LIMIT_BYTES),
        name="proj_sample",
    )(*args)


def _attn_decode_kernel(pt_ref, ckv_hbm, kr_hbm, qa_ref, qr_ref, cnew_ref, krnew_ref, wukt_ref, wuv_ref,
                        gout_ref, y_ref, cbuf, kbuf, sem, *, layer, n_pages, page, t_dec):
    b = pl.program_id(0)
    n_seq = pl.num_programs(0)
    chunk_pages = cbuf.shape[1] // page
    n_chunks = n_pages // chunk_pages
    rows = N_HEADS * t_dec

    def copies(first_page, slot):
        out = []
        for pg in range(chunk_pages):
            pid = 0 if first_page is None else pt_ref[first_page + pg]
            out.append(pltpu.make_async_copy(ckv_hbm.at[layer, pid], cbuf.at[slot, pl.ds(pg * page, page)], sem.at[0, slot]))
            out.append(pltpu.make_async_copy(kr_hbm.at[layer, pid], kbuf.at[slot, :, pl.ds(pg * page, page)],
                                             sem.at[1, slot]))
        return out

    qa = qa_ref[...].reshape(rows, qa_ref.shape[2]).astype(BF16)
    qr = qr_ref[...].reshape(rows, ROPE_DIM).astype(BF16)

    n_up = wukt_ref.shape[0]
    wq = jnp.concatenate([wukt_ref[...], qa], axis=0)

    def nope_scores(c_b):
        both = lax.dot_general(wq, c_b, _NT, preferred_element_type=F32)
        rs = []
        for hd in range(N_HEADS):
            blk = both[hd * NOPE_DIM:(hd + 1) * NOPE_DIM]
            rs.append(jnp.broadcast_to(_rms_rows(blk), (t_dec, blk.shape[1])))
        return both[n_up:] * jnp.concatenate(rs, axis=0)

    def update(carry, s, c_b):
        m_prev, l_prev, acc = carry
        m_new = jnp.maximum(m_prev, jnp.max(s, axis=-1, keepdims=True))
        alpha = jnp.exp2(m_prev - m_new)
        pr = jnp.exp2(s - m_new)
        l_new = alpha * l_prev + jnp.sum(pr, axis=-1, keepdims=True)
        acc = alpha * acc + jnp.dot(pr.astype(BF16), c_b, preferred_element_type=F32)
        return m_new, l_new, acc

    last = n_seq * n_chunks - 1
    g0 = b * n_chunks

    def fetch(g):
        for cp in copies(jnp.minimum(g, last) * chunk_pages, lax.rem(g, RING)):
            cp.start()

    def arrive(g):
        for cp in copies(None, lax.rem(g, RING)):
            cp.wait()

    def chunk_scores(g):
        slot = lax.rem(g, RING)
        s_rope = jnp.dot(qr, kbuf[slot].astype(BF16), preferred_element_type=F32)
        return nope_scores(cbuf[slot].astype(BF16)) + s_rope

    def chunk_update(carry, s, g):
        return update(carry, s, cbuf[lax.rem(g, RING)].astype(BF16))

    ahead = RING - 1

    @pl.when(b == 0)
    def _():
        for k in range(ahead):
            fetch(k)

    arrive(g0)
    s_first = chunk_scores(g0)

    def body(j, state):
        carry, s = state
        g = g0 + j
        fetch(g + ahead)
        arrive(g + 1)
        s_next = chunk_scores(g + 1)
        return chunk_update(carry, s, g), s_next

    init = (jnp.full((rows, 1), -jnp.inf, F32), jnp.zeros((rows, 1), F32), jnp.zeros((rows, cbuf.shape[2]), F32))
    carry, s_last = lax.fori_loop(0, n_chunks - 1, body, (init, s_first))
    fetch(g0 + n_chunks - 1 + ahead)
    carry = chunk_update(carry, s_last, g0 + n_chunks - 1)

    @pl.when(b == n_seq - 1)
    def _():
        for k in range(1, ahead + 1):
            arrive(last + k)

    pad = LANES - t_dec
    c_new = jnp.concatenate([cnew_ref[...], jnp.zeros((pad, cnew_ref.shape[1]), F32)], axis=0).astype(BF16)
    kr_new = jnp.concatenate([krnew_ref[...], jnp.zeros((pad, ROPE_DIM), F32)], axis=0).astype(BF16)
    s_new = nope_scores(c_new) + lax.dot_general(qr, kr_new, _NT, preferred_element_type=F32)
    q_t = lax.rem(lax.broadcasted_iota(jnp.int32, (rows, LANES), 0), t_dec)
    key = lax.broadcasted_iota(jnp.int32, (rows, LANES), 1)
    s_new = jnp.where(key <= q_t, s_new, MASK_VALUE)
    _, l_fin, acc = update(carry, s_new, c_new)

    o_lat = (acc / l_fin).astype(BF16)
    ov = jnp.dot(o_lat, wuv_ref[...], preferred_element_type=F32)
    outs = []
    for hd in range(N_HEADS):
        o = ov[hd * t_dec:(hd + 1) * t_dec, hd * V_DIM:(hd + 1) * V_DIM]
        outs.append(o * _rms_lanes(o) * gout_ref[hd])
    y_ref[...] = jnp.concatenate(outs, axis=1)


def _attn_decode(layer, page_table, cache_ckv, cache_kr, qa, qr, c_new, kr_new, w, t_dec):
    n_seq, n_pages = page_table.shape
    _, _, page, kv_rank = cache_ckv.shape
    chunk_pages = min(DECODE_CHUNK_PAGES, n_pages)
    assert n_pages % chunk_pages == 0 and t_dec % SUBLANES == 0 and t_dec <= LANES
    chunk = chunk_pages * page
    grid_spec = pltpu.PrefetchScalarGridSpec(
        num_scalar_prefetch=1,
        grid=(n_seq,),
        in_specs=[
            pl.BlockSpec(memory_space=pl.ANY),
            pl.BlockSpec(memory_space=pl.ANY),
            pl.BlockSpec((N_HEADS, t_dec, kv_rank), lambda b, pt: (0, b, 0)),
            pl.BlockSpec((N_HEADS, t_dec, ROPE_DIM), lambda b, pt: (0, b, 0)),
            pl.BlockSpec((t_dec, kv_rank), lambda b, pt: (b, 0)),
            pl.BlockSpec((t_dec, ROPE_DIM), lambda b, pt: (b, 0)),
            pl.BlockSpec(w["wuk_t"].shape, lambda b, pt: (0, 0)),
            pl.BlockSpec(w["wuv"].shape, lambda b, pt: (0, 0)),
            pl.BlockSpec(w["g_out_a_row"].shape, lambda b, pt: (0, 0, 0)),
        ],
        out_specs=pl.BlockSpec((t_dec, N_HEADS * V_DIM), lambda b, pt: (b, 0)),
        scratch_shapes=[pltpu.VMEM((RING, chunk, kv_rank), F32), pltpu.VMEM((RING, ROPE_DIM, chunk), F32),
                        pltpu.SemaphoreType.DMA((2, RING))],
    )
    return pl.pallas_call(
        functools.partial(_attn_decode_kernel, layer=layer, n_pages=n_pages, page=page, t_dec=t_dec),
        grid_spec=grid_spec,
        out_shape=jax.ShapeDtypeStruct((n_seq * t_dec, N_HEADS * V_DIM), F32),
        compiler_params=_params(("arbitrary",)),
        name="attn_decode",
    )(page_table.reshape(-1), cache_ckv, cache_kr, qa, qr, c_new, kr_new, w["wuk_t"], w["wuv"], w["g_out_a_row"])


ROUTE_E0, ROUTE_E1, ROUTE_R0, ROUTE_R1, ROUTE_W0, ROUTE_W1 = range(6)


def _lane_pick(x, lane, idx):
    return jnp.sum(jnp.where(lane == idx, x, 0.0), axis=-1, keepdims=True)


def _store_row_tiles(ref, x):
    rows, width = x.shape
    n = width // LANES
    for j in range(n):
        ref[pl.ds(j, rows, stride=n), :] = x[:, j * LANES:(j + 1) * LANES]


def _load_row_tiles(ref, rows, n):
    return jnp.concatenate([ref[pl.ds(j, rows, stride=n), :] for j in range(n)], axis=1)


def _merge_route_kernel(ycp_ref, yap_ref, xp_ref, ycs_ref, yas_ref, xs_ref, wout_ref, gffn_ref, wr_ref, br_ref,
                        xmid_ref, h2_ref, route_ref, counts_ref, carry_ref, *, n_prompt_tiles):
    i = pl.program_id(0)
    tm = xp_ref.shape[0]
    is_p = i < n_prompt_tiles

    @pl.when(i == 0)
    def _():
        carry_ref[...] = jnp.zeros(carry_ref.shape, F32)

    yc = jnp.where(is_p, ycp_ref[...], ycs_ref[...])
    ya = jnp.where(is_p, yap_ref[...], yas_ref[...].astype(BF16))
    x = jnp.where(is_p, xp_ref[...], xs_ref[...])
    y = jnp.concatenate([yc, ya], axis=1)
    xm = x + jnp.dot(y, wout_ref[...], preferred_element_type=F32)
    xmid_ref[...] = xm
    h2f = xm * _rms_lanes(xm) * gffn_ref[...]
    h2 = h2f.astype(BF16)
    h2_ref[...] = h2f.reshape(h2_ref.shape).astype(BF16)
    logits = jnp.dot(h2, wr_ref[...], preferred_element_type=F32) + br_ref[...]

    lt = logits.T
    epg = EXPERTS_PER_GROUP
    row = lax.broadcasted_iota(jnp.int32, (epg, tm), 0).astype(F32)
    neg = -jnp.inf
    far = float(epg)

    def first_max(x):
        v = jnp.max(x, axis=0, keepdims=True)
        return v, jnp.min(jnp.where(x == v, row, far), axis=0, keepdims=True)

    gl = jnp.where(row < N_GROUPS, lt[N_EXPERTS:N_EXPERTS + epg], neg)
    ge = jnp.exp(gl - jnp.max(gl, axis=0, keepdims=True))
    pg = ge / jnp.sum(ge, axis=0, keepdims=True)
    p_sel, g_sel = first_max(pg)
    v1 = i1 = v2 = i2 = None
    for g in range(N_GROUPS):
        eg = lt[g * epg:(g + 1) * epg]
        a1, j1 = first_max(eg)
        a2, j2 = first_max(jnp.where(row == j1, neg, eg))
        pick = g_sel == g
        v1, i1, v2, i2 = (c if g == 0 else jnp.where(pick, c, p) for c, p in ((a1, v1), (j1, i1), (a2, v2), (j2, i2)))
    e2 = jnp.exp(v2 - v1)
    w0 = 1.0 / (1.0 + e2) * p_sel
    w1 = e2 / (1.0 + e2) * p_sel
    e0 = g_sel * epg + i1
    e1 = g_sel * epg + i2

    erow = lax.broadcasted_iota(jnp.int32, (N_EXPERTS, tm), 0).astype(F32)
    oh0 = erow == e0
    oh1 = erow == e1
    onehot = (oh0 | oh1).astype(BF16)
    earlier = (lax.broadcasted_iota(jnp.int32, (tm, tm), 0) < lax.broadcasted_iota(jnp.int32, (tm, tm), 1)).astype(BF16)
    before = jnp.dot(onehot, earlier, preferred_element_type=F32) + carry_ref[...]
    r0 = jnp.sum(jnp.where(oh0, before, 0.0), axis=0, keepdims=True)
    r1 = jnp.sum(jnp.where(oh1, before, 0.0), axis=0, keepdims=True)
    carry_ref[...] += jnp.sum(onehot.astype(F32), axis=1, keepdims=True)

    zero = jnp.zeros((1, tm), F32)
    route_ref[...] = jnp.concatenate([e0, e1, r0, r1, w0, w1, zero, zero], axis=0)

    @pl.when(i == pl.num_programs(0) - 1)
    def _():
        counts_ref[...] = carry_ref[...]


def _merge_route(ycp, yap, xp, ycs, yas, xs, w):
    tp, d = xp.shape
    ts = xs.shape[0]
    tm = TOKEN_TILE
    assert tp % tm == 0 and ts % tm == 0
    npt, nst = tp // tm, ts // tm
    half = ycp.shape[1]
    pmap = lambda i: (jnp.minimum(i, npt - 1), 0)
    smap = lambda i: (jnp.maximum(i - npt, 0), 0)
    cmap = lambda i: (0, 0)
    return pl.pallas_call(
        functools.partial(_merge_route_kernel, n_prompt_tiles=npt),
        grid=(npt + nst,),
        in_specs=[
            pl.BlockSpec((tm, half), pmap), pl.BlockSpec((tm, half), pmap), pl.BlockSpec((tm, d), pmap),
            pl.BlockSpec((tm, half), smap), pl.BlockSpec((tm, half), smap), pl.BlockSpec((tm, d), smap),
            pl.BlockSpec(w["w_out"].shape, cmap), pl.BlockSpec(w["g_ffn"].shape, cmap),
            pl.BlockSpec(w["w_r"].shape, cmap), pl.BlockSpec(w["b_r"].shape, cmap),
        ],
        out_specs=[pl.BlockSpec((tm, d), lambda i: (i, 0)),
                   pl.BlockSpec((tm * (d // LANES), LANES), lambda i: (i, 0)),
                   pl.BlockSpec((SUBLANES, tm), lambda i: (0, i)),
                   pl.BlockSpec((N_EXPERTS, 1), cmap)],
        out_shape=[jax.ShapeDtypeStruct((tp + ts, d), F32),
                   jax.ShapeDtypeStruct(((tp + ts) * (d // LANES), LANES), BF16),
                   jax.ShapeDtypeStruct((SUBLANES, tp + ts), F32),
                   jax.ShapeDtypeStruct((N_EXPERTS, 1), F32)],
        scratch_shapes=[pltpu.VMEM((N_EXPERTS, 1), F32)],
        compiler_params=_params(("arbitrary",)),
        name="merge_route",
    )(ycp, yap, xp, ycs, yas, xs, w["w_out"], w["g_ffn"], w["w_r"], w["b_r"])


def _tile_of(ref, row, n):
    start = row * n
    return ref.at[pl.ds(start if isinstance(row, int) else pl.multiple_of(start, n), n)]


def _pad_fill_copies(base_ref, cnt_ref, tiles_ref, nu_ref, zeros_ref, xs_hbm, sem, tme, n, n_tiles):
    out = []
    for e in range(N_EXPERTS):
        pad = tiles_ref[e] * tme - cnt_ref[e]
        pos = base_ref[e] + cnt_ref[e]
        bit = tme // 2
        while bit >= 1:
            take = pad & bit
            out.append((take != 0, pltpu.make_async_copy(
                zeros_ref.at[pl.ds(0, bit * n)], xs_hbm.at[pl.ds(pl.multiple_of(pos * n, n), bit * n)], sem)))
            pos = pos + take
            bit //= 2
    for k in range(N_EXPERTS + 1):
        tile = nu_ref[0] + k
        out.append((tile < n_tiles, pltpu.make_async_copy(
            zeros_ref, xs_hbm.at[pl.ds(pl.multiple_of(jnp.minimum(tile, n_tiles - 1) * (tme * n), tme * n), tme * n)],
            sem)))
    return out


def _dispatch_kernel(base_ref, cnt_ref, tiles_ref, nu_ref, s0_ref, s1_ref, h2_ref, xs_hbm, stage, zeros, sem,
                     fill_sem, *, tm, tme, n, n_tiles):
    i = pl.program_id(0)
    last = pl.num_programs(0) - 1
    slot = lax.rem(i, 2)

    def drain(sl):
        for t in range(2 * tm):
            pltpu.make_async_copy(_tile_of(stage.at[sl], 0, n), _tile_of(xs_hbm, 0, n), sem.at[sl]).wait()

    @pl.when(i >= 2)
    def _():
        drain(slot)

    stage[slot] = h2_ref[...]
    for t0 in range(0, tm, SLOT_BATCH):
        slots = [(s0_ref[0, 0, t], s1_ref[0, 0, t]) for t in range(t0, t0 + SLOT_BATCH)]
        for t, pair in zip(range(t0, t0 + SLOT_BATCH), slots):
            for k, s in enumerate(pair):
                pltpu.make_async_copy(_tile_of(stage.at[slot], t, n), _tile_of(xs_hbm, s, n),
                                      sem.at[slot]).start(priority=k)

    def fills():
        return _pad_fill_copies(base_ref, cnt_ref, tiles_ref, nu_ref, zeros, xs_hbm, fill_sem, tme, n, n_tiles)

    @pl.when(i == 0)
    def _():
        zeros[...] = jnp.zeros(zeros.shape, zeros.dtype)
        for pred, cp in fills():
            pl.when(pred)(cp.start)
        for pred, cp in fills():
            pl.when(pred)(cp.wait)

    @pl.when((i == last) & (i >= 1))
    def _():
        drain(1 - slot)

    @pl.when(i == last)
    def _():
        drain(slot)


def _dispatch(plan, h2_tiles, n_slots):
    base, cnt, tiles, n_used, slot0, slot1 = plan
    n_tok_tiles, _, tm = slot0.shape
    n = h2_tiles.shape[0] // (n_tok_tiles * tm)
    tme = EXPERT_TILE
    n_tiles = n_slots // tme
    smem_blk = pl.BlockSpec((1, 1, tm), lambda i, *_: (i, 0, 0), memory_space=pltpu.SMEM)
    grid_spec = pltpu.PrefetchScalarGridSpec(
        num_scalar_prefetch=4,
        grid=(n_tok_tiles,),
        in_specs=[smem_blk] * 2 + [pl.BlockSpec((tm * n, LANES), lambda i, *_: (i, 0))],
        out_specs=pl.BlockSpec(memory_space=pl.ANY),
        scratch_shapes=[pltpu.VMEM((2, tm * n, LANES), h2_tiles.dtype), pltpu.VMEM((tme * n, LANES), h2_tiles.dtype),
                        pltpu.SemaphoreType.DMA((2,)), pltpu.SemaphoreType.DMA],
    )
    return pl.pallas_call(
        functools.partial(_dispatch_kernel, tm=tm, tme=tme, n=n, n_tiles=n_tiles),
        grid_spec=grid_spec,
        out_shape=jax.ShapeDtypeStruct((n_slots * n, LANES), h2_tiles.dtype),
        compiler_params=_params(("arbitrary",)),
        name="dispatch",
    )(base, cnt, tiles, n_used, slot0, slot1, h2_tiles)


def _experts_kernel(te_ref, nx_ref, nu_ref, xs_ref, wg_hbm, wu_hbm, wd_hbm, ys_ref, wg_buf, wu_buf, wd_buf, wgu_bf,
                    wd_bf, par_ref, sem, *, layer):
    i = pl.program_id(0)
    n_used = nu_ref[0]
    _, _, d_exp, d = wd_hbm.shape
    n = d // LANES
    tme = ys_ref.shape[0] // n
    expert = te_ref[i]
    new_expert = (i == 0) | (expert != te_ref[jnp.maximum(i - 1, 0)])

    def weight_copies(e, slot):
        return [pltpu.make_async_copy(src.at[layer, e], dst.at[slot], sem.at[slot])
                for src, dst in ((wg_hbm, wg_buf), (wu_hbm, wu_buf), (wd_hbm, wd_buf))]

    @pl.when(i == 0)
    def _():
        par_ref[0] = 0
        for cp in weight_copies(expert, 0):
            cp.start()

    @pl.when(new_expert & (i < n_used))
    def _():
        cur = par_ref[0]
        for cp in weight_copies(0, cur):
            cp.wait()
        nxt = nx_ref[i]

        @pl.when(nxt >= 0)
        def _():
            for cp in weight_copies(nxt, 1 - cur):
                cp.start()

        wgu_bf[:, :d_exp] = wg_buf[cur].astype(BF16)
        wgu_bf[:, d_exp:] = wu_buf[cur].astype(BF16)
        wd_bf[...] = wd_buf[cur].astype(BF16)
        par_ref[0] = 1 - cur

    @pl.when(i < n_used)
    def _():
        h2 = xs_ref[...].astype(F32).reshape(tme, d).astype(BF16)
        gu = jnp.dot(h2, wgu_bf[...], preferred_element_type=F32)
        g = gu[:, :d_exp]
        a = (g / (1.0 + jnp.exp(-g))) * gu[:, d_exp:]
        _store_row_tiles(ys_ref, jnp.dot(a.astype(BF16), wd_bf[...], preferred_element_type=F32))

    @pl.when(i >= n_used)
    def _():
        ys_ref[...] = jnp.zeros(ys_ref.shape, F32)


def _experts(layer, tile_expert, n_used, xs_tiles, w_gate, w_up, w_down):
    n_tiles = tile_expert.shape[0]
    _, _, d, d_exp = w_gate.shape
    blk = xs_tiles.shape[0] // n_tiles
    idx = jnp.arange(n_tiles, dtype=jnp.int32)
    later_other = (tile_expert[None, :] != tile_expert[:, None]) & (idx[None, :] > idx[:, None])
    first_other = jnp.min(jnp.where(later_other, idx[None, :], n_tiles), axis=1)
    next_expert = jnp.where(first_other < n_tiles,
                            jnp.sum(jnp.where(idx[None, :] == first_other[:, None], tile_expert[None, :], 0), axis=1), -1)
    grid_spec = pltpu.PrefetchScalarGridSpec(
        num_scalar_prefetch=3,
        grid=(n_tiles,),
        in_specs=[pl.BlockSpec((blk, LANES), lambda i, te, nx, nu: (jnp.minimum(i, nu[0] - 1), 0))]
        + [pl.BlockSpec(memory_space=pl.ANY)] * 3,
        out_specs=pl.BlockSpec((blk, LANES), lambda i, te, nx, nu: (i, 0)),
        scratch_shapes=[pltpu.VMEM((2, d, d_exp), F32), pltpu.VMEM((2, d, d_exp), F32), pltpu.VMEM((2, d_exp, d), F32),
                        pltpu.VMEM((d, 2 * d_exp), BF16), pltpu.VMEM((d_exp, d), BF16),
                        pltpu.SMEM((1,), jnp.int32), pltpu.SemaphoreType.DMA((2,))],
    )
    return pl.pallas_call(
        functools.partial(_experts_kernel, layer=layer),
        grid_spec=grid_spec,
        out_shape=jax.ShapeDtypeStruct((n_tiles * blk, LANES), F32),
        compiler_params=_params(("arbitrary",)),
        name="experts",
    )(tile_expert, next_expert.astype(jnp.int32), n_used, xs_tiles, w_gate, w_up, w_down)


def _combine_kernel(*refs, n_prompt_tiles):
    tables = [refs[2 * a:2 * a + 2] for a in range(COMBINE_AHEAD + 1)]
    ys_hbm, xmid_ref, route_ref, yp_ref, ysmp_ref, buf, sem = refs[2 * (COMBINE_AHEAD + 1):]
    ring = COMBINE_AHEAD + 1
    i = pl.program_id(0)
    last = pl.num_programs(0) - 1
    tm, d = xmid_ref.shape
    n = d // LANES
    slot = lax.rem(i, ring)

    def gather(refs, dst_slot):
        s0, s1 = refs
        for t0 in range(0, tm, SLOT_BATCH):
            slots = [(s0[0, 0, t], s1[0, 0, t]) for t in range(t0, t0 + SLOT_BATCH)]
            for t, pair in zip(range(t0, t0 + SLOT_BATCH), slots):
                for k, s in enumerate(pair):
                    pltpu.make_async_copy(_tile_of(ys_hbm, s, n), _tile_of(buf.at[dst_slot, k], t, n),
                                          sem.at[dst_slot]).start(priority=k)

    def gather_wait(dst_slot):
        for t in range(2 * tm):
            pltpu.make_async_copy(_tile_of(ys_hbm, 0, n), _tile_of(buf.at[dst_slot, 0], 0, n), sem.at[dst_slot]).wait()

    @pl.when(i == 0)
    def _():
        for a in range(COMBINE_AHEAD):
            gather(tables[a], a)

    gather(tables[COMBINE_AHEAD], lax.rem(i + COMBINE_AHEAD, ring))
    gather_wait(slot)

    rt = route_ref[...]
    cols = jnp.concatenate([rt, jnp.zeros((LANES - rt.shape[0], tm), F32)], axis=0).T
    lane = lax.broadcasted_iota(jnp.int32, cols.shape, 1)
    w0 = _lane_pick(cols, lane, ROUTE_W0)
    w1 = _lane_pick(cols, lane, ROUTE_W1)
    y0 = _load_row_tiles(buf.at[slot, 0], tm, n)
    y1 = _load_row_tiles(buf.at[slot, 1], tm, n)
    out = xmid_ref[...] + (w0 * y0 + w1 * y1)

    @pl.when(i < n_prompt_tiles)
    def _():
        yp_ref[...] = out

    @pl.when(i >= n_prompt_tiles)
    def _():
        ysmp_ref[...] = out

    @pl.when(i == last)
    def _():
        for a in range(1, COMBINE_AHEAD + 1):
            gather_wait(lax.rem(i + a, ring))


def _combine(plan, ys_tiles, xmid, route_t, tp):
    slot0, slot1 = plan[4:]
    ttot, d = xmid.shape
    n_tok_tiles, _, tm = slot0.shape
    assert n_tok_tiles >= COMBINE_AHEAD
    npt = tp // tm
    n = d // LANES

    def table(a):
        return pl.BlockSpec((1, 1, tm), lambda i: (jnp.minimum(i + a, n_tok_tiles - 1), 0, 0), memory_space=pltpu.SMEM)

    return pl.pallas_call(
        functools.partial(_combine_kernel, n_prompt_tiles=npt),
        grid=(n_tok_tiles,),
        in_specs=[table(a) for a in range(COMBINE_AHEAD + 1) for _ in range(2)] + [
                  pl.BlockSpec(memory_space=pl.ANY),
                  pl.BlockSpec((tm, d), lambda i: (i, 0)),
                  pl.BlockSpec((SUBLANES, tm), lambda i: (0, i))],
        out_specs=[pl.BlockSpec((tm, d), lambda i: (jnp.minimum(i, npt - 1), 0)),
                   pl.BlockSpec((tm, d), lambda i: (jnp.maximum(i - npt, 0), 0))],
        out_shape=[jax.ShapeDtypeStruct((tp, d), F32), jax.ShapeDtypeStruct((ttot - tp, d), F32)],
        scratch_shapes=[pltpu.VMEM((COMBINE_AHEAD + 1, 2, tm * n, LANES), F32),
                        pltpu.SemaphoreType.DMA((COMBINE_AHEAD + 1,))],
        compiler_params=_params(("arbitrary",)),
        name="combine",
    )(*([slot0, slot1] * (COMBINE_AHEAD + 1)), ys_tiles, xmid, route_t)


def _rope_tables(pos):
    inv_freq = ROPE_BASE ** (-jnp.arange(HALF_ROPE, dtype=F32) / HALF_ROPE)
    ang = pos.astype(F32)[:, None] * inv_freq[None, :]
    return jnp.cos(ang).T, jnp.sin(ang).T


def _layer_weights(l, g_mix, w_in, conv_w, conv_b, g_q_lat, w_uq, g_kv_lat, w_uk, w_uv, g_q_nope, g_q_rope,
                   g_k_nope, g_k_rope, g_out, w_out, g_ffn, w_router_group, b_router_group, w_router_expert,
                   b_router_expert, w_gate, w_up, w_down):
    c = conv_w.shape[2]
    col = lambda g: g[l].reshape(-1, 1)
    w_r = jnp.concatenate([w_router_expert[l], w_router_group[l]], axis=1)
    b_r = jnp.concatenate([b_router_expert[l], b_router_group[l]])
    return {
        "g_mix": g_mix[l].reshape(1, -1),
        "win_c": w_in[l][:, :3 * c].astype(BF16),
        "win_at": w_in[l][:, 3 * c:].T.astype(BF16),
        "conv_w": conv_w[l],
        "conv_b": conv_b[l].reshape(1, -1),
        "g_out_c": g_out[l][:c].reshape(1, -1),
        "g_out_a": g_out[l][c:].reshape(N_HEADS, V_DIM, 1),
        "g_out_a_row": g_out[l][c:].reshape(N_HEADS, 1, V_DIM),
        "g_qlat": col(g_q_lat), "g_kv": col(g_kv_lat), "g_qn": col(g_q_nope), "g_qr": col(g_q_rope),
        "g_kn": col(g_k_nope), "g_kr": col(g_k_rope),
        "wuq_t": w_uq[l].T.astype(BF16),
        "wuk_t": w_uk[l].T.astype(BF16),
        "wuk": w_uk[l].astype(BF16),
        "wuv_t": w_uv[l].T.astype(BF16),
        "wuv": w_uv[l].astype(BF16),
        "w_out": w_out[l].astype(BF16),
        "g_ffn": g_ffn[l].reshape(1, -1),
        "w_r": jnp.pad(w_r, ((0, 0), (0, LANES - w_r.shape[1]))).astype(BF16),
        "b_r": jnp.pad(b_r, (0, LANES - b_r.shape[0])).reshape(1, -1),
    }


def _moe_plan(route_t, counts):
    ttot = route_t.shape[1]
    tme = EXPERT_TILE
    tm = TOKEN_TILE
    n_tiles = (TOP_K * ttot) // tme + N_EXPERTS + 1
    cnt = counts[:, 0].astype(jnp.int32)
    tiles = (cnt + tme - 1) // tme
    tile_end = jnp.cumsum(tiles)
    base = (tile_end - tiles) * tme
    n_used = tile_end[-1:]
    tile_id = jnp.minimum(jnp.arange(n_tiles, dtype=jnp.int32), n_used[0] - 1)
    tile_expert = jnp.sum((tile_end[None, :] <= tile_id[:, None]).astype(jnp.int32), axis=1)
    ids = route_t[:ROUTE_R1 + 1].astype(jnp.int32)
    expert_ids = jnp.arange(N_EXPERTS, dtype=jnp.int32)[:, None]

    def slots(e, r):
        s = r + jnp.sum(jnp.where(e[None, :] == expert_ids, base[:, None], 0), axis=0)
        return s.reshape(ttot // tm, 1, tm)

    plan = (base, cnt, tiles, n_used, slots(ids[ROUTE_E0], ids[ROUTE_R0]), slots(ids[ROUTE_E1], ids[ROUTE_R1]))
    return plan, tile_expert, n_tiles * tme


def kernel(x_prompt, x_sample, state_conv, cache_ckv, cache_krope, page_table, g_mix, w_in, conv_w, conv_b, g_q_lat, w_uq, g_kv_lat, w_uk, w_uv, g_q_nope, g_q_rope, g_k_nope, g_k_rope, g_out, w_out, g_ffn, w_router_group, b_router_group, w_router_expert, b_router_expert, w_gate, w_up, w_down):
    b_p, s_p, d = x_prompt.shape
    b_s, t_s, _ = x_sample.shape
    depth = g_mix.shape[0]
    c = conv_w.shape[2]
    page = cache_ckv.shape[2]
    past_len = page_table.shape[1] * page
    kv_rank = cache_ckv.shape[3]

    cos_p, sin_p = _rope_tables(jnp.arange(s_p, dtype=jnp.int32))
    cos_s, sin_s = _rope_tables(jnp.tile(past_len + jnp.arange(t_s, dtype=jnp.int32), b_s))
    tpos = jnp.tile(jnp.arange(t_s, dtype=jnp.int32), b_s).reshape(-1, 1)

    krope_pages = jnp.swapaxes(cache_krope, 2, 3)

    xp, xs = x_prompt, x_sample.reshape(b_s * t_s, d)
    outs = [[] for _ in range(6)]
    for l in range(depth):
        w = _layer_weights(l, g_mix, w_in, conv_w, conv_b, g_q_lat, w_uq, g_kv_lat, w_uk, w_uv, g_q_nope,
                           g_q_rope, g_k_nope, g_k_rope, g_out, w_out, g_ffn, w_router_group, b_router_group,
                           w_router_expert, b_router_expert, w_gate, w_up, w_down)
        yconv_p, qt, kt, vt, ckv_p, kr_p, conv_p = _proj_prompt(xp, w, cos_p, sin_p)
        yattn_p = _attn_prompt(qt, kt, vt, w["g_out_a"])
        st = state_conv[l]
        zeros = lambda n: jnp.zeros((b_s, n, c), F32)
        st1 = jnp.concatenate([st[:, CONV_K - 2:], zeros(t_s - 1)], axis=1).reshape(b_s * t_s, c)
        st2 = jnp.concatenate([st, zeros(t_s - (CONV_K - 1))], axis=1).reshape(b_s * t_s, c)
        yconv_s, u_s, qa, qr, ckv_s, kr_s = _proj_sample(xs, tpos, st1, st2, w, cos_s, sin_s)
        yattn_s = _attn_decode(l, page_table, cache_ckv, krope_pages, qa, qr, ckv_s, kr_s, w, t_s)
        xmid, h2_tiles, route_t, counts = _merge_route(
            yconv_p.reshape(b_p * s_p, c), yattn_p.reshape(b_p * s_p, -1), xp.reshape(b_p * s_p, d),
            yconv_s, yattn_s, xs, w)
        plan, tile_expert, n_slots = _moe_plan(route_t, counts)
        xs_tiles = _dispatch(plan, h2_tiles, n_slots)
        ys_tiles = _experts(l, tile_expert, plan[3], xs_tiles, w_gate, w_up, w_down)
        yp, ysmp = _combine(plan, ys_tiles, xmid, route_t, b_p * s_p)
        xp, xs = yp.reshape(b_p, s_p, d), ysmp
        for lst, val in zip(outs, (ckv_p, kr_p, conv_p, ckv_s.reshape(b_s, t_s, kv_rank),
                                   kr_s.reshape(b_s, t_s, ROPE_DIM),
                                   u_s.reshape(b_s, t_s, c)[:, t_s - (CONV_K - 1):])):
            lst.append(val)
    return (xp, xs.reshape(b_s, t_s, d)) + tuple(jnp.stack(o) for o in outs)
```

```python
import functools

import jax
import jax.numpy as jnp
from jax import lax
from jax.experimental import pallas as pl
from jax.experimental.pallas import tpu as pltpu

N_HEADS = 8
NOPE_DIM = 64
ROPE_DIM = 32
V_DIM = 64
HEAD_QK = NOPE_DIM + ROPE_DIM
HALF_ROPE = ROPE_DIM // 2
ROPE_BASE = 10000.0
CONV_K = 3
OUT_GROUP_DIM = 64
N_GROUPS = 4
EXPERTS_PER_GROUP = 8
N_EXPERTS = N_GROUPS * EXPERTS_PER_GROUP
TOP_K = 2
EPS = 1e-6
MASK_VALUE = -1e30
LOG2_E = 1.4426950408889634

LANES = 128
SUBLANES = 8
VMEM_LIMIT_BYTES = 48 * 1024 * 1024

PROJ_TILE = 1024
ATTN_Q_TILE = 512
ATTN_K_TILE = 512
ATTN_LOOKAHEAD = 2
TOKEN_TILE = 256
EXPERT_TILE = 512
DECODE_CHUNK_PAGES = 16
RING = 4
COMBINE_AHEAD = 2
SLOT_BATCH = 8

F32 = jnp.float32
BF16 = jnp.bfloat16

_NT = (((1,), (1,)), ((), ()))
_TN = (((0,), (0,)), ((), ()))


def _params(sem):
    return pltpu.CompilerParams(dimension_semantics=sem, vmem_limit_bytes=VMEM_LIMIT_BYTES)


def _rms_rows(x):
    return lax.rsqrt(jnp.mean(x * x, axis=0, keepdims=True) + EPS)


def _rms_lanes(x):
    return lax.rsqrt(jnp.mean(x * x, axis=-1, keepdims=True) + EPS)


def _group_norm_lanes(y, gain):
    lane = lax.broadcasted_iota(jnp.int32, (1, LANES), 1)
    low = lane < OUT_GROUP_DIM
    outs = []
    for j in range(y.shape[1] // LANES):
        t = y[:, j * LANES:(j + 1) * LANES]
        sq = t * t
        ss_lo = jnp.sum(jnp.where(low, sq, 0.0), axis=-1, keepdims=True)
        ss_hi = jnp.sum(jnp.where(low, 0.0, sq), axis=-1, keepdims=True)
        r = jnp.where(low, lax.rsqrt(ss_lo / OUT_GROUP_DIM + EPS), lax.rsqrt(ss_hi / OUT_GROUP_DIM + EPS))
        outs.append(t * r)
    return jnp.concatenate(outs, axis=1) * gain


def _rope_rows(x, cos, sin):
    x1, x2 = x[:HALF_ROPE], x[HALF_ROPE:]
    return x1 * cos - x2 * sin, x1 * sin + x2 * cos


def _attention_side(h_bf16, win_at_ref, gqlat_ref, wuqt_ref, gkv_ref, gqn_ref, gqr_ref, gkr_ref, cos, sin):
    q_rank = gqlat_ref.shape[0]
    kv_rank = gkv_ref.shape[0]
    scale = HEAD_QK ** -0.5 * LOG2_E
    zat = lax.dot_general(win_at_ref[...], h_bf16, _NT, preferred_element_type=F32)
    qlt = zat[:q_rank]
    kvt = zat[q_rank:q_rank + kv_rank]
    krt = zat[q_rank + kv_rank:]
    qln = (qlt * _rms_rows(qlt) * gqlat_ref[...]).astype(BF16)
    qt = jnp.dot(wuqt_ref[...], qln, preferred_element_type=F32)
    q_nope, q_rope = [], []
    for h in range(N_HEADS):
        nope = qt[h * HEAD_QK:h * HEAD_QK + NOPE_DIM]
        rope = qt[h * HEAD_QK + NOPE_DIM:(h + 1) * HEAD_QK]
        q_nope.append(nope * _rms_rows(nope) * gqn_ref[...] * scale)
        r1, r2 = _rope_rows(rope * _rms_rows(rope) * gqr_ref[...], cos, sin)
        q_rope.append((r1 * scale, r2 * scale))
    ckvt = kvt * _rms_rows(kvt) * gkv_ref[...]
    k1, k2 = _rope_rows(krt * _rms_rows(krt) * gkr_ref[...], cos, sin)
    return q_nope, q_rope, ckvt, (k1, k2)


def _to_token_major(xt, width):
    rows, toks = xt.shape
    if rows < LANES:
        xt = jnp.concatenate([xt, jnp.zeros((LANES - rows, toks), F32)], axis=0)
    return xt.T[:, :width]


def _proj_prompt_kernel(x_ref, gmix_ref, win_c_ref, win_at_ref, convw_ref, convb_ref, gout_c_ref,
                        gqlat_ref, wuqt_ref, gkv_ref, gqn_ref, gqr_ref, gkn_ref, gkr_ref,
                        wukt_ref, wuvt_ref, cos_ref, sin_ref,
                        yconv_ref, qt_ref, kt_ref, vt_ref, ckv_ref, krope_ref, convst_ref,
                        ext_ref):
    si = pl.program_id(1)
    tm = x_ref.shape[1]
    c = convw_ref.shape[1]

    @pl.when(si == 0)
    def _():
        ext_ref[0:SUBLANES, :] = jnp.zeros((SUBLANES, c), F32)

    xf = x_ref[0]
    h = (xf * _rms_lanes(xf) * gmix_ref[...]).astype(BF16)

    zc = jnp.dot(h, win_c_ref[...], preferred_element_type=F32)
    u = zc[:, 2 * c:] * zc[:, :c]
    ext_ref[SUBLANES:, :] = u
    v = (convb_ref[...]
         + convw_ref[0:1, :] * ext_ref[pl.ds(SUBLANES - 2, tm), :]
         + convw_ref[1:2, :] * ext_ref[pl.ds(SUBLANES - 1, tm), :]
         + convw_ref[2:3, :] * u)
    yconv = zc[:, c:2 * c] * v
    yconv_ref[0] = _group_norm_lanes(yconv, gout_c_ref[...]).astype(BF16)
    ext_ref[0:SUBLANES, :] = ext_ref[pl.ds(tm, SUBLANES), :]
    convst_ref[0] = ext_ref[pl.ds(SUBLANES - (CONV_K - 1), CONV_K - 1), :]

    q_nope, q_rope, ckvt, (k1, k2) = _attention_side(
        h, win_at_ref, gqlat_ref, wuqt_ref, gkv_ref, gqn_ref, gqr_ref, gkr_ref, cos_ref[...], sin_ref[...])
    for hd in range(N_HEADS):
        qt_ref[0, hd, 0:NOPE_DIM, :] = q_nope[hd].astype(BF16)
        qt_ref[0, hd, NOPE_DIM:NOPE_DIM + HALF_ROPE, :] = q_rope[hd][0].astype(BF16)
        qt_ref[0, hd, NOPE_DIM + HALF_ROPE:HEAD_QK, :] = q_rope[hd][1].astype(BF16)
    ckv_ref[0] = ckvt.T
    krt = jnp.concatenate([k1, k2], axis=0)
    krope_ref[0] = krt
    ckv_b = ckvt.astype(BF16)
    ktn = jnp.dot(wukt_ref[...], ckv_b, preferred_element_type=F32)
    vt = jnp.dot(wuvt_ref[...], ckv_b, preferred_element_type=F32)
    krt_b = krt.astype(BF16)
    for hd in range(N_HEADS):
        blk = ktn[hd * NOPE_DIM:(hd + 1) * NOPE_DIM]
        kt_ref[0, hd, 0:NOPE_DIM, :] = (blk * _rms_rows(blk) * gkn_ref[...]).astype(BF16)
        kt_ref[0, hd, NOPE_DIM:HEAD_QK, :] = krt_b
        vt_ref[0, hd] = vt[hd * V_DIM:(hd + 1) * V_DIM].astype(BF16)


def _proj_prompt(x, w, cos_t, sin_t):
    b, s, d = x.shape
    tm = min(PROJ_TILE, s)
    assert s % tm == 0
    c = w["conv_w"].shape[1]
    kv_rank = w["g_kv"].shape[0]
    full = lambda a: pl.BlockSpec(a.shape, lambda bi, si: (0,) * a.ndim)
    weights = [w["g_mix"], w["win_c"], w["win_at"], w["conv_w"], w["conv_b"], w["g_out_c"],
               w["g_qlat"], w["wuq_t"], w["g_kv"], w["g_qn"], w["g_qr"], w["g_kn"], w["g_kr"],
               w["wuk_t"], w["wuv_t"]]
    in_specs = ([pl.BlockSpec((1, tm, d), lambda bi, si: (bi, si, 0))] + [full(a) for a in weights]
                + [pl.BlockSpec((HALF_ROPE, tm), lambda bi, si: (0, si))] * 2)
    out_shape = [
        jax.ShapeDtypeStruct((b, s, c), BF16),
        jax.ShapeDtypeStruct((b, N_HEADS, HEAD_QK, s), BF16),
        jax.ShapeDtypeStruct((b, N_HEADS, HEAD_QK, s), BF16),
        jax.ShapeDtypeStruct((b, N_HEADS, V_DIM, s), BF16),
        jax.ShapeDtypeStruct((b, s, kv_rank), F32),
        jax.ShapeDtypeStruct((b, ROPE_DIM, s), F32),
        jax.ShapeDtypeStruct((b, CONV_K - 1, c), F32),
    ]
    out_specs = [
        pl.BlockSpec((1, tm, c), lambda bi, si: (bi, si, 0)),
        pl.BlockSpec((1, N_HEADS, HEAD_QK, tm), lambda bi, si: (bi, 0, 0, si)),
        pl.BlockSpec((1, N_HEADS, HEAD_QK, tm), lambda bi, si: (bi, 0, 0, si)),
        pl.BlockSpec((1, N_HEADS, V_DIM, tm), lambda bi, si: (bi, 0, 0, si)),
        pl.BlockSpec((1, tm, kv_rank), lambda bi, si: (bi, si, 0)),
        pl.BlockSpec((1, ROPE_DIM, tm), lambda bi, si: (bi, 0, si)),
        pl.BlockSpec((1, CONV_K - 1, c), lambda bi, si: (bi, 0, 0)),
    ]
    return pl.pallas_call(
        _proj_prompt_kernel,
        grid=(b, s // tm),
        in_specs=in_specs,
        out_specs=out_specs,
        out_shape=out_shape,
        scratch_shapes=[pltpu.VMEM((tm + SUBLANES, c), F32)],
        compiler_params=_params(("arbitrary", "arbitrary")),
        name="proj_prompt",
    )(x, *weights, cos_t, sin_t)


ATTN_FULL, ATTN_MASKED, ATTN_LAST = 0, 1, 2


def _attn_prompt_kernel(qi_ref, ki_ref, kind_ref, qt_ref, kt_ref, vt_ref, gout_ref, y_ref, m_ref, l_ref, acc_ref):
    p = pl.program_id(1)
    qi = qi_ref[p]
    ki = ki_ref[p]
    kind = kind_ref[p]
    tq = qt_ref.shape[3]
    tk = kt_ref.shape[3]

    @pl.when(ki == 0)
    def _():
        m_ref[...] = jnp.full(m_ref.shape, -jnp.inf, F32)
        l_ref[...] = jnp.zeros(l_ref.shape, F32)
        acc_ref[...] = jnp.zeros(acc_ref.shape, F32)

    def block(masked):
        if masked:
            offset = 0 if tq == tk else qi * tq - ki * tk
            visible = (lax.broadcasted_iota(jnp.int32, (tk, tq), 0) - lax.broadcasted_iota(jnp.int32, (tk, tq), 1)
                       <= offset)

        def logits(hd):
            return lax.dot_general(kt_ref[0, hd], qt_ref[0, hd], _TN, preferred_element_type=F32)

        ahead = [logits(hd) for hd in range(ATTN_LOOKAHEAD)]
        for hd in range(N_HEADS):
            s = ahead.pop(0)
            if hd + ATTN_LOOKAHEAD < N_HEADS:
                ahead.append(logits(hd + ATTN_LOOKAHEAD))
            if masked:
                s = jnp.where(visible, s, MASK_VALUE)
            m_prev = m_ref[hd]
            m_new = jnp.maximum(m_prev, jnp.max(s, axis=0, keepdims=True))
            alpha = jnp.exp2(m_prev - m_new)
            pr = jnp.exp2(s - m_new)
            l_ref[hd] = alpha * l_ref[hd] + jnp.sum(pr, axis=0, keepdims=True)
            pv = jnp.dot(vt_ref[0, hd], pr.astype(BF16), preferred_element_type=F32)
            acc_ref[hd] = alpha * acc_ref[hd] + pv
            m_ref[hd] = m_new

    @pl.when(kind == ATTN_FULL)
    def _():
        block(False)

    @pl.when(kind != ATTN_FULL)
    def _():
        block(True)

    @pl.when(kind == ATTN_LAST)
    def _():
        outs = []
        for hd in range(N_HEADS):
            o = acc_ref[hd] / l_ref[hd]
            outs.append(o * _rms_rows(o) * gout_ref[hd])
        y_ref[0] = jnp.concatenate(outs, axis=0).T.astype(BF16)


def _attn_prompt(qt, kt, vt, gout_a):
    b, _, _, s = qt.shape
    tq, tk = min(ATTN_Q_TILE, s), min(ATTN_K_TILE, s)
    assert s % tq == 0 and s % tk == 0
    steps = []
    for i in range(s // tq):
        j_last = ((i + 1) * tq - 1) // tk
        for j in range(j_last + 1):
            crosses_diagonal = (j + 1) * tk - 1 > i * tq
            steps.append((i, j, ATTN_LAST if j == j_last else ATTN_MASKED if crosses_diagonal else ATTN_FULL))
    qi_tab, ki_tab, kind_tab = (jnp.asarray([st[k] for st in steps], jnp.int32) for k in range(3))
    grid_spec = pltpu.PrefetchScalarGridSpec(
        num_scalar_prefetch=3,
        grid=(b, len(steps)),
        in_specs=[
            pl.BlockSpec((1, N_HEADS, HEAD_QK, tq), lambda bi, p, qi, ki, kind: (bi, 0, 0, qi[p])),
            pl.BlockSpec((1, N_HEADS, HEAD_QK, tk), lambda bi, p, qi, ki, kind: (bi, 0, 0, ki[p])),
            pl.BlockSpec((1, N_HEADS, V_DIM, tk), lambda bi, p, qi, ki, kind: (bi, 0, 0, ki[p])),
            pl.BlockSpec(gout_a.shape, lambda bi, p, qi, ki, kind: (0, 0, 0)),
        ],
        out_specs=pl.BlockSpec((1, tq, N_HEADS * V_DIM), lambda bi, p, qi, ki, kind: (bi, qi[p], 0)),
        scratch_shapes=[pltpu.VMEM((N_HEADS, 1, tq), F32), pltpu.VMEM((N_HEADS, 1, tq), F32),
                        pltpu.VMEM((N_HEADS, V_DIM, tq), F32)],
    )
    return pl.pallas_call(
        _attn_prompt_kernel,
        grid_spec=grid_spec,
        out_shape=jax.ShapeDtypeStruct((b, s, N_HEADS * V_DIM), BF16),
        compiler_params=_params(("arbitrary", "arbitrary")),
        name="attn_prompt",
    )(qi_tab, ki_tab, kind_tab, qt, kt, vt, gout_a)


def _proj_sample_kernel(x_ref, tpos_ref, st1_ref, st2_ref, gmix_ref, win_c_ref, win_at_ref, convw_ref,
                        convb_ref, gout_c_ref, gqlat_ref, wuqt_ref, gkv_ref, gqn_ref, gqr_ref, gkn_ref,
                        gkr_ref, wuk_ref, cos_ref, sin_ref,
                        yconv_ref, u_ref, qa_ref, qr_ref, ckv_ref, krope_ref, ext_ref):
    tm = x_ref.shape[0]
    c = convw_ref.shape[1]
    xf = x_ref[...]
    h = (xf * _rms_lanes(xf) * gmix_ref[...]).astype(BF16)

    zc = jnp.dot(h, win_c_ref[...], preferred_element_type=F32)
    u = zc[:, 2 * c:] * zc[:, :c]
    ext_ref[0:SUBLANES, :] = jnp.zeros((SUBLANES, c), F32)
    ext_ref[SUBLANES:, :] = u
    tpos = tpos_ref[...]
    u_m2 = jnp.where(tpos >= 2, ext_ref[pl.ds(SUBLANES - 2, tm), :], st2_ref[...])
    u_m1 = jnp.where(tpos >= 1, ext_ref[pl.ds(SUBLANES - 1, tm), :], st1_ref[...])
    v = convb_ref[...] + convw_ref[0:1, :] * u_m2 + convw_ref[1:2, :] * u_m1 + convw_ref[2:3, :] * u
    yconv = zc[:, c:2 * c] * v
    yconv_ref[...] = _group_norm_lanes(yconv, gout_c_ref[...]).astype(BF16)
    u_ref[...] = u

    q_nope, q_rope, ckvt, (k1, k2) = _attention_side(
        h, win_at_ref, gqlat_ref, wuqt_ref, gkv_ref, gqn_ref, gqr_ref, gkr_ref, cos_ref[...], sin_ref[...])
    for hd in range(N_HEADS):
        qg = (q_nope[hd] * gkn_ref[...]).astype(BF16)
        qa_t = jnp.dot(wuk_ref[:, hd * NOPE_DIM:(hd + 1) * NOPE_DIM], qg, preferred_element_type=F32)
        qa_ref[hd] = qa_t.T
        qr_ref[hd] = _to_token_major(jnp.concatenate(q_rope[hd], axis=0), ROPE_DIM)
    ckv_ref[...] = ckvt.T
    krope_ref[...] = _to_token_major(jnp.concatenate([k1, k2], axis=0), ROPE_DIM)


def _proj_sample(x, tpos, st1, st2, w, cos_t, sin_t):
    tm, d = x.shape
    c = w["conv_w"].shape[1]
    kv_rank = w["g_kv"].shape[0]
    args = [x, tpos, st1, st2, w["g_mix"], w["win_c"], w["win_at"], w["conv_w"], w["conv_b"], w["g_out_c"],
            w["g_qlat"], w["wuq_t"], w["g_kv"], w["g_qn"], w["g_qr"], w["g_kn"], w["g_kr"], w["wuk"],
            cos_t, sin_t]
    out_shape = [
        jax.ShapeDtypeStruct((tm, c), BF16),
        jax.ShapeDtypeStruct((tm, c), F32),
        jax.ShapeDtypeStruct((N_HEADS, tm, kv_rank), F32),
        jax.ShapeDtypeStruct((N_HEADS, tm, ROPE_DIM), F32),
        jax.ShapeDtypeStruct((tm, kv_rank), F32),
        jax.ShapeDtypeStruct((tm, ROPE_DIM), F32),
    ]
    return pl.pallas_call(
        _proj_sample_kernel,
        out_shape=out_shape,
        scratch_shapes=[pltpu.VMEM((tm + SUBLANES, c), F32)],
        compiler_params=pltpu.CompilerParams(vmem_limit_bytes=VMEM_LIMIT_BYTES),
        name="proj_sample",
    )(*args)


def _attn_decode_kernel(pt_ref, ckv_hbm, kr_hbm, qa_ref, qr_ref, cnew_ref, krnew_ref, wukt_ref, wuv_ref,
                        gout_ref, y_ref, cbuf, kbuf, sem, *, layer, n_pages, page, t_dec):
    b = pl.program_id(0)
    n_seq = pl.num_programs(0)
    chunk_pages = cbuf.shape[1] // page
    n_chunks = n_pages // chunk_pages
    rows = N_HEADS * t_dec

    def copies(first_page, slot):
        out = []
        for pg in range(chunk_pages):
            pid = 0 if first_page is None else pt_ref[first_page + pg]
            out.append(pltpu.make_async_copy(ckv_hbm.at[layer, pid], cbuf.at[slot, pl.ds(pg * page, page)], sem.at[0, slot]))
            out.append(pltpu.make_async_copy(kr_hbm.at[layer, pid], kbuf.at[slot, :, pl.ds(pg * page, page)],
                                             sem.at[1, slot]))
        return out

    qa = qa_ref[...].reshape(rows, qa_ref.shape[2]).astype(BF16)
    qr = qr_ref[...].reshape(rows, ROPE_DIM).astype(BF16)

    n_up = wukt_ref.shape[0]
    wq = jnp.concatenate([wukt_ref[...], qa], axis=0)

    def nope_scores(c_b):
        both = lax.dot_general(wq, c_b, _NT, preferred_element_type=F32)
        rs = []
        for hd in range(N_HEADS):
            blk = both[hd * NOPE_DIM:(hd + 1) * NOPE_DIM]
            rs.append(jnp.broadcast_to(_rms_rows(blk), (t_dec, blk.shape[1])))
        return both[n_up:] * jnp.concatenate(rs, axis=0)

    def update(carry, s, c_b):
        m_prev, l_prev, acc = carry
        m_new = jnp.maximum(m_prev, jnp.max(s, axis=-1, keepdims=True))
        alpha = jnp.exp2(m_prev - m_new)
        pr = jnp.exp2(s - m_new)
        l_new = alpha * l_prev + jnp.sum(pr, axis=-1, keepdims=True)
        acc = alpha * acc + jnp.dot(pr.astype(BF16), c_b, preferred_element_type=F32)
        return m_new, l_new, acc

    last = n_seq * n_chunks - 1
    g0 = b * n_chunks

    def fetch(g):
        for cp in copies(jnp.minimum(g, last) * chunk_pages, lax.rem(g, RING)):
            cp.start()

    def arrive(g):
        for cp in copies(None, lax.rem(g, RING)):
            cp.wait()

    def chunk_scores(g):
        slot = lax.rem(g, RING)
        s_rope = jnp.dot(qr, kbuf[slot].astype(BF16), preferred_element_type=F32)
        return nope_scores(cbuf[slot].astype(BF16)) + s_rope

    def chunk_update(carry, s, g):
        return update(carry, s, cbuf[lax.rem(g, RING)].astype(BF16))

    ahead = RING - 1

    @pl.when(b == 0)
    def _():
        for k in range(ahead):
            fetch(k)

    arrive(g0)
    s_first = chunk_scores(g0)

    def body(j, state):
        carry, s = state
        g = g0 + j
        fetch(g + ahead)
        arrive(g + 1)
        s_next = chunk_scores(g + 1)
        return chunk_update(carry, s, g), s_next

    init = (jnp.full((rows, 1), -jnp.inf, F32), jnp.zeros((rows, 1), F32), jnp.zeros((rows, cbuf.shape[2]), F32))
    carry, s_last = lax.fori_loop(0, n_chunks - 1, body, (init, s_first))
    fetch(g0 + n_chunks - 1 + ahead)
    carry = chunk_update(carry, s_last, g0 + n_chunks - 1)

    @pl.when(b == n_seq - 1)
    def _():
        for k in range(1, ahead + 1):
            arrive(last + k)

    pad = LANES - t_dec
    c_new = jnp.concatenate([cnew_ref[...], jnp.zeros((pad, cnew_ref.shape[1]), F32)], axis=0).astype(BF16)
    kr_new = jnp.concatenate([krnew_ref[...], jnp.zeros((pad, ROPE_DIM), F32)], axis=0).astype(BF16)
    s_new = nope_scores(c_new) + lax.dot_general(qr, kr_new, _NT, preferred_element_type=F32)
    q_t = lax.rem(lax.broadcasted_iota(jnp.int32, (rows, LANES), 0), t_dec)
    key = lax.broadcasted_iota(jnp.int32, (rows, LANES), 1)
    s_new = jnp.where(key <= q_t, s_new, MASK_VALUE)
    _, l_fin, acc = update(carry, s_new, c_new)

    o_lat = (acc / l_fin).astype(BF16)
    ov = jnp.dot(o_lat, wuv_ref[...], preferred_element_type=F32)
    outs = []
    for hd in range(N_HEADS):
        o = ov[hd * t_dec:(hd + 1) * t_dec, hd * V_DIM:(hd + 1) * V_DIM]
        outs.append(o * _rms_lanes(o) * gout_ref[hd])
    y_ref[...] = jnp.concatenate(outs, axis=1)


def _attn_decode(layer, page_table, cache_ckv, cache_kr, qa, qr, c_new, kr_new, w, t_dec):
    n_seq, n_pages = page_table.shape
    _, _, page, kv_rank = cache_ckv.shape
    chunk_pages = min(DECODE_CHUNK_PAGES, n_pages)
    assert n_pages % chunk_pages == 0 and t_dec % SUBLANES == 0 and t_dec <= LANES
    chunk = chunk_pages * page
    grid_spec = pltpu.PrefetchScalarGridSpec(
        num_scalar_prefetch=1,
        grid=(n_seq,),
        in_specs=[
            pl.BlockSpec(memory_space=pl.ANY),
            pl.BlockSpec(memory_space=pl.ANY),
            pl.BlockSpec((N_HEADS, t_dec, kv_rank), lambda b, pt: (0, b, 0)),
            pl.BlockSpec((N_HEADS, t_dec, ROPE_DIM), lambda b, pt: (0, b, 0)),
            pl.BlockSpec((t_dec, kv_rank), lambda b, pt: (b, 0)),
            pl.BlockSpec((t_dec, ROPE_DIM), lambda b, pt: (b, 0)),
            pl.BlockSpec(w["wuk_t"].shape, lambda b, pt: (0, 0)),
            pl.BlockSpec(w["wuv"].shape, lambda b, pt: (0, 0)),
            pl.BlockSpec(w["g_out_a_row"].shape, lambda b, pt: (0, 0, 0)),
        ],
        out_specs=pl.BlockSpec((t_dec, N_HEADS * V_DIM), lambda b, pt: (b, 0)),
        scratch_shapes=[pltpu.VMEM((RING, chunk, kv_rank), F32), pltpu.VMEM((RING, ROPE_DIM, chunk), F32),
                        pltpu.SemaphoreType.DMA((2, RING))],
    )
    return pl.pallas_call(
        functools.partial(_attn_decode_kernel, layer=layer, n_pages=n_pages, page=page, t_dec=t_dec),
        grid_spec=grid_spec,
        out_shape=jax.ShapeDtypeStruct((n_seq * t_dec, N_HEADS * V_DIM), F32),
        compiler_params=_params(("arbitrary",)),
        name="attn_decode",
    )(page_table.reshape(-1), cache_ckv, cache_kr, qa, qr, c_new, kr_new, w["wuk_t"], w["wuv"], w["g_out_a_row"])


ROUTE_E0, ROUTE_E1, ROUTE_R0, ROUTE_R1, ROUTE_W0, ROUTE_W1 = range(6)


def _lane_pick(x, lane, idx):
    return jnp.sum(jnp.where(lane == idx, x, 0.0), axis=-1, keepdims=True)


def _store_row_tiles(ref, x):
    rows, width = x.shape
    n = width // LANES
    for j in range(n):
        ref[pl.ds(j, rows, stride=n), :] = x[:, j * LANES:(j + 1) * LANES]


def _load_row_tiles(ref, rows, n):
    return jnp.concatenate([ref[pl.ds(j, rows, stride=n), :] for j in range(n)], axis=1)


def _merge_route_kernel(ycp_ref, yap_ref, xp_ref, ycs_ref, yas_ref, xs_ref, wout_ref, gffn_ref, wr_ref, br_ref,
                        xmid_ref, h2_ref, route_ref, counts_ref, carry_ref, *, n_prompt_tiles):
    i = pl.program_id(0)
    tm = xp_ref.shape[0]
    is_p = i < n_prompt_tiles

    @pl.when(i == 0)
    def _():
        carry_ref[...] = jnp.zeros(carry_ref.shape, F32)

    yc = jnp.where(is_p, ycp_ref[...], ycs_ref[...])
    ya = jnp.where(is_p, yap_ref[...], yas_ref[...].astype(BF16))
    x = jnp.where(is_p, xp_ref[...], xs_ref[...])
    y = jnp.concatenate([yc, ya], axis=1)
    xm = x + jnp.dot(y, wout_ref[...], preferred_element_type=F32)
    xmid_ref[...] = xm
    h2f = xm * _rms_lanes(xm) * gffn_ref[...]
    h2 = h2f.astype(BF16)
    h2_ref[...] = h2f.reshape(h2_ref.shape).astype(BF16)
    logits = jnp.dot(h2, wr_ref[...], preferred_element_type=F32) + br_ref[...]

    lt = logits.T
    epg = EXPERTS_PER_GROUP
    row = lax.broadcasted_iota(jnp.int32, (epg, tm), 0).astype(F32)
    neg = -jnp.inf
    far = float(epg)

    def first_max(x):
        v = jnp.max(x, axis=0, keepdims=True)
        return v, jnp.min(jnp.where(x == v, row, far), axis=0, keepdims=True)

    gl = jnp.where(row < N_GROUPS, lt[N_EXPERTS:N_EXPERTS + epg], neg)
    ge = jnp.exp(gl - jnp.max(gl, axis=0, keepdims=True))
    pg = ge / jnp.sum(ge, axis=0, keepdims=True)
    p_sel, g_sel = first_max(pg)
    v1 = i1 = v2 = i2 = None
    for g in range(N_GROUPS):
        eg = lt[g * epg:(g + 1) * epg]
        a1, j1 = first_max(eg)
        a2, j2 = first_max(jnp.where(row == j1, neg, eg))
        pick = g_sel == g
        v1, i1, v2, i2 = (c if g == 0 else jnp.where(pick, c, p) for c, p in ((a1, v1), (j1, i1), (a2, v2), (j2, i2)))
    e2 = jnp.exp(v2 - v1)
    w0 = 1.0 / (1.0 + e2) * p_sel
    w1 = e2 / (1.0 + e2) * p_sel
    e0 = g_sel * epg + i1
    e1 = g_sel * epg + i2

    erow = lax.broadcasted_iota(jnp.int32, (N_EXPERTS, tm), 0).astype(F32)
    oh0 = erow == e0
    oh1 = erow == e1
    onehot = (oh0 | oh1).astype(BF16)
    earlier = (lax.broadcasted_iota(jnp.int32, (tm, tm), 0) < lax.broadcasted_iota(jnp.int32, (tm, tm), 1)).astype(BF16)
    before = jnp.dot(onehot, earlier, preferred_element_type=F32) + carry_ref[...]
    r0 = jnp.sum(jnp.where(oh0, before, 0.0), axis=0, keepdims=True)
    r1 = jnp.sum(jnp.where(oh1, before, 0.0), axis=0, keepdims=True)
    carry_ref[...] += jnp.sum(onehot.astype(F32), axis=1, keepdims=True)

    zero = jnp.zeros((1, tm), F32)
    route_ref[...] = jnp.concatenate([e0, e1, r0, r1, w0, w1, zero, zero], axis=0)

    @pl.when(i == pl.num_programs(0) - 1)
    def _():
        counts_ref[...] = carry_ref[...]


def _merge_route(ycp, yap, xp, ycs, yas, xs, w):
    tp, d = xp.shape
    ts = xs.shape[0]
    tm = TOKEN_TILE
    assert tp % tm == 0 and ts % tm == 0
    npt, nst = tp // tm, ts // tm
    half = ycp.shape[1]
    pmap = lambda i: (jnp.minimum(i, npt - 1), 0)
    smap = lambda i: (jnp.maximum(i - npt, 0), 0)
    cmap = lambda i: (0, 0)
    return pl.pallas_call(
        functools.partial(_merge_route_kernel, n_prompt_tiles=npt),
        grid=(npt + nst,),
        in_specs=[
            pl.BlockSpec((tm, half), pmap), pl.BlockSpec((tm, half), pmap), pl.BlockSpec((tm, d), pmap),
            pl.BlockSpec((tm, half), smap), pl.BlockSpec((tm, half), smap), pl.BlockSpec((tm, d), smap),
            pl.BlockSpec(w["w_out"].shape, cmap), pl.BlockSpec(w["g_ffn"].shape, cmap),
            pl.BlockSpec(w["w_r"].shape, cmap), pl.BlockSpec(w["b_r"].shape, cmap),
        ],
        out_specs=[pl.BlockSpec((tm, d), lambda i: (i, 0)),
                   pl.BlockSpec((tm * (d // LANES), LANES), lambda i: (i, 0)),
                   pl.BlockSpec((SUBLANES, tm), lambda i: (0, i)),
                   pl.BlockSpec((N_EXPERTS, 1), cmap)],
        out_shape=[jax.ShapeDtypeStruct((tp + ts, d), F32),
                   jax.ShapeDtypeStruct(((tp + ts) * (d // LANES), LANES), BF16),
                   jax.ShapeDtypeStruct((SUBLANES, tp + ts), F32),
                   jax.ShapeDtypeStruct((N_EXPERTS, 1), F32)],
        scratch_shapes=[pltpu.VMEM((N_EXPERTS, 1), F32)],
        compiler_params=_params(("arbitrary",)),
        name="merge_route",
    )(ycp, yap, xp, ycs, yas, xs, w["w_out"], w["g_ffn"], w["w_r"], w["b_r"])


def _tile_of(ref, row, n):
    start = row * n
    return ref.at[pl.ds(start if isinstance(row, int) else pl.multiple_of(start, n), n)]


def _pad_fill_copies(base_ref, cnt_ref, tiles_ref, nu_ref, zeros_ref, xs_hbm, sem, tme, n, n_tiles):
    out = []
    for e in range(N_EXPERTS):
        pad = tiles_ref[e] * tme - cnt_ref[e]
        pos = base_ref[e] + cnt_ref[e]
        bit = tme // 2
        while bit >= 1:
            take = pad & bit
            out.append((take != 0, pltpu.make_async_copy(
                zeros_ref.at[pl.ds(0, bit * n)], xs_hbm.at[pl.ds(pl.multiple_of(pos * n, n), bit * n)], sem)))
            pos = pos + take
            bit //= 2
    for k in range(N_EXPERTS + 1):
        tile = nu_ref[0] + k
        out.append((tile < n_tiles, pltpu.make_async_copy(
            zeros_ref, xs_hbm.at[pl.ds(pl.multiple_of(jnp.minimum(tile, n_tiles - 1) * (tme * n), tme * n), tme * n)],
            sem)))
    return out


def _dispatch_kernel(base_ref, cnt_ref, tiles_ref, nu_ref, s0_ref, s1_ref, h2_ref, xs_hbm, stage, zeros, sem,
                     fill_sem, *, tm, tme, n, n_tiles):
    i = pl.program_id(0)
    last = pl.num_programs(0) - 1
    slot = lax.rem(i, 2)

    def drain(sl):
        for t in range(2 * tm):
            pltpu.make_async_copy(_tile_of(stage.at[sl], 0, n), _tile_of(xs_hbm, 0, n), sem.at[sl]).wait()

    @pl.when(i >= 2)
    def _():
        drain(slot)

    stage[slot] = h2_ref[...]
    for t0 in range(0, tm, SLOT_BATCH):
        slots = [(s0_ref[0, 0, t], s1_ref[0, 0, t]) for t in range(t0, t0 + SLOT_BATCH)]
        for t, pair in zip(range(t0, t0 + SLOT_BATCH), slots):
            for k, s in enumerate(pair):
                pltpu.make_async_copy(_tile_of(stage.at[slot], t, n), _tile_of(xs_hbm, s, n),
                                      sem.at[slot]).start(priority=k)

    def fills():
        return _pad_fill_copies(base_ref, cnt_ref, tiles_ref, nu_ref, zeros, xs_hbm, fill_sem, tme, n, n_tiles)

    @pl.when(i == 0)
    def _():
        zeros[...] = jnp.zeros(zeros.shape, zeros.dtype)
        for pred, cp in fills():
            pl.when(pred)(cp.start)
        for pred, cp in fills():
            pl.when(pred)(cp.wait)

    @pl.when((i == last) & (i >= 1))
    def _():
        drain(1 - slot)

    @pl.when(i == last)
    def _():
        drain(slot)


def _dispatch(plan, h2_tiles, n_slots):
    base, cnt, tiles, n_used, slot0, slot1 = plan
    n_tok_tiles, _, tm = slot0.shape
    n = h2_tiles.shape[0] // (n_tok_tiles * tm)
    tme = EXPERT_TILE
    n_tiles = n_slots // tme
    smem_blk = pl.BlockSpec((1, 1, tm), lambda i, *_: (i, 0, 0), memory_space=pltpu.SMEM)
    grid_spec = pltpu.PrefetchScalarGridSpec(
        num_scalar_prefetch=4,
        grid=(n_tok_tiles,),
        in_specs=[smem_blk] * 2 + [pl.BlockSpec((tm * n, LANES), lambda i, *_: (i, 0))],
        out_specs=pl.BlockSpec(memory_space=pl.ANY),
        scratch_shapes=[pltpu.VMEM((2, tm * n, LANES), h2_tiles.dtype), pltpu.VMEM((tme * n, LANES), h2_tiles.dtype),
                        pltpu.SemaphoreType.DMA((2,)), pltpu.SemaphoreType.DMA],
    )
    return pl.pallas_call(
        functools.partial(_dispatch_kernel, tm=tm, tme=tme, n=n, n_tiles=n_tiles),
        grid_spec=grid_spec,
        out_shape=jax.ShapeDtypeStruct((n_slots * n, LANES), h2_tiles.dtype),
        compiler_params=_params(("arbitrary",)),
        name="dispatch",
    )(base, cnt, tiles, n_used, slot0, slot1, h2_tiles)


def _experts_kernel(te_ref, nx_ref, nu_ref, xs_ref, wg_hbm, wu_hbm, wd_hbm, ys_ref, wg_buf, wu_buf, wd_buf, wgu_bf,
                    wd_bf, par_ref, sem, *, layer):
    i = pl.program_id(0)
    n_used = nu_ref[0]
    _, _, d_exp, d = wd_hbm.shape
    n = d // LANES
    tme = ys_ref.shape[0] // n
    expert = te_ref[i]
    new_expert = (i == 0) | (expert != te_ref[jnp.maximum(i - 1, 0)])

    def weight_copies(e, slot):
        return [pltpu.make_async_copy(src.at[layer, e], dst.at[slot], sem.at[slot])
                for src, dst in ((wg_hbm, wg_buf), (wu_hbm, wu_buf), (wd_hbm, wd_buf))]

    @pl.when(i == 0)
    def _():
        par_ref[0] = 0
        for cp in weight_copies(expert, 0):
            cp.start()

    @pl.when(new_expert & (i < n_used))
    def _():
        cur = par_ref[0]
        for cp in weight_copies(0, cur):
            cp.wait()
        nxt = nx_ref[i]

        @pl.when(nxt >= 0)
        def _():
            for cp in weight_copies(nxt, 1 - cur):
                cp.start()

        wgu_bf[:, :d_exp] = wg_buf[cur].astype(BF16)
        wgu_bf[:, d_exp:] = wu_buf[cur].astype(BF16)
        wd_bf[...] = wd_buf[cur].astype(BF16)
        par_ref[0] = 1 - cur

    @pl.when(i < n_used)
    def _():
        h2 = xs_ref[...].astype(F32).reshape(tme, d).astype(BF16)
        gu = jnp.dot(h2, wgu_bf[...], preferred_element_type=F32)
        g = gu[:, :d_exp]
        a = (g / (1.0 + jnp.exp(-g))) * gu[:, d_exp:]
        _store_row_tiles(ys_ref, jnp.dot(a.astype(BF16), wd_bf[...], preferred_element_type=F32))

    @pl.when(i >= n_used)
    def _():
        ys_ref[...] = jnp.zeros(ys_ref.shape, F32)


def _experts(layer, tile_expert, n_used, xs_tiles, w_gate, w_up, w_down):
    n_tiles = tile_expert.shape[0]
    _, _, d, d_exp = w_gate.shape
    blk = xs_tiles.shape[0] // n_tiles
    idx = jnp.arange(n_tiles, dtype=jnp.int32)
    later_other = (tile_expert[None, :] != tile_expert[:, None]) & (idx[None, :] > idx[:, None])
    first_other = jnp.min(jnp.where(later_other, idx[None, :], n_tiles), axis=1)
    next_expert = jnp.where(first_other < n_tiles,
                            jnp.sum(jnp.where(idx[None, :] == first_other[:, None], tile_expert[None, :], 0), axis=1), -1)
    grid_spec = pltpu.PrefetchScalarGridSpec(
        num_scalar_prefetch=3,
        grid=(n_tiles,),
        in_specs=[pl.BlockSpec((blk, LANES), lambda i, te, nx, nu: (jnp.minimum(i, nu[0] - 1), 0))]
        + [pl.BlockSpec(memory_space=pl.ANY)] * 3,
        out_specs=pl.BlockSpec((blk, LANES), lambda i, te, nx, nu: (i, 0)),
        scratch_shapes=[pltpu.VMEM((2, d, d_exp), F32), pltpu.VMEM((2, d, d_exp), F32), pltpu.VMEM((2, d_exp, d), F32),
                        pltpu.VMEM((d, 2 * d_exp), BF16), pltpu.VMEM((d_exp, d), BF16),
                        pltpu.SMEM((1,), jnp.int32), pltpu.SemaphoreType.DMA((2,))],
    )
    return pl.pallas_call(
        functools.partial(_experts_kernel, layer=layer),
        grid_spec=grid_spec,
        out_shape=jax.ShapeDtypeStruct((n_tiles * blk, LANES), F32),
        compiler_params=_params(("arbitrary",)),
        name="experts",
    )(tile_expert, next_expert.astype(jnp.int32), n_used, xs_tiles, w_gate, w_up, w_down)


def _combine_kernel(*refs, n_prompt_tiles):
    tables = [refs[2 * a:2 * a + 2] for a in range(COMBINE_AHEAD + 1)]
    ys_hbm, xmid_ref, route_ref, yp_ref, ysmp_ref, buf, sem = refs[2 * (COMBINE_AHEAD + 1):]
    ring = COMBINE_AHEAD + 1
    i = pl.program_id(0)
    last = pl.num_programs(0) - 1
    tm, d = xmid_ref.shape
    n = d // LANES
    slot = lax.rem(i, ring)

    def gather(refs, dst_slot):
        s0, s1 = refs
        for t0 in range(0, tm, SLOT_BATCH):
            slots = [(s0[0, 0, t], s1[0, 0, t]) for t in range(t0, t0 + SLOT_BATCH)]
            for t, pair in zip(range(t0, t0 + SLOT_BATCH), slots):
                for k, s in enumerate(pair):
                    pltpu.make_async_copy(_tile_of(ys_hbm, s, n), _tile_of(buf.at[dst_slot, k], t, n),
                                          sem.at[dst_slot]).start(priority=k)

    def gather_wait(dst_slot):
        for t in range(2 * tm):
            pltpu.make_async_copy(_tile_of(ys_hbm, 0, n), _tile_of(buf.at[dst_slot, 0], 0, n), sem.at[dst_slot]).wait()

    @pl.when(i == 0)
    def _():
        for a in range(COMBINE_AHEAD):
            gather(tables[a], a)

    gather(tables[COMBINE_AHEAD], lax.rem(i + COMBINE_AHEAD, ring))
    gather_wait(slot)

    rt = route_ref[...]
    cols = jnp.concatenate([rt, jnp.zeros((LANES - rt.shape[0], tm), F32)], axis=0).T
    lane = lax.broadcasted_iota(jnp.int32, cols.shape, 1)
    w0 = _lane_pick(cols, lane, ROUTE_W0)
    w1 = _lane_pick(cols, lane, ROUTE_W1)
    y0 = _load_row_tiles(buf.at[slot, 0], tm, n)
    y1 = _load_row_tiles(buf.at[slot, 1], tm, n)
    out = xmid_ref[...] + (w0 * y0 + w1 * y1)

    @pl.when(i < n_prompt_tiles)
    def _():
        yp_ref[...] = out

    @pl.when(i >= n_prompt_tiles)
    def _():
        ysmp_ref[...] = out

    @pl.when(i == last)
    def _():
        for a in range(1, COMBINE_AHEAD + 1):
            gather_wait(lax.rem(i + a, ring))


def _combine(plan, ys_tiles, xmid, route_t, tp):
    slot0, slot1 = plan[4:]
    ttot, d = xmid.shape
    n_tok_tiles, _, tm = slot0.shape
    assert n_tok_tiles >= COMBINE_AHEAD
    npt = tp // tm
    n = d // LANES

    def table(a):
        return pl.BlockSpec((1, 1, tm), lambda i: (jnp.minimum(i + a, n_tok_tiles - 1), 0, 0), memory_space=pltpu.SMEM)

    return pl.pallas_call(
        functools.partial(_combine_kernel, n_prompt_tiles=npt),
        grid=(n_tok_tiles,),
        in_specs=[table(a) for a in range(COMBINE_AHEAD + 1) for _ in range(2)] + [
                  pl.BlockSpec(memory_space=pl.ANY),
                  pl.BlockSpec((tm, d), lambda i: (i, 0)),
                  pl.BlockSpec((SUBLANES, tm), lambda i: (0, i))],
        out_specs=[pl.BlockSpec((tm, d), lambda i: (jnp.minimum(i, npt - 1), 0)),
                   pl.BlockSpec((tm, d), lambda i: (jnp.maximum(i - npt, 0), 0))],
        out_shape=[jax.ShapeDtypeStruct((tp, d), F32), jax.ShapeDtypeStruct((ttot - tp, d), F32)],
        scratch_shapes=[pltpu.VMEM((COMBINE_AHEAD + 1, 2, tm * n, LANES), F32),
                        pltpu.SemaphoreType.DMA((COMBINE_AHEAD + 1,))],
        compiler_params=_params(("arbitrary",)),
        name="combine",
    )(*([slot0, slot1] * (COMBINE_AHEAD + 1)), ys_tiles, xmid, route_t)


def _rope_tables(pos):
    inv_freq = ROPE_BASE ** (-jnp.arange(HALF_ROPE, dtype=F32) / HALF_ROPE)
    ang = pos.astype(F32)[:, None] * inv_freq[None, :]
    return jnp.cos(ang).T, jnp.sin(ang).T


def _layer_weights(l, g_mix, w_in, conv_w, conv_b, g_q_lat, w_uq, g_kv_lat, w_uk, w_uv, g_q_nope, g_q_rope,
                   g_k_nope, g_k_rope, g_out, w_out, g_ffn, w_router_group, b_router_group, w_router_expert,
                   b_router_expert, w_gate, w_up, w_down):
    c = conv_w.shape[2]
    col = lambda g: g[l].reshape(-1, 1)
    w_r = jnp.concatenate([w_router_expert[l], w_router_group[l]], axis=1)
    b_r = jnp.concatenate([b_router_expert[l], b_router_group[l]])
    return {
        "g_mix": g_mix[l].reshape(1, -1),
        "win_c": w_in[l][:, :3 * c].astype(BF16),
        "win_at": w_in[l][:, 3 * c:].T.astype(BF16),
        "conv_w": conv_w[l],
        "conv_b": conv_b[l].reshape(1, -1),
        "g_out_c": g_out[l][:c].reshape(1, -1),
        "g_out_a": g_out[l][c:].reshape(N_HEADS, V_DIM, 1),
        "g_out_a_row": g_out[l][c:].reshape(N_HEADS, 1, V_DIM),
        "g_qlat": col(g_q_lat), "g_kv": col(g_kv_lat), "g_qn": col(g_q_nope), "g_qr": col(g_q_rope),
        "g_kn": col(g_k_nope), "g_kr": col(g_k_rope),
        "wuq_t": w_uq[l].T.astype(BF16),
        "wuk_t": w_uk[l].T.astype(BF16),
        "wuk": w_uk[l].astype(BF16),
        "wuv_t": w_uv[l].T.astype(BF16),
        "wuv": w_uv[l].astype(BF16),
        "w_out": w_out[l].astype(BF16),
        "g_ffn": g_ffn[l].reshape(1, -1),
        "w_r": jnp.pad(w_r, ((0, 0), (0, LANES - w_r.shape[1]))).astype(BF16),
        "b_r": jnp.pad(b_r, (0, LANES - b_r.shape[0])).reshape(1, -1),
    }


def _moe_plan(route_t, counts):
    ttot = route_t.shape[1]
    tme = EXPERT_TILE
    tm = TOKEN_TILE
    n_tiles = (TOP_K * ttot) // tme + N_EXPERTS + 1
    cnt = counts[:, 0].astype(jnp.int32)
    tiles = (cnt + tme - 1) // tme
    tile_end = jnp.cumsum(tiles)
    base = (tile_end - tiles) * tme
    n_used = tile_end[-1:]
    tile_id = jnp.minimum(jnp.arange(n_tiles, dtype=jnp.int32), n_used[0] - 1)
    tile_expert = jnp.sum((tile_end[None, :] <= tile_id[:, None]).astype(jnp.int32), axis=1)
    ids = route_t[:ROUTE_R1 + 1].astype(jnp.int32)
    expert_ids = jnp.arange(N_EXPERTS, dtype=jnp.int32)[:, None]

    def slots(e, r):
        s = r + jnp.sum(jnp.where(e[None, :] == expert_ids, base[:, None], 0), axis=0)
        return s.reshape(ttot // tm, 1, tm)

    plan = (base, cnt, tiles, n_used, slots(ids[ROUTE_E0], ids[ROUTE_R0]), slots(ids[ROUTE_E1], ids[ROUTE_R1]))
    return plan, tile_expert, n_tiles * tme


def kernel(x_prompt, x_sample, state_conv, cache_ckv, cache_krope, page_table, g_mix, w_in, conv_w, conv_b, g_q_lat, w_uq, g_kv_lat, w_uk, w_uv, g_q_nope, g_q_rope, g_k_nope, g_k_rope, g_out, w_out, g_ffn, w_router_group, b_router_group, w_router_expert, b_router_expert, w_gate, w_up, w_down):
    b_p, s_p, d = x_prompt.shape
    b_s, t_s, _ = x_sample.shape
    depth = g_mix.shape[0]
    c = conv_w.shape[2]
    page = cache_ckv.shape[2]
    past_len = page_table.shape[1] * page
    kv_rank = cache_ckv.shape[3]

    cos_p, sin_p = _rope_tables(jnp.arange(s_p, dtype=jnp.int32))
    cos_s, sin_s = _rope_tables(jnp.tile(past_len + jnp.arange(t_s, dtype=jnp.int32), b_s))
    tpos = jnp.tile(jnp.arange(t_s, dtype=jnp.int32), b_s).reshape(-1, 1)

    krope_pages = jnp.swapaxes(cache_krope, 2, 3)

    xp, xs = x_prompt, x_sample.reshape(b_s * t_s, d)
    outs = [[] for _ in range(6)]
    for l in range(depth):
        w = _layer_weights(l, g_mix, w_in, conv_w, conv_b, g_q_lat, w_uq, g_kv_lat, w_uk, w_uv, g_q_nope,
                           g_q_rope, g_k_nope, g_k_rope, g_out, w_out, g_ffn, w_router_group, b_router_group,
                           w_router_expert, b_router_expert, w_gate, w_up, w_down)
        yconv_p, qt, kt, vt, ckv_p, kr_p, conv_p = _proj_prompt(xp, w, cos_p, sin_p)
        yattn_p = _attn_prompt(qt, kt, vt, w["g_out_a"])
        st = state_conv[l]
        zeros = lambda n: jnp.zeros((b_s, n, c), F32)
        st1 = jnp.concatenate([st[:, CONV_K - 2:], zeros(t_s - 1)], axis=1).reshape(b_s * t_s, c)
        st2 = jnp.concatenate([st, zeros(t_s - (CONV_K - 1))], axis=1).reshape(b_s * t_s, c)
        yconv_s, u_s, qa, qr, ckv_s, kr_s = _proj_sample(xs, tpos, st1, st2, w, cos_s, sin_s)
        yattn_s = _attn_decode(l, page_table, cache_ckv, krope_pages, qa, qr, ckv_s, kr_s, w, t_s)
        xmid, h2_tiles, route_t, counts = _merge_route(
            yconv_p.reshape(b_p * s_p, c), yattn_p.reshape(b_p * s_p, -1), xp.reshape(b_p * s_p, d),
            yconv_s, yattn_s, xs, w)
        plan, tile_expert, n_slots = _moe_plan(route_t, counts)
        xs_tiles = _dispatch(plan, h2_tiles, n_slots)
        ys_tiles = _experts(l, tile_expert, plan[3], xs_tiles, w_gate, w_up, w_down)
        yp, ysmp = _combine(plan, ys_tiles, xmid, route_t, b_p * s_p)
        xp, xs = yp.reshape(b_p, s_p, d), ysmp
        for lst, val in zip(outs, (ckv_p, jnp.swapaxes(kr_p, 1, 2), conv_p, ckv_s.reshape(b_s, t_s, kv_rank),
                                   kr_s.reshape(b_s, t_s, ROPE_DIM),
                                   u_s.reshape(b_s, t_s, c)[:, t_s - (CONV_K - 1):])):
            lst.append(val)
    return (xp, xs.reshape(b_s, t_s, d)) + tuple(jnp.stack(o) for o in outs)
```
